```python
import jax, jax.numpy as jnp
from jax import lax
import numpy as np

D_MODEL = 1024
BATCH = 2
SEQ = 8192
DEPTH = 2

GRID_W = 64
POOL_W = D_MODEL // 4
N_POOL_GROUPS = 4
POOL_GROUP_W = POOL_W // N_POOL_GROUPS
POOL_WINDOWS = (2, 4, 8, 16)
NA_HEAD_DIM = 64
NA_W = D_MODEL // 2
NA_HEADS = NA_W // NA_HEAD_DIM
NA_ROWS = 8
NA_COLS = 16
NA_QR = 2
NA_QC = 16
CONV_W = D_MODEL // 4
CONV_K = 31
D_MIX = POOL_W + NA_W + CONV_W
OFF_POOL = 0
OFF_Q = OFF_POOL + POOL_W
OFF_K = OFF_Q + NA_W
OFF_V = OFF_K + NA_W
OFF_CA = OFF_V + NA_W
OFF_CG = OFF_CA + CONV_W
D_IN = OFF_CG + CONV_W
N_EXPERTS = 32
TOP_K = 4
D_FF = D_MODEL
SWIGLU_ALPHA = 1.702
SWIGLU_LIMIT = 7.0
DEEPNORM_ALPHA = (2.0 * DEPTH) ** 0.25
DEEPNORM_BETA = (8.0 * DEPTH) ** -0.25
LN_EPS = 1e-5
NEG_INF = -1e30

kernel_name = "hybrid_pool_natten_conformer_moe_deepnorm"


def _layer_norm(x, g, b):
    xf = x.astype(jnp.float32)
    mu = jnp.mean(xf, axis=-1, keepdims=True)
    var = jnp.mean(jnp.square(xf - mu), axis=-1, keepdims=True)
    y = (xf - mu) * lax.rsqrt(var + LN_EPS)
    return (y * g + b).astype(x.dtype)


def _pool_mixer(u, w_pool, scale):
    B, S, C = u.shape
    uf = u.astype(jnp.float32)
    cs = jnp.concatenate([jnp.zeros((B, 1, C), jnp.float32), jnp.cumsum(uf, axis=1)], axis=1)
    t = np.arange(S)
    outs = []
    for g, w in enumerate(POOL_WINDOWS):
        sl = slice(g * POOL_GROUP_W, (g + 1) * POOL_GROUP_W)
        lo = np.clip(t - w // 2, 0, S)
        hi = np.clip(t + w // 2, 0, S)
        cnt = (hi - lo).astype(np.float32)
        mean = (cs[:, hi, sl] - cs[:, lo, sl]) / cnt[None, :, None]
        d = (mean - uf[..., sl]).astype(u.dtype)
        outs.append(jnp.einsum('bsc,cd->bsd', d, w_pool[g]))
    return jnp.concatenate(outs, axis=-1) * scale


def _na_tables(rows):
    kr = min(NA_ROWS, rows)
    band_r = min(rows, kr + NA_QR - 1)
    band_c = min(GRID_W, NA_COLS + NA_QC - 1)
    r0 = np.arange(0, rows, NA_QR)
    c0 = np.arange(0, GRID_W, NA_QC)
    br = np.clip(r0 - kr // 2, 0, rows - band_r)
    bc = np.clip(c0 - NA_COLS // 2, 0, GRID_W - band_c)
    qr, qc = np.broadcast_arrays(r0[:, None, None, None] + np.arange(NA_QR)[None, None, :, None],
                                 c0[None, :, None, None] + np.arange(NA_QC)[None, None, None, :])
    qr = qr.reshape(-1, NA_QR * NA_QC)
    qc = qc.reshape(-1, NA_QR * NA_QC)
    kr_, kc_ = np.broadcast_arrays(br[:, None, None, None] + np.arange(band_r)[None, None, :, None],
                                   bc[None, :, None, None] + np.arange(band_c)[None, None, None, :])
    kr_ = kr_.reshape(-1, band_r * band_c)
    kc_ = kc_.reshape(-1, band_r * band_c)
    sr = np.clip(qr - kr // 2, 0, rows - kr)[..., None]
    sc = np.clip(qc - NA_COLS // 2, 0, GRID_W - NA_COLS)[..., None]
    krk = kr_[:, None, :]
    kck = kc_[:, None, :]
    mask = (krk >= sr) & (krk < sr + kr) & (kck >= sc) & (kck < sc + NA_COLS)
    r_off = np.clip(krk - qr[..., None] + NA_ROWS - 1, 0, 2 * NA_ROWS - 2)
    c_off = np.clip(kck - qc[..., None] + NA_COLS - 1, 0, 2 * NA_COLS - 2)
    q_idx = qr * GRID_W + qc
    k_idx = kr_ * GRID_W + kc_
    inv = np.argsort(q_idx.reshape(-1))
    return q_idx, k_idx, r_off, c_off, mask, inv


def _neighbourhood_attention(q, k, v, rpb):
    B, S, H, Dh = q.shape
    rows = S // GRID_W
    q_idx, k_idx, r_off, c_off, mask, inv = _na_tables(rows)
    qb = jnp.take(q, q_idx, axis=1)
    kb = jnp.take(k, k_idx, axis=1)
    vb = jnp.take(v, k_idx, axis=1)
    bias = jnp.where(mask, rpb[:, r_off, c_off].astype(jnp.float32), NEG_INF)
    s = jnp.einsum('bnqhd,bnkhd->bhnqk', qb, kb).astype(jnp.float32) * (Dh ** -0.5) + bias
    p = jax.nn.softmax(s, axis=-1).astype(v.dtype)
    o = jnp.einsum('bhnqk,bnkhd->bnqhd', p, vb).reshape(B, S, H * Dh)
    return jnp.take(o, inv, axis=1)


def _conv_module(a, gate, w_dw, b_dw, ln_g, ln_b, w_pw, b_pw):
    h = a * jax.nn.sigmoid(gate)
    h = lax.conv_general_dilated(h, w_dw, window_strides=(1,),
                                 padding=[(CONV_K // 2, CONV_K // 2)],
                                 dimension_numbers=('NWC', 'WIO', 'NWC'),
                                 feature_group_count=h.shape[-1]) + b_dw
    h = jax.nn.silu(_layer_norm(h, ln_g, ln_b))
    return h @ w_pw + b_pw


def _moe(h, w_router, b_router, w_gate, b_gate, w_up, b_up, w_down, b_down):
    B, S, D = h.shape
    hf = h.reshape(-1, D)
    logits = (hf @ w_router + b_router).astype(jnp.float32)
    top_v, top_i = lax.top_k(logits, TOP_K)
    probs = jax.nn.softmax(top_v, axis=-1)
    gates = jnp.sum(jax.nn.one_hot(top_i, N_EXPERTS, dtype=jnp.float32) * probs[..., None], axis=1).astype(h.dtype)
    out = jnp.zeros_like(hf)
    for e in range(N_EXPERTS):
        g = jnp.minimum(hf @ w_gate[e] + b_gate[e], SWIGLU_LIMIT)
        u = jnp.clip(hf @ w_up[e] + b_up[e], -SWIGLU_LIMIT, SWIGLU_LIMIT)
        act = (u + 1.0) * g * jax.nn.sigmoid(SWIGLU_ALPHA * g)
        out = out + gates[:, e:e + 1] * (act @ w_down[e] + b_down[e])
    return out.reshape(B, S, D)


def setup_inputs(seed: int = 0) -> dict:
    key = jax.random.key(seed)
    ks = jax.random.split(key, 26)

    def nrm(k, shape, scale):
        return jax.random.normal(k, shape, jnp.float32) * scale

    col_scale = np.ones((D_IN,), np.float32)
    col_scale[OFF_V:OFF_V + NA_W] = DEEPNORM_BETA
    return {
        "x": nrm(ks[0], (BATCH, SEQ, D_MODEL), 1.0),
        "w_in": nrm(ks[1], (DEPTH, D_MODEL, D_IN), D_MODEL ** -0.5) * jnp.asarray(col_scale),
        "b_in": nrm(ks[2], (DEPTH, D_IN), 0.02),
        "w_pool": nrm(ks[3], (DEPTH, N_POOL_GROUPS, POOL_GROUP_W, POOL_GROUP_W), POOL_GROUP_W ** -0.5),
        "pool_scale": 1.0 + nrm(ks[4], (DEPTH, POOL_W), 0.02),
        "rpb": nrm(ks[5], (DEPTH, NA_HEADS, 2 * NA_ROWS - 1, 2 * NA_COLS - 1), 0.1),
        "conv_dw": nrm(ks[6], (DEPTH, CONV_K, 1, CONV_W), CONV_K ** -0.5),
        "conv_dw_b": nrm(ks[7], (DEPTH, CONV_W), 0.02),
        "conv_ln_g": 1.0 + nrm(ks[8], (DEPTH, CONV_W), 0.02),
        "conv_ln_b": nrm(ks[9], (DEPTH, CONV_W), 0.02),
        "w_conv_pw": nrm(ks[10], (DEPTH, CONV_W, CONV_W), CONV_W ** -0.5),
        "b_conv_pw": nrm(ks[11], (DEPTH, CONV_W), 0.02),
        "w_out": nrm(ks[12], (DEPTH, D_MIX, D_MODEL), (D_MIX ** -0.5) * DEEPNORM_BETA),
        "b_out": nrm(ks[13], (DEPTH, D_MODEL), 0.02),
        "ln1_g": 1.0 + nrm(ks[14], (DEPTH, D_MODEL), 0.02),
        "ln1_b": nrm(ks[15], (DEPTH, D_MODEL), 0.02),
        "w_router": nrm(ks[16], (DEPTH, D_MODEL, N_EXPERTS), D_MODEL ** -0.5),
        "b_router": nrm(ks[17], (DEPTH, N_EXPERTS), 0.01),
        "w_gate": nrm(ks[18], (DEPTH, N_EXPERTS, D_MODEL, D_FF), D_MODEL ** -0.5),
        "b_gate": nrm(ks[19], (DEPTH, N_EXPERTS, D_FF), 0.02),
        "w_up": nrm(ks[20], (DEPTH, N_EXPERTS, D_MODEL, D_FF), D_MODEL ** -0.5),
        "b_up": nrm(ks[21], (DEPTH, N_EXPERTS, D_FF), 0.02),
        "w_down": nrm(ks[22], (DEPTH, N_EXPERTS, D_FF, D_MODEL), (D_FF ** -0.5) * DEEPNORM_BETA),
        "b_down": nrm(ks[23], (DEPTH, N_EXPERTS, D_MODEL), 0.02),
        "ln2_g": 1.0 + nrm(ks[24], (DEPTH, D_MODEL), 0.02),
        "ln2_b": nrm(ks[25], (DEPTH, D_MODEL), 0.02),
    }


def reference(x, w_in, b_in, w_pool, pool_scale, rpb, conv_dw, conv_dw_b, conv_ln_g, conv_ln_b,
              w_conv_pw, b_conv_pw, w_out, b_out, ln1_g, ln1_b, w_router, b_router,
              w_gate, b_gate, w_up, b_up, w_down, b_down, ln2_g, ln2_b):
    B, S, _ = x.shape
    for l in range(DEPTH):
        z = x @ w_in[l] + b_in[l]
        q = z[..., OFF_Q:OFF_K].reshape(B, S, NA_HEADS, NA_HEAD_DIM)
        k = z[..., OFF_K:OFF_V].reshape(B, S, NA_HEADS, NA_HEAD_DIM)
        v = z[..., OFF_V:OFF_CA].reshape(B, S, NA_HEADS, NA_HEAD_DIM)
        y_pool = _pool_mixer(z[..., OFF_POOL:OFF_Q], w_pool[l], pool_scale[l])
        y_attn = _neighbourhood_attention(q, k, v, rpb[l])
        y_conv = _conv_module(z[..., OFF_CA:OFF_CG], z[..., OFF_CG:D_IN], conv_dw[l], conv_dw_b[l],
                              conv_ln_g[l], conv_ln_b[l], w_conv_pw[l], b_conv_pw[l])
        mix = jnp.concatenate([y_pool, y_attn, y_conv], axis=-1) @ w_out[l] + b_out[l]
        x = _layer_norm(DEEPNORM_ALPHA * x + mix, ln1_g[l], ln1_b[l])
        ffn = _moe(x, w_router[l], b_router[l], w_gate[l], b_gate[l], w_up[l], b_up[l], w_down[l], b_down[l])
        x = _layer_norm(DEEPNORM_ALPHA * x + ffn, ln2_g[l], ln2_b[l])
    return x
```

```python
import functools

import jax
import jax.numpy as jnp
import numpy as np
from jax import lax
from jax.experimental import pallas as pl
from jax.experimental.pallas import tpu as pltpu

F32 = jnp.float32
BF16 = jnp.bfloat16
I32 = jnp.int32

D_MODEL = 1024
GRID_W = 64
POOL_W = 256
POOL_WINDOWS = (2, 4, 8, 16)
POOL_GROUP_W = 64
NA_W = 512
NA_HEADS = 8
NA_HEAD_DIM = 64
NA_ROWS = 8
NA_COLS = 16
CONV_W = 256
CONV_K = 31
N_EXPERTS = 32
TOP_K = 4
SWIGLU_ALPHA = 1.702
SWIGLU_LIMIT = 7.0
LN_EPS = 1e-5
NEG_INF = -1e30

LANES = 128
TM_IN = 512
TQ = 256
TQ_ROWS = TQ // GRID_W
NKEY = 3 * TQ
HALO = 16
TM_E = 512
VMEM_LIMIT = 56 * 1024 * 1024


def _layer_norm(v, g, b):
    mu = jnp.mean(v, axis=-1, keepdims=True)
    c = v - mu
    var = jnp.mean(c * c, axis=-1, keepdims=True)
    return c * lax.rsqrt(var + LN_EPS) * g + b


def _inproj_kernel(x_ref, wqkv_ref, bqkv_ref, wrest_ref, brest_ref, qkv_ref, rest_ref):
    x = x_ref[...].astype(BF16)
    qkv = jnp.dot(x, wqkv_ref[...], preferred_element_type=F32) + bqkv_ref[...]
    qkv_ref[...] = qkv.astype(BF16)
    rest_ref[...] = jnp.dot(x, wrest_ref[...], preferred_element_type=F32) + brest_ref[...]


def _inproj(x, wqkv, bqkv, wrest, brest):
    n = x.shape[0]
    return pl.pallas_call(
        _inproj_kernel,
        grid=(n // TM_IN,),
        in_specs=[
            pl.BlockSpec((TM_IN, D_MODEL), lambda i: (i, 0)),
            pl.BlockSpec((D_MODEL, 3 * NA_W), lambda i: (0, 0)),
            pl.BlockSpec((1, 3 * NA_W), lambda i: (0, 0)),
            pl.BlockSpec((D_MODEL, POOL_W + 2 * CONV_W), lambda i: (0, 0)),
            pl.BlockSpec((1, POOL_W + 2 * CONV_W), lambda i: (0, 0)),
        ],
        out_specs=[
            pl.BlockSpec((TM_IN, 3 * NA_W), lambda i: (i, 0)),
            pl.BlockSpec((TM_IN, POOL_W + 2 * CONV_W), lambda i: (i, 0)),
        ],
        out_shape=[
            jax.ShapeDtypeStruct((n, 3 * NA_W), BF16),
            jax.ShapeDtypeStruct((n, POOL_W + 2 * CONV_W), F32),
        ],
        compiler_params=pltpu.CompilerParams(
            dimension_semantics=("arbitrary",), vmem_limit_bytes=VMEM_LIMIT),
        name="inproj",
    )(x, wqkv, bqkv, wrest, brest)


def _attn_bias_tables(rpb, rows):
    kr_win = min(NA_ROWS, rows)
    n_tiles = rows // TQ_ROWS
    tabs = []
    for tile in (0, 1, n_tiles - 1):
        qr = tile * TQ_ROWS + np.arange(TQ) // GRID_W
        qc = np.arange(TQ) % GRID_W
        slot_tile = np.array([tile - 1, tile, tile + 1])
        slot_ok = (slot_tile >= 0) & (slot_tile < n_tiles)
        kr = (slot_tile[:, None] * TQ_ROWS + np.arange(TQ)[None, :] // GRID_W).reshape(-1)
        kc = np.tile(np.arange(TQ) % GRID_W, 3)
        k_ok = np.repeat(slot_ok, TQ)
        sr = np.clip(qr - kr_win // 2, 0, rows - kr_win)[:, None]
        sc = np.clip(qc - NA_COLS // 2, 0, GRID_W - NA_COLS)[:, None]
        mask = (kr[None] >= sr) & (kr[None] < sr + kr_win) & (kc[None] >= sc) & (kc[None] < sc + NA_COLS)
        mask = mask & k_ok[None]
        r_off = np.clip(kr[None] - qr[:, None] + NA_ROWS - 1, 0, 2 * NA_ROWS - 2)
        c_off = np.clip(kc[None] - qc[:, None] + NA_COLS - 1, 0, 2 * NA_COLS - 2)
        bias = jnp.where(mask[None], rpb[:, r_off, c_off].astype(F32), NEG_INF)
        tabs.append(bias.reshape(NA_HEADS // 2, 2 * TQ, NKEY))
    return jnp.stack(tabs)


def _mixer_kernel(tiles_per_seq, alpha,
                  x_ref, q_ref, kp_ref, kc_ref, kn_ref, vp_ref, vc_ref, vn_ref,
                  rc_ref, rp_ref, rn_ref, bias_ref,
                  wpool_ref, pscale_ref, dw_ref, dwb_ref, clng_ref, clnb_ref, wpw_ref, bpw_ref,
                  wout_ref, bout_ref, lng_ref, lnb_ref, wr_ref, br_ref, tri_ref,
                  x1_ref, topi_ref, topp_ref, rank_ref, counts_ref,
                  halo_ref, base_ref):
    i = pl.program_id(0)
    ib = i % tiles_per_seq
    has_prev = ib > 0
    has_next = ib < tiles_per_seq - 1

    @pl.when(i == 0)
    def _():
        base_ref[...] = jnp.zeros_like(base_ref)

    halo_ref[0:HALO, :] = jnp.where(has_prev, rp_ref[...], 0.0)
    halo_ref[HALO:HALO + TQ, :] = rc_ref[...]
    halo_ref[HALO + TQ:, :] = jnp.where(has_next, rn_ref[...], 0.0)

    def u_at(off):
        return halo_ref[pl.ds(HALO + off, TQ), 0:POOL_W]

    t_seq = ib * TQ + lax.broadcasted_iota(I32, (TQ, 1), 0)
    seq_len = tiles_per_seq * TQ
    group = lax.broadcasted_iota(I32, (1, POOL_W), 1) // POOL_GROUP_W
    u0 = u_at(0)
    acc = u0
    mean = jnp.zeros((TQ, POOL_W), F32)
    done = 0
    for g, w in enumerate(POOL_WINDOWS):
        half = w // 2
        for o in range(done + 1, half + 1):
            acc = acc + u_at(-o) + (u_at(o - 1) if o > 1 else 0.0)
        done = half
        cnt = (jnp.minimum(t_seq + half, seq_len) - jnp.maximum(t_seq - half, 0)).astype(F32)
        mean = jnp.where(group == g, acc / cnt, mean)
    d = (mean - u0).astype(BF16)
    y_pool = jnp.dot(d, wpool_ref[...], preferred_element_type=F32) * pscale_ref[...]
    mix = jnp.dot(y_pool.astype(BF16), wout_ref[0:POOL_W, :], preferred_element_type=F32)

    a = halo_ref[:, POOL_W:POOL_W + CONV_W]
    gate = halo_ref[:, POOL_W + CONV_W:]
    halo_ref[:, 0:CONV_W] = a * jax.nn.sigmoid(gate)
    conv = jnp.zeros((TQ, CONV_W), F32) + dwb_ref[...]
    for k in range(CONV_K):
        conv = conv + halo_ref[pl.ds(HALO - CONV_K // 2 + k, TQ), 0:CONV_W] * dw_ref[k:k + 1, :]
    hc = _layer_norm(conv, clng_ref[...], clnb_ref[...])
    hc = hc * jax.nn.sigmoid(hc)
    y_conv = jnp.dot(hc.astype(BF16), wpw_ref[...], preferred_element_type=F32) + bpw_ref[...]
    mix = mix + jnp.dot(y_conv.astype(BF16), wout_ref[POOL_W + NA_W:, :], preferred_element_type=F32)

    lane = lax.broadcasted_iota(I32, (TQ, LANES), 1)
    low = lane < NA_HEAD_DIM
    scale = NA_HEAD_DIM ** -0.5
    for p in range(NA_HEADS // 2):
        cs = slice(p * LANES, (p + 1) * LANES)
        qp = q_ref[:, cs].astype(F32) * scale
        qs = jnp.concatenate([jnp.where(low, qp, 0.0), jnp.where(low, 0.0, qp)], axis=0).astype(BF16)
        kk = jnp.concatenate([kp_ref[:, cs], kc_ref[:, cs], kn_ref[:, cs]], axis=0)
        vv = jnp.concatenate([vp_ref[:, cs], vc_ref[:, cs], vn_ref[:, cs]], axis=0)
        s = lax.dot_general(qs, kk, (((1,), (1,)), ((), ())), preferred_element_type=F32)
        s = s + bias_ref[0, p]
        m = jnp.max(s, axis=-1, keepdims=True)
        e = jnp.exp(s - m)
        l = jnp.sum(e, axis=-1, keepdims=True)
        o = jnp.dot(e.astype(BF16), vv, preferred_element_type=F32) / l
        o_pair = jnp.where(low, o[0:TQ], o[TQ:])
        r0 = POOL_W + p * LANES
        mix = mix + jnp.dot(o_pair.astype(BF16), wout_ref[r0:r0 + LANES, :], preferred_element_type=F32)

    x1 = _layer_norm(alpha * x_ref[...] + mix + bout_ref[...], lng_ref[...], lnb_ref[...])
    x1_ref[...] = x1

    lt = lax.dot_general(wr_ref[...], x1.astype(BF16), (((1,), (1,)), ((), ())),
                         preferred_element_type=F32) + br_ref[...]
    eidx = lax.broadcasted_iota(I32, (N_EXPERTS, TQ), 0)
    work = lt
    vals, idxs = [], []
    for _ in range(TOP_K):
        mk = jnp.max(work, axis=0, keepdims=True)
        ik = jnp.min(jnp.where(work == mk, eidx, N_EXPERTS), axis=0, keepdims=True)
        vals.append(mk)
        idxs.append(ik)
        work = jnp.where(eidx == ik, -jnp.inf, work)
    ex = [jnp.exp(v - vals[0]) for v in vals]
    den = ex[0] + ex[1] + ex[2] + ex[3]
    topp_ref[...] = jnp.concatenate([e_ / den for e_ in ex], axis=0)
    topi_ref[...] = jnp.concatenate(idxs, axis=0)

    run = base_ref[...]
    ranks = []
    for k in range(TOP_K):
        hot = (eidx == idxs[k]).astype(F32)
        before = jnp.dot(hot.astype(BF16), tri_ref[...], preferred_element_type=F32)
        ranks.append(jnp.sum(hot * (run + before), axis=0, keepdims=True))
        run = run + jnp.sum(hot, axis=1, keepdims=True)
    rank_ref[...] = jnp.concatenate(ranks, axis=0).astype(I32)
    base_ref[...] = run
    counts_ref[...] = jnp.broadcast_to(run, counts_ref.shape).astype(I32)


def _mixer(x, qkv, rest, bias_tab, wpool_bd, pscale, dw, dwb, clng, clnb, wpw, bpw,
           wout, bout, lng, lnb, wr_t, br, tri, *, seq, alpha):
    n = x.shape[0]
    nt = n // TQ
    tps = seq // TQ
    hb = TQ // HALO

    def prev_t(i):
        return jnp.where(i % tps == 0, i, i - 1)

    def next_t(i):
        return jnp.where(i % tps == tps - 1, i, i + 1)

    def variant(i):
        ib = i % tps
        return jnp.where(ib == 0, 0, jnp.where(ib == tps - 1, 2, 1))

    def const(shape):
        return pl.BlockSpec(shape, lambda i: tuple(0 for _ in shape))

    in_specs = [
        pl.BlockSpec((TQ, D_MODEL), lambda i: (i, 0)),
        pl.BlockSpec((TQ, NA_W), lambda i: (i, 0)),
        pl.BlockSpec((TQ, NA_W), lambda i: (prev_t(i), 1)),
        pl.BlockSpec((TQ, NA_W), lambda i: (i, 1)),
        pl.BlockSpec((TQ, NA_W), lambda i: (next_t(i), 1)),
        pl.BlockSpec((TQ, NA_W), lambda i: (prev_t(i), 2)),
        pl.BlockSpec((TQ, NA_W), lambda i: (i, 2)),
        pl.BlockSpec((TQ, NA_W), lambda i: (next_t(i), 2)),
        pl.BlockSpec((TQ, POOL_W + 2 * CONV_W), lambda i: (i, 0)),
        pl.BlockSpec((HALO, POOL_W + 2 * CONV_W), lambda i: (jnp.maximum(i * hb - 1, 0), 0)),
        pl.BlockSpec((HALO, POOL_W + 2 * CONV_W), lambda i: (jnp.minimum((i + 1) * hb, nt * hb - 1), 0)),
        pl.BlockSpec((1, NA_HEADS // 2, 2 * TQ, NKEY), lambda i: (variant(i), 0, 0, 0)),
        const((POOL_W, POOL_W)), const((1, POOL_W)),
        const((CONV_K, CONV_W)), const((1, CONV_W)), const((1, CONV_W)), const((1, CONV_W)),
        const((CONV_W, CONV_W)), const((1, CONV_W)),
        const((D_MODEL, D_MODEL)), const((1, D_MODEL)), const((1, D_MODEL)), const((1, D_MODEL)),
        const((N_EXPERTS, D_MODEL)), const((N_EXPERTS, 1)), const((TQ, TQ)),
    ]
    out_specs = [
        pl.BlockSpec((TQ, D_MODEL), lambda i: (i, 0)),
        pl.BlockSpec((TOP_K, TQ), lambda i: (0, i)),
        pl.BlockSpec((TOP_K, TQ), lambda i: (0, i)),
        pl.BlockSpec((TOP_K, TQ), lambda i: (0, i)),
        pl.BlockSpec((N_EXPERTS, LANES), lambda i: (0, 0)),
    ]
    out_shape = [
        jax.ShapeDtypeStruct((n, D_MODEL), F32),
        jax.ShapeDtypeStruct((TOP_K, n), I32),
        jax.ShapeDtypeStruct((TOP_K, n), F32),
        jax.ShapeDtypeStruct((TOP_K, n), I32),
        jax.ShapeDtypeStruct((N_EXPERTS, LANES), I32),
    ]
    return pl.pallas_call(
        functools.partial(_mixer_kernel, tps, alpha),
        grid=(nt,),
        in_specs=in_specs,
        out_specs=out_specs,
        out_shape=out_shape,
        scratch_shapes=[
            pltpu.VMEM((TQ + 2 * HALO, POOL_W + 2 * CONV_W), F32),
            pltpu.VMEM((N_EXPERTS, 1), F32),
        ],
        compiler_params=pltpu.CompilerParams(
            dimension_semantics=("arbitrary",), vmem_limit_bytes=VMEM_LIMIT),
        name="mixer",
    )(x, qkv, qkv, qkv, qkv, qkv, qkv, qkv, rest, rest, rest, bias_tab,
      wpool_bd, pscale, dw, dwb, clng, clnb, wpw, bpw, wout, bout, lng, lnb, wr_t, br, tri)


def _dispatch_kernel(pos_ref, x_hbm, xs_init_hbm, xs_hbm, sem):
    del xs_init_hbm
    i = pl.program_id(0)

    def row_copy(k, r):
        return pltpu.make_async_copy(
            x_hbm.at[pl.ds(i * TQ + r, 1), :], xs_hbm.at[pl.ds(pos_ref[k, r], 1), :], sem)

    for k in range(TOP_K):
        def start(r, c, k=k):
            row_copy(k, r).start()
            return c
        lax.fori_loop(0, TQ, start, 0)
    for k in range(TOP_K):
        def wait(r, c, k=k):
            row_copy(k, r).wait()
            return c
        lax.fori_loop(0, TQ, wait, 0)


def _dispatch(x1, pos, xs_init):
    n = x1.shape[0]
    return pl.pallas_call(
        _dispatch_kernel,
        grid=(n // TQ,),
        in_specs=[
            pl.BlockSpec((TOP_K, TQ), lambda i: (0, i), memory_space=pltpu.SMEM),
            pl.BlockSpec(memory_space=pl.ANY),
            pl.BlockSpec(memory_space=pl.ANY),
        ],
        out_specs=pl.BlockSpec(memory_space=pl.ANY),
        out_shape=jax.ShapeDtypeStruct(xs_init.shape, xs_init.dtype),
        scratch_shapes=[pltpu.SemaphoreType.DMA(())],
        input_output_aliases={2: 0},
        compiler_params=pltpu.CompilerParams(dimension_semantics=("arbitrary",)),
        name="dispatch",
    )(pos, x1, xs_init)


def _experts_kernel(te_ref, na_ref, xs_ref, wg_ref, bg_ref, wu_ref, bu_ref, wd_ref, bd_ref,
                    y_ref, wg_bf, wu_bf, wd_bf):
    t = pl.program_id(0)
    active = t < na_ref[0]
    new_expert = jnp.logical_or(t == 0, te_ref[t] != te_ref[jnp.maximum(t - 1, 0)])

    @pl.when(jnp.logical_and(active, new_expert))
    def _():
        wg_bf[...] = wg_ref[0].astype(BF16)
        wu_bf[...] = wu_ref[0].astype(BF16)
        wd_bf[...] = wd_ref[0].astype(BF16)

    @pl.when(active)
    def _():
        x = xs_ref[...].astype(BF16)
        g = jnp.minimum(jnp.dot(x, wg_bf[...], preferred_element_type=F32) + bg_ref[0], SWIGLU_LIMIT)
        u = jnp.clip(jnp.dot(x, wu_bf[...], preferred_element_type=F32) + bu_ref[0],
                     -SWIGLU_LIMIT, SWIGLU_LIMIT)
        act = (u + 1.0) * g * jax.nn.sigmoid(SWIGLU_ALPHA * g)
        y_ref[...] = jnp.dot(act.astype(BF16), wd_bf[...], preferred_element_type=F32) + bd_ref[0]

    @pl.when(jnp.logical_not(active))
    def _():
        y_ref[...] = jnp.zeros_like(y_ref)


def _experts(tile_expert, n_active, xs, wg, bg, wu, bu, wd, bd):
    m_pad = xs.shape[0]
    n_tiles = m_pad // TM_E

    def xmap(t, te, na):
        return (jnp.minimum(t, na[0] - 1), 0)

    def wmap(t, te, na):
        return (te[t], 0, 0)

    grid_spec = pltpu.PrefetchScalarGridSpec(
        num_scalar_prefetch=2,
        grid=(n_tiles,),
        in_specs=[
            pl.BlockSpec((TM_E, D_MODEL), xmap),
            pl.BlockSpec((1, D_MODEL, D_MODEL), wmap),
            pl.BlockSpec((1, 1, D_MODEL), wmap),
            pl.BlockSpec((1, D_MODEL, D_MODEL), wmap),
            pl.BlockSpec((1, 1, D_MODEL), wmap),
            pl.BlockSpec((1, D_MODEL, D_MODEL), wmap),
            pl.BlockSpec((1, 1, D_MODEL), wmap),
        ],
        out_specs=pl.BlockSpec((TM_E, D_MODEL), lambda t, te, na: (t, 0)),
        scratch_shapes=[pltpu.VMEM((D_MODEL, D_MODEL), BF16)] * 3,
    )
    return pl.pallas_call(
        _experts_kernel,
        grid_spec=grid_spec,
        out_shape=jax.ShapeDtypeStruct((m_pad, D_MODEL), F32),
        compiler_params=pltpu.CompilerParams(
            dimension_semantics=("arbitrary",), vmem_limit_bytes=VMEM_LIMIT),
        name="experts",
    )(tile_expert, n_active, xs, wg, bg, wu, bu, wd, bd)


def _combine_kernel(alpha, pos_ref, x1_ref, gates_ref, lng_ref, lnb_ref, ys_hbm, out_ref, buf, sem):
    def row_copy(k, r):
        return pltpu.make_async_copy(
            ys_hbm.at[pl.ds(pos_ref[k, r], 1), :], buf.at[k, pl.ds(r, 1), :], sem)

    for k in range(TOP_K):
        def start(r, c, k=k):
            row_copy(k, r).start()
            return c
        lax.fori_loop(0, TQ, start, 0)
    for k in range(TOP_K):
        def wait(r, c, k=k):
            row_copy(k, r).wait()
            return c
        lax.fori_loop(0, TQ, wait, 0)

    gates = gates_ref[...]
    ffn = buf[0] * gates[:, 0:1]
    for k in range(1, TOP_K):
        ffn = ffn + buf[k] * gates[:, k:k + 1]
    out_ref[...] = _layer_norm(alpha * x1_ref[...] + ffn, lng_ref[...], lnb_ref[...])


def _combine(pos, x1, gates_tm, lng, lnb, ys, *, alpha):
    n = x1.shape[0]
    return pl.pallas_call(
        functools.partial(_combine_kernel, alpha),
        grid=(n // TQ,),
        in_specs=[
            pl.BlockSpec((TOP_K, TQ), lambda i: (0, i), memory_space=pltpu.SMEM),
            pl.BlockSpec((TQ, D_MODEL), lambda i: (i, 0)),
            pl.BlockSpec((TQ, TOP_K), lambda i: (i, 0)),
            pl.BlockSpec((1, D_MODEL), lambda i: (0, 0)),
            pl.BlockSpec((1, D_MODEL), lambda i: (0, 0)),
            pl.BlockSpec(memory_space=pl.ANY),
        ],
        out_specs=pl.BlockSpec((TQ, D_MODEL), lambda i: (i, 0)),
        out_shape=jax.ShapeDtypeStruct((n, D_MODEL), F32),
        scratch_shapes=[pltpu.VMEM((TOP_K, TQ, D_MODEL), F32), pltpu.SemaphoreType.DMA(())],
        compiler_params=pltpu.CompilerParams(
            dimension_semantics=("arbitrary",), vmem_limit_bytes=VMEM_LIMIT),
        name="combine",
    )(pos, x1, gates_tm, lng, lnb, ys)


def kernel(x, w_in, b_in, w_pool, pool_scale, rpb, conv_dw, conv_dw_b, conv_ln_g, conv_ln_b,
           w_conv_pw, b_conv_pw, w_out, b_out, ln1_g, ln1_b, w_router, b_router,
           w_gate, b_gate, w_up, b_up, w_down, b_down, ln2_g, ln2_b):
    batch, seq, d = x.shape
    depth = w_in.shape[0]
    n = batch * seq
    rows = seq // GRID_W
    alpha = (2.0 * depth) ** 0.25
    off_q, off_k, off_v = POOL_W, POOL_W + NA_W, POOL_W + 2 * NA_W
    off_ca = off_v + NA_W
    m_pad = n * TOP_K + N_EXPERTS * TM_E
    n_tiles = m_pad // TM_E

    tri = (np.arange(TQ)[:, None] < np.arange(TQ)[None, :]).astype(np.float32)
    tri = jnp.asarray(tri, BF16)
    row2 = lambda v: v.reshape(1, -1)

    h = x.reshape(n, d)
    for l in range(depth):
        wqkv = w_in[l][:, off_q:off_ca].astype(BF16)
        bqkv = row2(b_in[l][off_q:off_ca])
        wrest = jnp.concatenate([w_in[l][:, :off_q], w_in[l][:, off_ca:]], axis=1).astype(BF16)
        brest = row2(jnp.concatenate([b_in[l][:off_q], b_in[l][off_ca:]]))
        qkv, rest = _inproj(h, wqkv, bqkv, wrest, brest)

        wpool_bd = jax.scipy.linalg.block_diag(*[w_pool[l][g] for g in range(len(POOL_WINDOWS))]).astype(BF16)
        bias_tab = _attn_bias_tables(rpb[l], rows)
        x1, top_i, top_p, rank, counts = _mixer(
            h, qkv, rest, bias_tab, wpool_bd, row2(pool_scale[l]),
            conv_dw[l].reshape(CONV_K, CONV_W), row2(conv_dw_b[l]), row2(conv_ln_g[l]), row2(conv_ln_b[l]),
            w_conv_pw[l].astype(BF16), row2(b_conv_pw[l]),
            w_out[l].astype(BF16), row2(b_out[l]), row2(ln1_g[l]), row2(ln1_b[l]),
            w_router[l].T.astype(BF16), b_router[l].reshape(N_EXPERTS, 1), tri,
            seq=seq, alpha=alpha)

        cnt = counts[:, 0]
        cpad = ((cnt + TM_E - 1) // TM_E) * TM_E
        ends = jnp.cumsum(cpad)
        off = ends - cpad
        pos = off[top_i] + rank
        n_active = (ends[-1] // TM_E).astype(I32)
        tile_expert = jnp.searchsorted(ends, jnp.arange(n_tiles, dtype=I32) * TM_E, side="right")
        last_e = jnp.searchsorted(ends, ends[-1] - 1, side="right")
        tile_expert = jnp.minimum(tile_expert, last_e).astype(I32)

        xs = _dispatch(x1, pos, jnp.zeros((m_pad, d), F32))
        ys = _experts(tile_expert, n_active.reshape(1), xs,
                      w_gate[l], b_gate[l].reshape(N_EXPERTS, 1, d),
                      w_up[l], b_up[l].reshape(N_EXPERTS, 1, d),
                      w_down[l], b_down[l].reshape(N_EXPERTS, 1, d))
        h = _combine(pos, x1, top_p.T, row2(ln2_g[l]), row2(ln2_b[l]), ys, alpha=alpha)
    return h.reshape(batch, seq, d)
```

```python
import functools

import jax
import jax.numpy as jnp
import numpy as np
from jax import lax
from jax.experimental import pallas as pl
from jax.experimental.pallas import tpu as pltpu

F32 = jnp.float32
BF16 = jnp.bfloat16
I32 = jnp.int32

D_MODEL = 1024
GRID_W = 64
POOL_W = 256
POOL_WINDOWS = (2, 4, 8, 16)
POOL_GROUP_W = 64
NA_W = 512
NA_HEADS = 8
NA_HEAD_DIM = 64
NA_ROWS = 8
NA_COLS = 16
CONV_W = 256
CONV_K = 31
N_EXPERTS = 32
TOP_K = 4
SWIGLU_ALPHA = 1.702
SWIGLU_LIMIT = 7.0
LN_EPS = 1e-5
NEG_INF = -1e30

LANES = 128
TM_IN = 512
TQ = 256
TQ_ROWS = TQ // GRID_W
NKEY = 3 * TQ
HALO = 16
TM_E = 512
VMEM_LIMIT = 56 * 1024 * 1024


def _layer_norm(v, g, b):
    mu = jnp.mean(v, axis=-1, keepdims=True)
    c = v - mu
    var = jnp.mean(c * c, axis=-1, keepdims=True)
    return c * lax.rsqrt(var + LN_EPS) * g + b


def _inproj_kernel(x_ref, wqkv_ref, bqkv_ref, wrest_ref, brest_ref, qkv_ref, rest_ref):
    x = x_ref[...].astype(BF16)
    qkv = jnp.dot(x, wqkv_ref[...], preferred_element_type=F32) + bqkv_ref[...]
    qkv_ref[...] = qkv.astype(BF16)
    rest_ref[...] = jnp.dot(x, wrest_ref[...], preferred_element_type=F32) + brest_ref[...]


def _inproj(x, wqkv, bqkv, wrest, brest):
    n = x.shape[0]
    return pl.pallas_call(
        _inproj_kernel,
        grid=(n // TM_IN,),
        in_specs=[
            pl.BlockSpec((TM_IN, D_MODEL), lambda i: (i, 0)),
            pl.BlockSpec((D_MODEL, 3 * NA_W), lambda i: (0, 0)),
            pl.BlockSpec((1, 3 * NA_W), lambda i: (0, 0)),
            pl.BlockSpec((D_MODEL, POOL_W + 2 * CONV_W), lambda i: (0, 0)),
            pl.BlockSpec((1, POOL_W + 2 * CONV_W), lambda i: (0, 0)),
        ],
        out_specs=[
            pl.BlockSpec((TM_IN, 3 * NA_W), lambda i: (i, 0)),
            pl.BlockSpec((TM_IN, POOL_W + 2 * CONV_W), lambda i: (i, 0)),
        ],
        out_shape=[
            jax.ShapeDtypeStruct((n, 3 * NA_W), BF16),
            jax.ShapeDtypeStruct((n, POOL_W + 2 * CONV_W), F32),
        ],
        compiler_params=pltpu.CompilerParams(
            dimension_semantics=("arbitrary",), vmem_limit_bytes=VMEM_LIMIT),
        name="inproj",
    )(x, wqkv, bqkv, wrest, brest)


def _attn_bias_tables(rpb, rows):
    kr_win = min(NA_ROWS, rows)
    n_tiles = rows // TQ_ROWS
    n_r, n_c = 2 * NA_ROWS - 1, 2 * NA_COLS - 1
    col = np.arange(GRID_W)
    c_off = np.clip(col[None, :] - col[:, None] + NA_COLS - 1, 0, n_c - 1)
    c_hot = (c_off[..., None] == np.arange(n_c)).astype(np.float32)
    sc = np.clip(col - NA_COLS // 2, 0, GRID_W - NA_COLS)[:, None]
    c_ok = (col[None, :] >= sc) & (col[None, :] < sc + NA_COLS)
    r_hots, masks = [], []
    for tile in (0, 1, n_tiles - 1):
        qr = tile * TQ_ROWS + np.arange(TQ_ROWS)
        slot_tile = np.repeat(np.array([tile - 1, tile, tile + 1]), TQ_ROWS)
        kr = slot_tile * TQ_ROWS + np.tile(np.arange(TQ_ROWS), 3)
        slot_ok = (slot_tile >= 0) & (slot_tile < n_tiles)
        sr = np.clip(qr - kr_win // 2, 0, rows - kr_win)[:, None]
        r_ok = (kr[None] >= sr) & (kr[None] < sr + kr_win) & slot_ok[None]
        r_off = np.clip(kr[None] - qr[:, None] + NA_ROWS - 1, 0, n_r - 1)
        r_hots.append((r_off[..., None] == np.arange(n_r)).astype(np.float32))
        masks.append(r_ok[:, None, :, None] & c_ok[None, :, None, :])
    r_hot = jnp.asarray(np.stack(r_hots))
    mask = jnp.asarray(np.stack(masks))[None, :, None]
    hi = lax.Precision.HIGHEST
    by_col = jnp.einsum("lhrc,xyc->lhrxy", rpb.astype(F32), jnp.asarray(c_hot), precision=hi)
    bias = jnp.einsum("vasr,lhrxy->lvhaxsy", r_hot, by_col, precision=hi)
    bias = jnp.where(mask, bias, NEG_INF)
    return bias.reshape(rpb.shape[0], 3, NA_HEADS // 2, 2 * TQ, NKEY)


def _mixer_kernel(tiles_per_seq, alpha,
                  x_ref, q_ref, kp_ref, kc_ref, kn_ref, vp_ref, vc_ref, vn_ref,
                  rc_ref, rp_ref, rn_ref, bias_ref,
                  wpool_ref, pscale_ref, dw_ref, dwb_ref, clng_ref, clnb_ref, wpw_ref, bpw_ref,
                  wout_ref, bout_ref, lng_ref, lnb_ref, wr_ref, br_ref, tri_ref,
                  x1_ref, topi_ref, topp_ref, rank_ref, counts_ref,
                  halo_ref, base_ref):
    i = pl.program_id(0)
    ib = i % tiles_per_seq
    has_prev = ib > 0
    has_next = ib < tiles_per_seq - 1

    @pl.when(i == 0)
    def _():
        base_ref[...] = jnp.zeros_like(base_ref)

    halo_ref[0:HALO, :] = jnp.where(has_prev, rp_ref[...], 0.0)
    halo_ref[HALO:HALO + TQ, :] = rc_ref[...]
    halo_ref[HALO + TQ:, :] = jnp.where(has_next, rn_ref[...], 0.0)

    def u_at(off):
        return halo_ref[pl.ds(HALO + off, TQ), 0:POOL_W]

    t_seq = ib * TQ + lax.broadcasted_iota(I32, (TQ, 1), 0)
    seq_len = tiles_per_seq * TQ
    group = lax.broadcasted_iota(I32, (1, POOL_W), 1) // POOL_GROUP_W
    u0 = u_at(0)
    acc = u0
    mean = jnp.zeros((TQ, POOL_W), F32)
    done = 0
    for g, w in enumerate(POOL_WINDOWS):
        half = w // 2
        for o in range(done + 1, half + 1):
            acc = acc + u_at(-o) + (u_at(o - 1) if o > 1 else 0.0)
        done = half
        cnt = (jnp.minimum(t_seq + half, seq_len) - jnp.maximum(t_seq - half, 0)).astype(F32)
        mean = jnp.where(group == g, acc / cnt, mean)
    d = (mean - u0).astype(BF16)
    y_pool = jnp.dot(d, wpool_ref[...], preferred_element_type=F32) * pscale_ref[...]
    mix = jnp.dot(y_pool.astype(BF16), wout_ref[0:POOL_W, :], preferred_element_type=F32)

    a = halo_ref[:, POOL_W:POOL_W + CONV_W]
    gate = halo_ref[:, POOL_W + CONV_W:]
    halo_ref[:, 0:CONV_W] = a * jax.nn.sigmoid(gate)
    conv = jnp.zeros((TQ, CONV_W), F32) + dwb_ref[...]
    for k in range(CONV_K):
        conv = conv + halo_ref[pl.ds(HALO - CONV_K // 2 + k, TQ), 0:CONV_W] * dw_ref[k:k + 1, :]
    hc = _layer_norm(conv, clng_ref[...], clnb_ref[...])
    hc = hc * jax.nn.sigmoid(hc)
    y_conv = jnp.dot(hc.astype(BF16), wpw_ref[...], preferred_element_type=F32) + bpw_ref[...]
    mix = mix + jnp.dot(y_conv.astype(BF16), wout_ref[POOL_W + NA_W:, :], preferred_element_type=F32)

    lane = lax.broadcasted_iota(I32, (TQ, LANES), 1)
    low = lane < NA_HEAD_DIM
    scale = NA_HEAD_DIM ** -0.5
    for p in range(NA_HEADS // 2):
        cs = slice(p * LANES, (p + 1) * LANES)
        qp = q_ref[:, cs].astype(F32) * scale
        qs = jnp.concatenate([jnp.where(low, qp, 0.0), jnp.where(low, 0.0, qp)], axis=0).astype(BF16)
        kk = jnp.concatenate([kp_ref[:, cs], kc_ref[:, cs], kn_ref[:, cs]], axis=0)
        vv = jnp.concatenate([vp_ref[:, cs], vc_ref[:, cs], vn_ref[:, cs]], axis=0)
        s = lax.dot_general(qs, kk, (((1,), (1,)), ((), ())), preferred_element_type=F32)
        s = s + bias_ref[0, p]
        m = jnp.max(s, axis=-1, keepdims=True)
        e = jnp.exp(s - m)
        l = jnp.sum(e, axis=-1, keepdims=True)
        o = jnp.dot(e.astype(BF16), vv, preferred_element_type=F32) / l
        o_pair = jnp.where(low, o[0:TQ], o[TQ:])
        r0 = POOL_W + p * LANES
        mix = mix + jnp.dot(o_pair.astype(BF16), wout_ref[r0:r0 + LANES, :], preferred_element_type=F32)

    x1 = _layer_norm(alpha * x_ref[...] + mix + bout_ref[...], lng_ref[...], lnb_ref[...])
    x1_ref[...] = x1

    lt = lax.dot_general(wr_ref[...], x1.astype(BF16), (((1,), (1,)), ((), ())),
                         preferred_element_type=F32) + br_ref[...]
    eidx = lax.broadcasted_iota(I32, (N_EXPERTS, TQ), 0)
    work = lt
    vals, idxs = [], []
    for _ in range(TOP_K):
        mk = jnp.max(work, axis=0, keepdims=True)
        ik = jnp.min(jnp.where(work == mk, eidx, N_EXPERTS), axis=0, keepdims=True)
        vals.append(mk)
        idxs.append(ik)
        work = jnp.where(eidx == ik, -jnp.inf, work)
    ex = [jnp.exp(v - vals[0]) for v in vals]
    den = ex[0] + ex[1] + ex[2] + ex[3]
    topp_ref[...] = jnp.concatenate([e_ / den for e_ in ex], axis=0)
    topi_ref[...] = jnp.concatenate(idxs, axis=0)

    run = base_ref[...]
    ranks = []
    for k in range(TOP_K):
        hot = (eidx == idxs[k]).astype(F32)
        before = jnp.dot(hot.astype(BF16), tri_ref[...], preferred_element_type=F32)
        ranks.append(jnp.sum(hot * (run + before), axis=0, keepdims=True))
        run = run + jnp.sum(hot, axis=1, keepdims=True)
    rank_ref[...] = jnp.concatenate(ranks, axis=0).astype(I32)
    base_ref[...] = run
    counts_ref[...] = jnp.broadcast_to(run, counts_ref.shape).astype(I32)


def _mixer(x, qkv, rest, bias_tab, wpool_bd, pscale, dw, dwb, clng, clnb, wpw, bpw,
           wout, bout, lng, lnb, wr_t, br, tri, *, layer, seq, alpha):
    n = x.shape[0]
    nt = n // TQ
    tps = seq // TQ
    hb = TQ // HALO

    def prev_t(i):
        return jnp.where(i % tps == 0, i, i - 1)

    def next_t(i):
        return jnp.where(i % tps == tps - 1, i, i + 1)

    def variant(i):
        ib = i % tps
        return jnp.where(ib == 0, 0, jnp.where(ib == tps - 1, 2, 1))

    def const(shape):
        return pl.BlockSpec(shape, lambda i: tuple(0 for _ in shape))

    in_specs = [
        pl.BlockSpec((TQ, D_MODEL), lambda i: (i, 0)),
        pl.BlockSpec((TQ, NA_W), lambda i: (i, 0)),
        pl.BlockSpec((TQ, NA_W), lambda i: (prev_t(i), 1)),
        pl.BlockSpec((TQ, NA_W), lambda i: (i, 1)),
        pl.BlockSpec((TQ, NA_W), lambda i: (next_t(i), 1)),
        pl.BlockSpec((TQ, NA_W), lambda i: (prev_t(i), 2)),
        pl.BlockSpec((TQ, NA_W), lambda i: (i, 2)),
        pl.BlockSpec((TQ, NA_W), lambda i: (next_t(i), 2)),
        pl.BlockSpec((TQ, POOL_W + 2 * CONV_W), lambda i: (i, 0)),
        pl.BlockSpec((HALO, POOL_W + 2 * CONV_W), lambda i: (jnp.maximum(i * hb - 1, 0), 0)),
        pl.BlockSpec((HALO, POOL_W + 2 * CONV_W), lambda i: (jnp.minimum((i + 1) * hb, nt * hb - 1), 0)),
        pl.BlockSpec((None, 1, NA_HEADS // 2, 2 * TQ, NKEY), lambda i: (layer, variant(i), 0, 0, 0)),
        const((POOL_W, POOL_W)), const((1, POOL_W)),
        const((CONV_K, CONV_W)), const((1, CONV_W)), const((1, CONV_W)), const((1, CONV_W)),
        const((CONV_W, CONV_W)), const((1, CONV_W)),
        const((D_MODEL, D_MODEL)), const((1, D_MODEL)), const((1, D_MODEL)), const((1, D_MODEL)),
        const((N_EXPERTS, D_MODEL)), const((N_EXPERTS, 1)), const((TQ, TQ)),
    ]
    out_specs = [
        pl.BlockSpec((TQ, D_MODEL), lambda i: (i, 0)),
        pl.BlockSpec((TOP_K, TQ), lambda i: (0, i)),
        pl.BlockSpec((TOP_K, TQ), lambda i: (0, i)),
        pl.BlockSpec((TOP_K, TQ), lambda i: (0, i)),
        pl.BlockSpec((N_EXPERTS, LANES), lambda i: (0, 0)),
    ]
    out_shape = [
        jax.ShapeDtypeStruct((n, D_MODEL), F32),
        jax.ShapeDtypeStruct((TOP_K, n), I32),
        jax.ShapeDtypeStruct((TOP_K, n), F32),
        jax.ShapeDtypeStruct((TOP_K, n), I32),
        jax.ShapeDtypeStruct((N_EXPERTS, LANES), I32),
    ]
    return pl.pallas_call(
        functools.partial(_mixer_kernel, tps, alpha),
        grid=(nt,),
        in_specs=in_specs,
        out_specs=out_specs,
        out_shape=out_shape,
        scratch_shapes=[
            pltpu.VMEM((TQ + 2 * HALO, POOL_W + 2 * CONV_W), F32),
            pltpu.VMEM((N_EXPERTS, 1), F32),
        ],
        compiler_params=pltpu.CompilerParams(
            dimension_semantics=("arbitrary",), vmem_limit_bytes=VMEM_LIMIT),
        name="mixer",
    )(x, qkv, qkv, qkv, qkv, qkv, qkv, qkv, rest, rest, rest, bias_tab,
      wpool_bd, pscale, dw, dwb, clng, clnb, wpw, bpw, wout, bout, lng, lnb, wr_t, br, tri)


def _dispatch_kernel(pos_ref, x_ref, xs_init_hbm, xs_hbm, sem):
    del xs_init_hbm
    for k in range(TOP_K):
        for r in range(TQ):
            pltpu.make_async_copy(
                x_ref.at[pl.ds(r, 1), :], xs_hbm.at[pl.ds(pos_ref[k, r], 1), :], sem).start()
    for k in range(TOP_K):
        pltpu.make_async_copy(x_ref, xs_hbm.at[pl.ds(0, TQ), :], sem).wait()


def _dispatch(x1, pos, xs_init):
    n = x1.shape[0]
    return pl.pallas_call(
        _dispatch_kernel,
        grid=(n // TQ,),
        in_specs=[
            pl.BlockSpec((TOP_K, TQ), lambda i: (0, i), memory_space=pltpu.SMEM),
            pl.BlockSpec((TQ, D_MODEL), lambda i: (i, 0)),
            pl.BlockSpec(memory_space=pl.ANY),
        ],
        out_specs=pl.BlockSpec(memory_space=pl.ANY),
        out_shape=jax.ShapeDtypeStruct(xs_init.shape, xs_init.dtype),
        scratch_shapes=[pltpu.SemaphoreType.DMA(())],
        input_output_aliases={2: 0},
        compiler_params=pltpu.CompilerParams(dimension_semantics=("arbitrary",)),
        name="dispatch",
    )(pos, x1, xs_init)


def _experts_kernel(te_ref, na_ref, xs_ref, wg_ref, bg_ref, wu_ref, bu_ref, wd_ref, bd_ref,
                    y_ref, wg_bf, wu_bf, wd_bf):
    t = pl.program_id(0)
    active = t < na_ref[0]
    new_expert = jnp.logical_or(t == 0, te_ref[t] != te_ref[jnp.maximum(t - 1, 0)])

    @pl.when(jnp.logical_and(active, new_expert))
    def _():
        wg_bf[...] = wg_ref[...].astype(BF16)
        wu_bf[...] = wu_ref[...].astype(BF16)
        wd_bf[...] = wd_ref[...].astype(BF16)

    @pl.when(active)
    def _():
        x = xs_ref[...].astype(BF16)
        g = jnp.minimum(jnp.dot(x, wg_bf[...], preferred_element_type=F32) + bg_ref[...], SWIGLU_LIMIT)
        u = jnp.clip(jnp.dot(x, wu_bf[...], preferred_element_type=F32) + bu_ref[...],
                     -SWIGLU_LIMIT, SWIGLU_LIMIT)
        act = (u + 1.0) * g * jax.nn.sigmoid(SWIGLU_ALPHA * g)
        y_ref[...] = jnp.dot(act.astype(BF16), wd_bf[...], preferred_element_type=F32) + bd_ref[...]

    @pl.when(jnp.logical_not(active))
    def _():
        y_ref[...] = jnp.zeros_like(y_ref)


def _experts(layer, tile_expert, n_active, xs, wg, bg, wu, bu, wd, bd):
    m_pad = xs.shape[0]
    n_tiles = m_pad // TM_E

    def xmap(t, te, na):
        return (jnp.minimum(t, na[0] - 1), 0)

    def wmap(t, te, na):
        return (layer, te[t], 0, 0)

    grid_spec = pltpu.PrefetchScalarGridSpec(
        num_scalar_prefetch=2,
        grid=(n_tiles,),
        in_specs=[
            pl.BlockSpec((TM_E, D_MODEL), xmap),
            pl.BlockSpec((None, None, D_MODEL, D_MODEL), wmap),
            pl.BlockSpec((None, None, 1, D_MODEL), wmap),
            pl.BlockSpec((None, None, D_MODEL, D_MODEL), wmap),
            pl.BlockSpec((None, None, 1, D_MODEL), wmap),
            pl.BlockSpec((None, None, D_MODEL, D_MODEL), wmap),
            pl.BlockSpec((None, None, 1, D_MODEL), wmap),
        ],
        out_specs=pl.BlockSpec((TM_E, D_MODEL), lambda t, te, na: (t, 0)),
        scratch_shapes=[pltpu.VMEM((D_MODEL, D_MODEL), BF16)] * 3,
    )
    return pl.pallas_call(
        _experts_kernel,
        grid_spec=grid_spec,
        out_shape=jax.ShapeDtypeStruct((m_pad, D_MODEL), F32),
        compiler_params=pltpu.CompilerParams(
            dimension_semantics=("arbitrary",), vmem_limit_bytes=VMEM_LIMIT),
        name="experts",
    )(tile_expert, n_active, xs, wg, bg, wu, bu, wd, bd)


def _combine_kernel(alpha, pos_ref, x1_ref, gates_ref, lng_ref, lnb_ref, ys_hbm, out_ref, buf, sem):
    for k in range(TOP_K):
        for r in range(TQ):
            pltpu.make_async_copy(
                ys_hbm.at[pl.ds(pos_ref[k, r], 1), :], buf.at[k, pl.ds(r, 1), :], sem).start()
    for k in range(TOP_K):
        pltpu.make_async_copy(ys_hbm.at[pl.ds(0, TQ), :], buf.at[k], sem).wait()

    gates = gates_ref[...]
    ffn = buf[0] * gates[:, 0:1]
    for k in range(1, TOP_K):
        ffn = ffn + buf[k] * gates[:, k:k + 1]
    out_ref[...] = _layer_norm(alpha * x1_ref[...] + ffn, lng_ref[...], lnb_ref[...])


def _combine(pos, x1, gates_tm, lng, lnb, ys, *, alpha):
    n = x1.shape[0]
    return pl.pallas_call(
        functools.partial(_combine_kernel, alpha),
        grid=(n // TQ,),
        in_specs=[
            pl.BlockSpec((TOP_K, TQ), lambda i: (0, i), memory_space=pltpu.SMEM),
            pl.BlockSpec((TQ, D_MODEL), lambda i: (i, 0)),
            pl.BlockSpec((TQ, TOP_K), lambda i: (i, 0)),
            pl.BlockSpec((1, D_MODEL), lambda i: (0, 0)),
            pl.BlockSpec((1, D_MODEL), lambda i: (0, 0)),
            pl.BlockSpec(memory_space=pl.ANY),
        ],
        out_specs=pl.BlockSpec((TQ, D_MODEL), lambda i: (i, 0)),
        out_shape=jax.ShapeDtypeStruct((n, D_MODEL), F32),
        scratch_shapes=[pltpu.VMEM((TOP_K, TQ, D_MODEL), F32), pltpu.SemaphoreType.DMA(())],
        compiler_params=pltpu.CompilerParams(
            dimension_semantics=("arbitrary",), vmem_limit_bytes=VMEM_LIMIT),
        name="combine",
    )(pos, x1, gates_tm, lng, lnb, ys)


def kernel(x, w_in, b_in, w_pool, pool_scale, rpb, conv_dw, conv_dw_b, conv_ln_g, conv_ln_b,
           w_conv_pw, b_conv_pw, w_out, b_out, ln1_g, ln1_b, w_router, b_router,
           w_gate, b_gate, w_up, b_up, w_down, b_down, ln2_g, ln2_b):
    batch, seq, d = x.shape
    depth = w_in.shape[0]
    n = batch * seq
    rows = seq // GRID_W
    alpha = (2.0 * depth) ** 0.25
    off_q, off_k, off_v = POOL_W, POOL_W + NA_W, POOL_W + 2 * NA_W
    off_ca = off_v + NA_W
    m_pad = n * TOP_K + N_EXPERTS * TM_E
    n_tiles = m_pad // TM_E

    tri = (np.arange(TQ)[:, None] < np.arange(TQ)[None, :]).astype(np.float32)
    tri = jnp.asarray(tri, BF16)
    row2 = lambda v: v.reshape(1, -1)
    bias_tabs = _attn_bias_tables(rpb, rows)
    experts_iota = jnp.arange(N_EXPERTS, dtype=I32)
    tile_starts = jnp.arange(n_tiles, dtype=I32) * TM_E
    b_gate4 = b_gate.reshape(depth, N_EXPERTS, 1, d)
    b_up4 = b_up.reshape(depth, N_EXPERTS, 1, d)
    b_down4 = b_down.reshape(depth, N_EXPERTS, 1, d)

    h = x.reshape(n, d)
    for l in range(depth):
        wqkv = w_in[l][:, off_q:off_ca].astype(BF16)
        bqkv = row2(b_in[l][off_q:off_ca])
        wrest = jnp.concatenate([w_in[l][:, :off_q], w_in[l][:, off_ca:]], axis=1).astype(BF16)
        brest = row2(jnp.concatenate([b_in[l][:off_q], b_in[l][off_ca:]]))
        qkv, rest = _inproj(h, wqkv, bqkv, wrest, brest)

        wpool_bd = jax.scipy.linalg.block_diag(*[w_pool[l][g] for g in range(len(POOL_WINDOWS))]).astype(BF16)
        x1, top_i, top_p, rank, counts = _mixer(
            h, qkv, rest, bias_tabs, wpool_bd, row2(pool_scale[l]),
            conv_dw[l].reshape(CONV_K, CONV_W), row2(conv_dw_b[l]), row2(conv_ln_g[l]), row2(conv_ln_b[l]),
            w_conv_pw[l].astype(BF16), row2(b_conv_pw[l]),
            w_out[l].astype(BF16), row2(b_out[l]), row2(ln1_g[l]), row2(ln1_b[l]),
            w_router[l].T.astype(BF16), b_router[l].reshape(N_EXPERTS, 1), tri,
            layer=l, seq=seq, alpha=alpha)

        cnt = counts[:, 0]
        cpad = ((cnt + TM_E - 1) // TM_E) * TM_E
        ends = jnp.cumsum(cpad)
        off = ends - cpad
        hot = top_i[None] == experts_iota[:, None, None]
        pos = rank + jnp.sum(jnp.where(hot, off[:, None, None], 0), axis=0)
        n_active = (ends[-1] // TM_E).astype(I32)
        last_start = jnp.minimum(tile_starts, ends[-1] - TM_E)
        tile_expert = jnp.sum((ends[None, :] <= last_start[:, None]).astype(I32), axis=1)

        xs = _dispatch(x1, pos, jnp.zeros((m_pad, d), F32))
        ys = _experts(l, tile_expert, n_active.reshape(1), xs,
                      w_gate, b_gate4, w_up, b_up4, w_down, b_down4)
        h = _combine(pos, x1, top_p.T, row2(ln2_g[l]), row2(ln2_b[l]), ys, alpha=alpha)
    return h.reshape(batch, seq, d)
```

```python
import functools

import jax
import jax.numpy as jnp
import numpy as np
from jax import lax
from jax.experimental import pallas as pl
from jax.experimental.pallas import tpu as pltpu

F32 = jnp.float32
BF16 = jnp.bfloat16
I32 = jnp.int32

D_MODEL = 1024
GRID_W = 64
POOL_W = 256
POOL_WINDOWS = (2, 4, 8, 16)
POOL_GROUP_W = 64
NA_W = 512
NA_HEADS = 8
NA_HEAD_DIM = 64
NA_ROWS = 8
NA_COLS = 16
CONV_W = 256
CONV_K = 31
N_EXPERTS = 32
TOP_K = 4
SWIGLU_ALPHA = 1.702
SWIGLU_LIMIT = 7.0
LN_EPS = 1e-5
NEG_INF = -1e30

LANES = 128
TM_IN = 512
TQ = 256
TQ_ROWS = TQ // GRID_W
NKEY = 3 * TQ
HALO = 16
TM_E = 512
VMEM_LIMIT = 56 * 1024 * 1024


def _layer_norm(v, g, b):
    mu = jnp.mean(v, axis=-1, keepdims=True)
    c = v - mu
    var = jnp.mean(c * c, axis=-1, keepdims=True)
    return c * lax.rsqrt(var + LN_EPS) * g + b


def _inproj_kernel(x_ref, wqkv_ref, bqkv_ref, wrest_ref, brest_ref, qkv_ref, rest_ref):
    x = x_ref[...].astype(BF16)
    qkv = jnp.dot(x, wqkv_ref[...], preferred_element_type=F32) + bqkv_ref[...]
    qkv_ref[...] = qkv.astype(BF16)
    rest_ref[...] = jnp.dot(x, wrest_ref[...], preferred_element_type=F32) + brest_ref[...]


def _inproj(x, wqkv, bqkv, wrest, brest):
    n = x.shape[0]
    return pl.pallas_call(
        _inproj_kernel,
        grid=(n // TM_IN,),
        in_specs=[
            pl.BlockSpec((TM_IN, D_MODEL), lambda i: (i, 0)),
            pl.BlockSpec((D_MODEL, 3 * NA_W), lambda i: (0, 0)),
            pl.BlockSpec((1, 3 * NA_W), lambda i: (0, 0)),
            pl.BlockSpec((D_MODEL, POOL_W + 2 * CONV_W), lambda i: (0, 0)),
            pl.BlockSpec((1, POOL_W + 2 * CONV_W), lambda i: (0, 0)),
        ],
        out_specs=[
            pl.BlockSpec((TM_IN, 3 * NA_W), lambda i: (i, 0)),
            pl.BlockSpec((TM_IN, POOL_W + 2 * CONV_W), lambda i: (i, 0)),
        ],
        out_shape=[
            jax.ShapeDtypeStruct((n, 3 * NA_W), BF16),
            jax.ShapeDtypeStruct((n, POOL_W + 2 * CONV_W), F32),
        ],
        compiler_params=pltpu.CompilerParams(
            dimension_semantics=("arbitrary",), vmem_limit_bytes=VMEM_LIMIT),
        name="inproj",
    )(x, wqkv, bqkv, wrest, brest)


def _attn_bias_tables(rpb, rows):
    kr_win = min(NA_ROWS, rows)
    n_tiles = rows // TQ_ROWS
    n_r, n_c = 2 * NA_ROWS - 1, 2 * NA_COLS - 1
    col = np.arange(GRID_W)
    c_off = np.clip(col[None, :] - col[:, None] + NA_COLS - 1, 0, n_c - 1)
    c_hot = (c_off[..., None] == np.arange(n_c)).astype(np.float32)
    sc = np.clip(col - NA_COLS // 2, 0, GRID_W - NA_COLS)[:, None]
    c_ok = (col[None, :] >= sc) & (col[None, :] < sc + NA_COLS)
    r_hots, masks = [], []
    for tile in (0, 1, n_tiles - 1):
        qr = tile * TQ_ROWS + np.arange(TQ_ROWS)
        slot_tile = np.repeat(np.array([tile - 1, tile, tile + 1]), TQ_ROWS)
        kr = slot_tile * TQ_ROWS + np.tile(np.arange(TQ_ROWS), 3)
        slot_ok = (slot_tile >= 0) & (slot_tile < n_tiles)
        sr = np.clip(qr - kr_win // 2, 0, rows - kr_win)[:, None]
        r_ok = (kr[None] >= sr) & (kr[None] < sr + kr_win) & slot_ok[None]
        r_off = np.clip(kr[None] - qr[:, None] + NA_ROWS - 1, 0, n_r - 1)
        r_hots.append((r_off[..., None] == np.arange(n_r)).astype(np.float32))
        masks.append(r_ok[:, None, :, None] & c_ok[None, :, None, :])
    r_hot = jnp.asarray(np.stack(r_hots))
    mask = jnp.asarray(np.stack(masks))[None, :, None]
    hi = lax.Precision.HIGHEST
    by_col = jnp.einsum("lhrc,xyc->lhrxy", rpb.astype(F32), jnp.asarray(c_hot), precision=hi)
    bias = jnp.einsum("vasr,lhrxy->lvhaxsy", r_hot, by_col, precision=hi)
    bias = jnp.where(mask, bias, NEG_INF)
    return bias.reshape(rpb.shape[0], 3, NA_HEADS // 2, 2 * TQ, NKEY)


def _mixer_kernel(tiles_per_seq, alpha,
                  x_ref, q_ref, kp_ref, kc_ref, kn_ref, vp_ref, vc_ref, vn_ref,
                  rc_ref, rp_ref, rn_ref, bias_ref,
                  wpool_ref, pscale_ref, dw_ref, dwb_ref, clng_ref, clnb_ref, wpw_ref, bpw_ref,
                  wout_ref, bout_ref, lng_ref, lnb_ref, wr_ref, br_ref, tri_ref,
                  x1_ref, topi_ref, topp_ref, rank_ref, counts_ref,
                  halo_ref, base_ref):
    i = pl.program_id(0)
    ib = i % tiles_per_seq
    has_prev = ib > 0
    has_next = ib < tiles_per_seq - 1

    @pl.when(i == 0)
    def _():
        base_ref[...] = jnp.zeros_like(base_ref)

    halo_ref[0:HALO, :] = jnp.where(has_prev, rp_ref[...], 0.0)
    halo_ref[HALO:HALO + TQ, :] = rc_ref[...]
    halo_ref[HALO + TQ:, :] = jnp.where(has_next, rn_ref[...], 0.0)

    def u_at(off):
        return halo_ref[pl.ds(HALO + off, TQ), 0:POOL_W]

    t_seq = ib * TQ + lax.broadcasted_iota(I32, (TQ, 1), 0)
    seq_len = tiles_per_seq * TQ
    group = lax.broadcasted_iota(I32, (1, POOL_W), 1) // POOL_GROUP_W
    u0 = u_at(0)
    acc = u0
    mean = jnp.zeros((TQ, POOL_W), F32)
    done = 0
    for g, w in enumerate(POOL_WINDOWS):
        half = w // 2
        for o in range(done + 1, half + 1):
            acc = acc + u_at(-o) + (u_at(o - 1) if o > 1 else 0.0)
        done = half
        cnt = (jnp.minimum(t_seq + half, seq_len) - jnp.maximum(t_seq - half, 0)).astype(F32)
        mean = jnp.where(group == g, acc / cnt, mean)
    d = (mean - u0).astype(BF16)
    y_pool = jnp.dot(d, wpool_ref[...], preferred_element_type=F32) * pscale_ref[...]
    mix = jnp.dot(y_pool.astype(BF16), wout_ref[0:POOL_W, :], preferred_element_type=F32)

    a = halo_ref[:, POOL_W:POOL_W + CONV_W]
    gate = halo_ref[:, POOL_W + CONV_W:]
    halo_ref[:, 0:CONV_W] = a * jax.nn.sigmoid(gate)
    conv = jnp.zeros((TQ, CONV_W), F32) + dwb_ref[...]
    for k in range(CONV_K):
        conv = conv + halo_ref[pl.ds(HALO - CONV_K // 2 + k, TQ), 0:CONV_W] * dw_ref[k:k + 1, :]
    hc = _layer_norm(conv, clng_ref[...], clnb_ref[...])
    hc = hc * jax.nn.sigmoid(hc)
    y_conv = jnp.dot(hc.astype(BF16), wpw_ref[...], preferred_element_type=F32) + bpw_ref[...]
    mix = mix + jnp.dot(y_conv.astype(BF16), wout_ref[POOL_W + NA_W:, :], preferred_element_type=F32)

    lane = lax.broadcasted_iota(I32, (TQ, LANES), 1)
    low = lane < NA_HEAD_DIM
    scale = NA_HEAD_DIM ** -0.5
    for p in range(NA_HEADS // 2):
        cs = slice(p * LANES, (p + 1) * LANES)
        qp = q_ref[:, cs].astype(F32) * scale
        qs = jnp.concatenate([jnp.where(low, qp, 0.0), jnp.where(low, 0.0, qp)], axis=0).astype(BF16)
        kk = jnp.concatenate([kp_ref[:, cs], kc_ref[:, cs], kn_ref[:, cs]], axis=0)
        vv = jnp.concatenate([vp_ref[:, cs], vc_ref[:, cs], vn_ref[:, cs]], axis=0)
        s = lax.dot_general(qs, kk, (((1,), (1,)), ((), ())), preferred_element_type=F32)
        s = s + bias_ref[0, p]
        m = jnp.max(s, axis=-1, keepdims=True)
        e = jnp.exp(s - m)
        l = jnp.sum(e, axis=-1, keepdims=True)
        o = jnp.dot(e.astype(BF16), vv, preferred_element_type=F32) / l
        o_pair = jnp.where(low, o[0:TQ], o[TQ:])
        r0 = POOL_W + p * LANES
        mix = mix + jnp.dot(o_pair.astype(BF16), wout_ref[r0:r0 + LANES, :], preferred_element_type=F32)

    x1 = _layer_norm(alpha * x_ref[...] + mix + bout_ref[...], lng_ref[...], lnb_ref[...])
    x1_ref[...] = x1

    lt = lax.dot_general(wr_ref[...], x1.astype(BF16), (((1,), (1,)), ((), ())),
                         preferred_element_type=F32) + br_ref[...]
    eidx = lax.broadcasted_iota(I32, (N_EXPERTS, TQ), 0)
    work = lt
    vals, idxs = [], []
    for _ in range(TOP_K):
        mk = jnp.max(work, axis=0, keepdims=True)
        ik = jnp.min(jnp.where(work == mk, eidx, N_EXPERTS), axis=0, keepdims=True)
        vals.append(mk)
        idxs.append(ik)
        work = jnp.where(eidx == ik, -jnp.inf, work)
    ex = [jnp.exp(v - vals[0]) for v in vals]
    den = ex[0] + ex[1] + ex[2] + ex[3]
    topp_ref[...] = jnp.concatenate([e_ / den for e_ in ex], axis=0)
    topi_ref[...] = jnp.concatenate(idxs, axis=0)

    run = base_ref[...]
    ranks = []
    for k in range(TOP_K):
        hot = (eidx == idxs[k]).astype(F32)
        before = jnp.dot(hot.astype(BF16), tri_ref[...], preferred_element_type=F32)
        ranks.append(jnp.sum(hot * (run + before), axis=0, keepdims=True))
        run = run + jnp.sum(hot, axis=1, keepdims=True)
    rank_ref[...] = jnp.concatenate(ranks, axis=0).astype(I32)
    base_ref[...] = run
    counts_ref[...] = jnp.broadcast_to(run, counts_ref.shape).astype(I32)


def _mixer(x, qkv, rest, bias_tab, wpool_bd, pscale, dw, dwb, clng, clnb, wpw, bpw,
           wout, bout, lng, lnb, wr_t, br, tri, *, layer, seq, alpha):
    n = x.shape[0]
    nt = n // TQ
    tps = seq // TQ
    hb = TQ // HALO

    def prev_t(i):
        return jnp.where(i % tps == 0, i, i - 1)

    def next_t(i):
        return jnp.where(i % tps == tps - 1, i, i + 1)

    def variant(i):
        ib = i % tps
        return jnp.where(ib == 0, 0, jnp.where(ib == tps - 1, 2, 1))

    def const(shape):
        return pl.BlockSpec(shape, lambda i: tuple(0 for _ in shape))

    in_specs = [
        pl.BlockSpec((TQ, D_MODEL), lambda i: (i, 0)),
        pl.BlockSpec((TQ, NA_W), lambda i: (i, 0)),
        pl.BlockSpec((TQ, NA_W), lambda i: (prev_t(i), 1)),
        pl.BlockSpec((TQ, NA_W), lambda i: (i, 1)),
        pl.BlockSpec((TQ, NA_W), lambda i: (next_t(i), 1)),
        pl.BlockSpec((TQ, NA_W), lambda i: (prev_t(i), 2)),
        pl.BlockSpec((TQ, NA_W), lambda i: (i, 2)),
        pl.BlockSpec((TQ, NA_W), lambda i: (next_t(i), 2)),
        pl.BlockSpec((TQ, POOL_W + 2 * CONV_W), lambda i: (i, 0)),
        pl.BlockSpec((HALO, POOL_W + 2 * CONV_W), lambda i: (jnp.maximum(i * hb - 1, 0), 0)),
        pl.BlockSpec((HALO, POOL_W + 2 * CONV_W), lambda i: (jnp.minimum((i + 1) * hb, nt * hb - 1), 0)),
        pl.BlockSpec((None, 1, NA_HEADS // 2, 2 * TQ, NKEY), lambda i: (layer, variant(i), 0, 0, 0)),
        const((POOL_W, POOL_W)), const((1, POOL_W)),
        const((CONV_K, CONV_W)), const((1, CONV_W)), const((1, CONV_W)), const((1, CONV_W)),
        const((CONV_W, CONV_W)), const((1, CONV_W)),
        const((D_MODEL, D_MODEL)), const((1, D_MODEL)), const((1, D_MODEL)), const((1, D_MODEL)),
        const((N_EXPERTS, D_MODEL)), const((N_EXPERTS, 1)), const((TQ, TQ)),
    ]
    out_specs = [
        pl.BlockSpec((TQ, D_MODEL), lambda i: (i, 0)),
        pl.BlockSpec((TOP_K, TQ), lambda i: (0, i)),
        pl.BlockSpec((TOP_K, TQ), lambda i: (0, i)),
        pl.BlockSpec((TOP_K, TQ), lambda i: (0, i)),
        pl.BlockSpec((N_EXPERTS, LANES), lambda i: (0, 0)),
    ]
    out_shape = [
        jax.ShapeDtypeStruct((n, D_MODEL), F32),
        jax.ShapeDtypeStruct((TOP_K, n), I32),
        jax.ShapeDtypeStruct((TOP_K, n), F32),
        jax.ShapeDtypeStruct((TOP_K, n), I32),
        jax.ShapeDtypeStruct((N_EXPERTS, LANES), I32),
    ]
    return pl.pallas_call(
        functools.partial(_mixer_kernel, tps, alpha),
        grid=(nt,),
        in_specs=in_specs,
        out_specs=out_specs,
        out_shape=out_shape,
        scratch_shapes=[
            pltpu.VMEM((TQ + 2 * HALO, POOL_W + 2 * CONV_W), F32),
            pltpu.VMEM((N_EXPERTS, 1), F32),
        ],
        compiler_params=pltpu.CompilerParams(
            dimension_semantics=("arbitrary",), vmem_limit_bytes=VMEM_LIMIT),
        name="mixer",
    )(x, qkv, qkv, qkv, qkv, qkv, qkv, qkv, rest, rest, rest, bias_tab,
      wpool_bd, pscale, dw, dwb, clng, clnb, wpw, bpw, wout, bout, lng, lnb, wr_t, br, tri)


def _dispatch_kernel(pos_ref, x_ref, xs_init_hbm, xs_hbm, sem):
    del xs_init_hbm
    for k in range(TOP_K):
        for r in range(TQ):
            pltpu.make_async_copy(
                x_ref.at[pl.ds(r, 1), :], xs_hbm.at[pl.ds(pos_ref[k, r], 1), :], sem).start(priority=r % 2)
    for k in range(TOP_K):
        pltpu.make_async_copy(x_ref, xs_hbm.at[pl.ds(0, TQ), :], sem).wait()


def _dispatch(x1, pos, xs_init):
    n = x1.shape[0]
    return pl.pallas_call(
        _dispatch_kernel,
        grid=(n // TQ,),
        in_specs=[
            pl.BlockSpec((TOP_K, TQ), lambda i: (0, i), memory_space=pltpu.SMEM),
            pl.BlockSpec((TQ, D_MODEL), lambda i: (i, 0)),
            pl.BlockSpec(memory_space=pl.ANY),
        ],
        out_specs=pl.BlockSpec(memory_space=pl.ANY),
        out_shape=jax.ShapeDtypeStruct(xs_init.shape, xs_init.dtype),
        scratch_shapes=[pltpu.SemaphoreType.DMA(())],
        input_output_aliases={2: 0},
        compiler_params=pltpu.CompilerParams(dimension_semantics=("arbitrary",)),
        name="dispatch",
    )(pos, x1, xs_init)


def _experts_kernel(te_ref, na_ref, xs_ref, wg_ref, bg_ref, wu_ref, bu_ref, wd_ref, bd_ref,
                    y_ref, wg_bf, wu_bf, wd_bf):
    t = pl.program_id(0)
    active = t < na_ref[0]
    new_expert = jnp.logical_or(t == 0, te_ref[t] != te_ref[jnp.maximum(t - 1, 0)])

    @pl.when(jnp.logical_and(active, new_expert))
    def _():
        wg_bf[...] = wg_ref[...].astype(BF16)
        wu_bf[...] = wu_ref[...].astype(BF16)
        wd_bf[...] = wd_ref[...].astype(BF16)

    @pl.when(active)
    def _():
        x = xs_ref[...].astype(BF16)
        g = jnp.minimum(jnp.dot(x, wg_bf[...], preferred_element_type=F32) + bg_ref[...], SWIGLU_LIMIT)
        u = jnp.clip(jnp.dot(x, wu_bf[...], preferred_element_type=F32) + bu_ref[...],
                     -SWIGLU_LIMIT, SWIGLU_LIMIT)
        act = (u + 1.0) * g * jax.nn.sigmoid(SWIGLU_ALPHA * g)
        y_ref[...] = jnp.dot(act.astype(BF16), wd_bf[...], preferred_element_type=F32) + bd_ref[...]

    @pl.when(jnp.logical_not(active))
    def _():
        y_ref[...] = jnp.zeros_like(y_ref)


def _experts(layer, tile_expert, n_active, xs, wg, bg, wu, bu, wd, bd):
    m_pad = xs.shape[0]
    n_tiles = m_pad // TM_E

    def xmap(t, te, na):
        return (jnp.minimum(t, na[0] - 1), 0)

    def wmap(t, te, na):
        return (layer, te[t], 0, 0)

    grid_spec = pltpu.PrefetchScalarGridSpec(
        num_scalar_prefetch=2,
        grid=(n_tiles,),
        in_specs=[
            pl.BlockSpec((TM_E, D_MODEL), xmap),
            pl.BlockSpec((None, None, D_MODEL, D_MODEL), wmap),
            pl.BlockSpec((None, None, 1, D_MODEL), wmap),
            pl.BlockSpec((None, None, D_MODEL, D_MODEL), wmap),
            pl.BlockSpec((None, None, 1, D_MODEL), wmap),
            pl.BlockSpec((None, None, D_MODEL, D_MODEL), wmap),
            pl.BlockSpec((None, None, 1, D_MODEL), wmap),
        ],
        out_specs=pl.BlockSpec((TM_E, D_MODEL), lambda t, te, na: (t, 0)),
        scratch_shapes=[pltpu.VMEM((D_MODEL, D_MODEL), BF16)] * 3,
    )
    return pl.pallas_call(
        _experts_kernel,
        grid_spec=grid_spec,
        out_shape=jax.ShapeDtypeStruct((m_pad, D_MODEL), F32),
        compiler_params=pltpu.CompilerParams(
            dimension_semantics=("arbitrary",), vmem_limit_bytes=VMEM_LIMIT),
        name="experts",
    )(tile_expert, n_active, xs, wg, bg, wu, bu, wd, bd)


def _combine_kernel(alpha, pos_ref, x1_ref, gates_ref, lng_ref, lnb_ref, ys_hbm, out_ref, buf, sem):
    for k in range(TOP_K):
        for r in range(TQ):
            pltpu.make_async_copy(
                ys_hbm.at[pl.ds(pos_ref[k, r], 1), :], buf.at[k, pl.ds(r, 1), :], sem).start(priority=r % 2)
    for k in range(TOP_K):
        pltpu.make_async_copy(ys_hbm.at[pl.ds(0, TQ), :], buf.at[k], sem).wait()

    gates = gates_ref[...]
    ffn = buf[0] * gates[:, 0:1]
    for k in range(1, TOP_K):
        ffn = ffn + buf[k] * gates[:, k:k + 1]
    out_ref[...] = _layer_norm(alpha * x1_ref[...] + ffn, lng_ref[...], lnb_ref[...])


def _combine(pos, x1, gates_tm, lng, lnb, ys, *, alpha):
    n = x1.shape[0]
    return pl.pallas_call(
        functools.partial(_combine_kernel, alpha),
        grid=(n // TQ,),
        in_specs=[
            pl.BlockSpec((TOP_K, TQ), lambda i: (0, i), memory_space=pltpu.SMEM),
            pl.BlockSpec((TQ, D_MODEL), lambda i: (i, 0)),
            pl.BlockSpec((TQ, TOP_K), lambda i: (i, 0)),
            pl.BlockSpec((1, D_MODEL), lambda i: (0, 0)),
            pl.BlockSpec((1, D_MODEL), lambda i: (0, 0)),
            pl.BlockSpec(memory_space=pl.ANY),
        ],
        out_specs=pl.BlockSpec((TQ, D_MODEL), lambda i: (i, 0)),
        out_shape=jax.ShapeDtypeStruct((n, D_MODEL), F32),
        scratch_shapes=[pltpu.VMEM((TOP_K, TQ, D_MODEL), F32), pltpu.SemaphoreType.DMA(())],
        compiler_params=pltpu.CompilerParams(
            dimension_semantics=("arbitrary",), vmem_limit_bytes=VMEM_LIMIT),
        name="combine",
    )(pos, x1, gates_tm, lng, lnb, ys)


def kernel(x, w_in, b_in, w_pool, pool_scale, rpb, conv_dw, conv_dw_b, conv_ln_g, conv_ln_b,
           w_conv_pw, b_conv_pw, w_out, b_out, ln1_g, ln1_b, w_router, b_router,
           w_gate, b_gate, w_up, b_up, w_down, b_down, ln2_g, ln2_b):
    batch, seq, d = x.shape
    depth = w_in.shape[0]
    n = batch * seq
    rows = seq // GRID_W
    alpha = (2.0 * depth) ** 0.25
    off_q, off_k, off_v = POOL_W, POOL_W + NA_W, POOL_W + 2 * NA_W
    off_ca = off_v + NA_W
    m_pad = n * TOP_K + N_EXPERTS * TM_E
    n_tiles = m_pad // TM_E

    tri = (np.arange(TQ)[:, None] < np.arange(TQ)[None, :]).astype(np.float32)
    tri = jnp.asarray(tri, BF16)
    row2 = lambda v: v.reshape(1, -1)
    bias_tabs = _attn_bias_tables(rpb, rows)
    experts_iota = jnp.arange(N_EXPERTS, dtype=I32)
    tile_starts = jnp.arange(n_tiles, dtype=I32) * TM_E
    b_gate4 = b_gate.reshape(depth, N_EXPERTS, 1, d)
    b_up4 = b_up.reshape(depth, N_EXPERTS, 1, d)
    b_down4 = b_down.reshape(depth, N_EXPERTS, 1, d)

    h = x.reshape(n, d)
    for l in range(depth):
        wqkv = w_in[l][:, off_q:off_ca].astype(BF16)
        bqkv = row2(b_in[l][off_q:off_ca])
        wrest = jnp.concatenate([w_in[l][:, :off_q], w_in[l][:, off_ca:]], axis=1).astype(BF16)
        brest = row2(jnp.concatenate([b_in[l][:off_q], b_in[l][off_ca:]]))
        qkv, rest = _inproj(h, wqkv, bqkv, wrest, brest)

        wpool_bd = jax.scipy.linalg.block_diag(*[w_pool[l][g] for g in range(len(POOL_WINDOWS))]).astype(BF16)
        x1, top_i, top_p, rank, counts = _mixer(
            h, qkv, rest, bias_tabs, wpool_bd, row2(pool_scale[l]),
            conv_dw[l].reshape(CONV_K, CONV_W), row2(conv_dw_b[l]), row2(conv_ln_g[l]), row2(conv_ln_b[l]),
            w_conv_pw[l].astype(BF16), row2(b_conv_pw[l]),
            w_out[l].astype(BF16), row2(b_out[l]), row2(ln1_g[l]), row2(ln1_b[l]),
            w_router[l].T.astype(BF16), b_router[l].reshape(N_EXPERTS, 1), tri,
            layer=l, seq=seq, alpha=alpha)

        cnt = counts[:, 0]
        cpad = ((cnt + TM_E - 1) // TM_E) * TM_E
        ends = jnp.cumsum(cpad)
        off = ends - cpad
        hot = top_i[None] == experts_iota[:, None, None]
        pos = rank + jnp.sum(jnp.where(hot, off[:, None, None], 0), axis=0)
        n_active = (ends[-1] // TM_E).astype(I32)
        last_start = jnp.minimum(tile_starts, ends[-1] - TM_E)
        tile_expert = jnp.sum((ends[None, :] <= last_start[:, None]).astype(I32), axis=1)

        xs = _dispatch(x1, pos, jnp.zeros((m_pad, d), F32) if l == 0 else xs)
        ys = _experts(l, tile_expert, n_active.reshape(1), xs,
                      w_gate, b_gate4, w_up, b_up4, w_down, b_down4)
        h = _combine(pos, x1, top_p.T, row2(ln2_g[l]), row2(ln2_b[l]), ys, alpha=alpha)
    return h.reshape(batch, seq, d)
```

```python
import functools

import jax
import jax.numpy as jnp
import numpy as np
from jax import lax
from jax.experimental import pallas as pl
from jax.experimental.pallas import tpu as pltpu

F32 = jnp.float32
BF16 = jnp.bfloat16
I32 = jnp.int32

D_MODEL = 1024
GRID_W = 64
POOL_W = 256
POOL_WINDOWS = (2, 4, 8, 16)
POOL_GROUP_W = 64
NA_W = 512
NA_HEADS = 8
NA_HEAD_DIM = 64
NA_ROWS = 8
NA_COLS = 16
CONV_W = 256
CONV_K = 31
N_EXPERTS = 32
TOP_K = 4
SWIGLU_ALPHA = 1.702
SWIGLU_LIMIT = 7.0
LN_EPS = 1e-5
NEG_INF = -1e30

LANES = 128
TM_IN = 512
TQ = 256
TQ_ROWS = TQ // GRID_W
NKEY = 3 * TQ
HALO = 16
TM_E = 512
VMEM_LIMIT = 56 * 1024 * 1024


def _layer_norm(v, g, b):
    mu = jnp.mean(v, axis=-1, keepdims=True)
    c = v - mu
    var = jnp.mean(c * c, axis=-1, keepdims=True)
    return c * lax.rsqrt(var + LN_EPS) * g + b


def _inproj_kernel(x_ref, wqkv_ref, bqkv_ref, wrest_ref, brest_ref, qkv_ref, rest_ref):
    x = x_ref[...].astype(BF16)
    qkv = jnp.dot(x, wqkv_ref[...], preferred_element_type=F32) + bqkv_ref[...]
    qkv_ref[...] = qkv.astype(BF16)
    rest_ref[...] = jnp.dot(x, wrest_ref[...], preferred_element_type=F32) + brest_ref[...]


def _inproj(x, wqkv, bqkv, wrest, brest):
    n = x.shape[0]
    return pl.pallas_call(
        _inproj_kernel,
        grid=(n // TM_IN,),
        in_specs=[
            pl.BlockSpec((TM_IN, D_MODEL), lambda i: (i, 0)),
            pl.BlockSpec((D_MODEL, 3 * NA_W), lambda i: (0, 0)),
            pl.BlockSpec((1, 3 * NA_W), lambda i: (0, 0)),
            pl.BlockSpec((D_MODEL, POOL_W + 2 * CONV_W), lambda i: (0, 0)),
            pl.BlockSpec((1, POOL_W + 2 * CONV_W), lambda i: (0, 0)),
        ],
        out_specs=[
            pl.BlockSpec((TM_IN, 3 * NA_W), lambda i: (i, 0)),
            pl.BlockSpec((TM_IN, POOL_W + 2 * CONV_W), lambda i: (i, 0)),
        ],
        out_shape=[
            jax.ShapeDtypeStruct((n, 3 * NA_W), BF16),
            jax.ShapeDtypeStruct((n, POOL_W + 2 * CONV_W), F32),
        ],
        compiler_params=pltpu.CompilerParams(
            dimension_semantics=("arbitrary",), vmem_limit_bytes=VMEM_LIMIT),
        name="inproj",
    )(x, wqkv, bqkv, wrest, brest)


def _attn_bias_tables(rpb, rows):
    kr_win = min(NA_ROWS, rows)
    n_tiles = rows // TQ_ROWS
    n_r, n_c = 2 * NA_ROWS - 1, 2 * NA_COLS - 1
    col = np.arange(GRID_W)
    c_off = np.clip(col[None, :] - col[:, None] + NA_COLS - 1, 0, n_c - 1)
    c_hot = (c_off[..., None] == np.arange(n_c)).astype(np.float32)
    sc = np.clip(col - NA_COLS // 2, 0, GRID_W - NA_COLS)[:, None]
    c_ok = (col[None, :] >= sc) & (col[None, :] < sc + NA_COLS)
    r_hots, masks = [], []
    for tile in (0, 1, n_tiles - 1):
        qr = tile * TQ_ROWS + np.arange(TQ_ROWS)
        slot_tile = np.repeat(np.array([tile - 1, tile, tile + 1]), TQ_ROWS)
        kr = slot_tile * TQ_ROWS + np.tile(np.arange(TQ_ROWS), 3)
        slot_ok = (slot_tile >= 0) & (slot_tile < n_tiles)
        sr = np.clip(qr - kr_win // 2, 0, rows - kr_win)[:, None]
        r_ok = (kr[None] >= sr) & (kr[None] < sr + kr_win) & slot_ok[None]
        r_off = np.clip(kr[None] - qr[:, None] + NA_ROWS - 1, 0, n_r - 1)
        r_hots.append((r_off[..., None] == np.arange(n_r)).astype(np.float32))
        masks.append(r_ok[:, None, :, None] & c_ok[None, :, None, :])
    r_hot = jnp.asarray(np.stack(r_hots))
    mask = jnp.asarray(np.stack(masks))[None, :, None]
    hi = lax.Precision.HIGHEST
    by_col = jnp.einsum("lhrc,xyc->lhrxy", rpb.astype(F32), jnp.asarray(c_hot), precision=hi)
    bias = jnp.einsum("vasr,lhrxy->lvhaxsy", r_hot, by_col, precision=hi)
    bias = jnp.where(mask, bias, NEG_INF)
    return bias.reshape(rpb.shape[0], 3, NA_HEADS // 2, 2 * TQ, NKEY)


def _mixer_kernel(tiles_per_seq, alpha,
                  x_ref, q_ref, kp_ref, kc_ref, kn_ref, vp_ref, vc_ref, vn_ref,
                  rc_ref, rp_ref, rn_ref, bias_ref,
                  wpool_ref, pscale_ref, dw_ref, dwb_ref, clng_ref, clnb_ref, wpw_ref, bpw_ref,
                  wout_ref, bout_ref, lng_ref, lnb_ref, wr_ref, br_ref, tri_ref,
                  x1_ref, topi_ref, topp_ref, rank_ref, counts_ref,
                  halo_ref, base_ref):
    i = pl.program_id(0)
    ib = i % tiles_per_seq
    has_prev = ib > 0
    has_next = ib < tiles_per_seq - 1

    @pl.when(i == 0)
    def _():
        base_ref[...] = jnp.zeros_like(base_ref)

    halo_ref[0:HALO, :] = jnp.where(has_prev, rp_ref[...], 0.0)
    halo_ref[HALO:HALO + TQ, :] = rc_ref[...]
    halo_ref[HALO + TQ:, :] = jnp.where(has_next, rn_ref[...], 0.0)

    def u_at(off):
        return halo_ref[pl.ds(HALO + off, TQ), 0:POOL_W]

    t_seq = ib * TQ + lax.broadcasted_iota(I32, (TQ, 1), 0)
    seq_len = tiles_per_seq * TQ
    group = lax.broadcasted_iota(I32, (1, POOL_W), 1) // POOL_GROUP_W
    u0 = u_at(0)
    acc = u0
    mean = jnp.zeros((TQ, POOL_W), F32)
    done = 0
    for g, w in enumerate(POOL_WINDOWS):
        half = w // 2
        for o in range(done + 1, half + 1):
            acc = acc + u_at(-o) + (u_at(o - 1) if o > 1 else 0.0)
        done = half
        cnt = (jnp.minimum(t_seq + half, seq_len) - jnp.maximum(t_seq - half, 0)).astype(F32)
        mean = jnp.where(group == g, acc / cnt, mean)
    d = (mean - u0).astype(BF16)
    y_pool = jnp.dot(d, wpool_ref[...], preferred_element_type=F32) * pscale_ref[...]
    mix = jnp.dot(y_pool.astype(BF16), wout_ref[0:POOL_W, :], preferred_element_type=F32)

    a = halo_ref[:, POOL_W:POOL_W + CONV_W]
    gate = halo_ref[:, POOL_W + CONV_W:]
    halo_ref[:, 0:CONV_W] = a * jax.nn.sigmoid(gate)
    conv = jnp.zeros((TQ, CONV_W), F32) + dwb_ref[...]
    for k in range(CONV_K):
        conv = conv + halo_ref[pl.ds(HALO - CONV_K // 2 + k, TQ), 0:CONV_W] * dw_ref[k:k + 1, :]
    hc = _layer_norm(conv, clng_ref[...], clnb_ref[...])
    hc = hc * jax.nn.sigmoid(hc)
    y_conv = jnp.dot(hc.astype(BF16), wpw_ref[...], preferred_element_type=F32) + bpw_ref[...]
    mix = mix + jnp.dot(y_conv.astype(BF16), wout_ref[POOL_W + NA_W:, :], preferred_element_type=F32)

    lane = lax.broadcasted_iota(I32, (TQ, LANES), 1)
    low = lane < NA_HEAD_DIM
    scale = NA_HEAD_DIM ** -0.5
    for p in range(NA_HEADS // 2):
        cs = slice(p * LANES, (p + 1) * LANES)
        qp = q_ref[:, cs].astype(F32) * scale
        qs = jnp.concatenate([jnp.where(low, qp, 0.0), jnp.where(low, 0.0, qp)], axis=0).astype(BF16)
        kk = jnp.concatenate([kp_ref[:, cs], kc_ref[:, cs], kn_ref[:, cs]], axis=0)
        vv = jnp.concatenate([vp_ref[:, cs], vc_ref[:, cs], vn_ref[:, cs]], axis=0)
        s = lax.dot_general(qs, kk, (((1,), (1,)), ((), ())), preferred_element_type=F32)
        s = s + bias_ref[0, p]
        m = jnp.max(s, axis=-1, keepdims=True)
        e = jnp.exp(s - m)
        l = jnp.sum(e, axis=-1, keepdims=True)
        o = jnp.dot(e.astype(BF16), vv, preferred_element_type=F32) / l
        o_pair = jnp.where(low, o[0:TQ], o[TQ:])
        r0 = POOL_W + p * LANES
        mix = mix + jnp.dot(o_pair.astype(BF16), wout_ref[r0:r0 + LANES, :], preferred_element_type=F32)

    x1 = _layer_norm(alpha * x_ref[...] + mix + bout_ref[...], lng_ref[...], lnb_ref[...])
    x1_ref[...] = x1

    lt = lax.dot_general(wr_ref[...], x1.astype(BF16), (((1,), (1,)), ((), ())),
                         preferred_element_type=F32) + br_ref[...]
    eidx = lax.broadcasted_iota(I32, (N_EXPERTS, TQ), 0)
    work = lt
    vals, idxs = [], []
    for _ in range(TOP_K):
        mk = jnp.max(work, axis=0, keepdims=True)
        ik = jnp.min(jnp.where(work == mk, eidx, N_EXPERTS), axis=0, keepdims=True)
        vals.append(mk)
        idxs.append(ik)
        work = jnp.where(eidx == ik, -jnp.inf, work)
    ex = [jnp.exp(v - vals[0]) for v in vals]
    den = ex[0] + ex[1] + ex[2] + ex[3]
    topp_ref[...] = jnp.concatenate([e_ / den for e_ in ex], axis=0)
    topi_ref[...] = jnp.concatenate(idxs, axis=0)

    run = base_ref[...]
    ranks = []
    for k in range(TOP_K):
        hot = (eidx == idxs[k]).astype(F32)
        before = jnp.dot(hot.astype(BF16), tri_ref[...], preferred_element_type=F32)
        ranks.append(jnp.sum(hot * (run + before), axis=0, keepdims=True))
        run = run + jnp.sum(hot, axis=1, keepdims=True)
    rank_ref[...] = jnp.concatenate(ranks, axis=0).astype(I32)
    base_ref[...] = run
    counts_ref[...] = jnp.broadcast_to(run, counts_ref.shape).astype(I32)


def _mixer(x, qkv, rest, bias_tab, wpool_bd, pscale, dw, dwb, clng, clnb, wpw, bpw,
           wout, bout, lng, lnb, wr_t, br, tri, *, layer, seq, alpha):
    n = x.shape[0]
    nt = n // TQ
    tps = seq // TQ
    hb = TQ // HALO

    def prev_t(i):
        return jnp.where(i % tps == 0, i, i - 1)

    def next_t(i):
        return jnp.where(i % tps == tps - 1, i, i + 1)

    def variant(i):
        ib = i % tps
        return jnp.where(ib == 0, 0, jnp.where(ib == tps - 1, 2, 1))

    def const(shape):
        return pl.BlockSpec(shape, lambda i: tuple(0 for _ in shape))

    in_specs = [
        pl.BlockSpec((TQ, D_MODEL), lambda i: (i, 0)),
        pl.BlockSpec((TQ, NA_W), lambda i: (i, 0)),
        pl.BlockSpec((TQ, NA_W), lambda i: (prev_t(i), 1)),
        pl.BlockSpec((TQ, NA_W), lambda i: (i, 1)),
        pl.BlockSpec((TQ, NA_W), lambda i: (next_t(i), 1)),
        pl.BlockSpec((TQ, NA_W), lambda i: (prev_t(i), 2)),
        pl.BlockSpec((TQ, NA_W), lambda i: (i, 2)),
        pl.BlockSpec((TQ, NA_W), lambda i: (next_t(i), 2)),
        pl.BlockSpec((TQ, POOL_W + 2 * CONV_W), lambda i: (i, 0)),
        pl.BlockSpec((HALO, POOL_W + 2 * CONV_W), lambda i: (jnp.maximum(i * hb - 1, 0), 0)),
        pl.BlockSpec((HALO, POOL_W + 2 * CONV_W), lambda i: (jnp.minimum((i + 1) * hb, nt * hb - 1), 0)),
        pl.BlockSpec((None, 1, NA_HEADS // 2, 2 * TQ, NKEY), lambda i: (layer, variant(i), 0, 0, 0)),
        const((POOL_W, POOL_W)), const((1, POOL_W)),
        const((CONV_K, CONV_W)), const((1, CONV_W)), const((1, CONV_W)), const((1, CONV_W)),
        const((CONV_W, CONV_W)), const((1, CONV_W)),
        const((D_MODEL, D_MODEL)), const((1, D_MODEL)), const((1, D_MODEL)), const((1, D_MODEL)),
        const((N_EXPERTS, D_MODEL)), const((N_EXPERTS, 1)), const((TQ, TQ)),
    ]
    out_specs = [
        pl.BlockSpec((TQ, D_MODEL), lambda i: (i, 0)),
        pl.BlockSpec((TOP_K, TQ), lambda i: (0, i)),
        pl.BlockSpec((TOP_K, TQ), lambda i: (0, i)),
        pl.BlockSpec((TOP_K, TQ), lambda i: (0, i)),
        pl.BlockSpec((N_EXPERTS, LANES), lambda i: (0, 0)),
    ]
    out_shape = [
        jax.ShapeDtypeStruct((n, D_MODEL), F32),
        jax.ShapeDtypeStruct((TOP_K, n), I32),
        jax.ShapeDtypeStruct((TOP_K, n), F32),
        jax.ShapeDtypeStruct((TOP_K, n), I32),
        jax.ShapeDtypeStruct((N_EXPERTS, LANES), I32),
    ]
    return pl.pallas_call(
        functools.partial(_mixer_kernel, tps, alpha),
        grid=(nt,),
        in_specs=in_specs,
        out_specs=out_specs,
        out_shape=out_shape,
        scratch_shapes=[
            pltpu.VMEM((TQ + 2 * HALO, POOL_W + 2 * CONV_W), F32),
            pltpu.VMEM((N_EXPERTS, 1), F32),
        ],
        compiler_params=pltpu.CompilerParams(
            dimension_semantics=("arbitrary",), vmem_limit_bytes=VMEM_LIMIT),
        name="mixer",
    )(x, qkv, qkv, qkv, qkv, qkv, qkv, qkv, rest, rest, rest, bias_tab,
      wpool_bd, pscale, dw, dwb, clng, clnb, wpw, bpw, wout, bout, lng, lnb, wr_t, br, tri)


def _dispatch_kernel(pos_ref, x_ref, xs_init_hbm, xs_hbm, sem):
    del xs_init_hbm
    for k in range(TOP_K):
        for r in range(TQ):
            pltpu.make_async_copy(
                x_ref.at[pl.ds(r, 1), :], xs_hbm.at[pl.ds(pos_ref[k, r], 1), :], sem).start(priority=r % 2)
    for k in range(TOP_K):
        pltpu.make_async_copy(x_ref, xs_hbm.at[pl.ds(0, TQ), :], sem).wait()


def _dispatch(x1, pos, xs_init):
    n = x1.shape[0]
    return pl.pallas_call(
        _dispatch_kernel,
        grid=(n // TQ,),
        in_specs=[
            pl.BlockSpec((TOP_K, TQ), lambda i: (0, i), memory_space=pltpu.SMEM),
            pl.BlockSpec((TQ, D_MODEL), lambda i: (i, 0)),
            pl.BlockSpec(memory_space=pl.ANY),
        ],
        out_specs=pl.BlockSpec(memory_space=pl.ANY),
        out_shape=jax.ShapeDtypeStruct(xs_init.shape, xs_init.dtype),
        scratch_shapes=[pltpu.SemaphoreType.DMA(())],
        input_output_aliases={2: 0},
        compiler_params=pltpu.CompilerParams(dimension_semantics=("arbitrary",)),
        name="dispatch",
    )(pos, x1, xs_init)


def _experts_kernel(layer, te_ref, na_ref, first_ref, slot_ref, nxt_ref,
                    xs_ref, bg_ref, bu_ref, bd_ref, wg_hbm, wu_hbm, wd_hbm,
                    y_ref, wbuf, wg_bf, wu_bf, wd_bf, sem):
    t = pl.program_id(0)
    active = t < na_ref[0]
    e = te_ref[t]
    s = slot_ref[t]

    def fetch(expert, slot):
        return [pltpu.make_async_copy(w.at[layer, expert], wbuf.at[j, slot], sem.at[j, slot])
                for j, w in enumerate((wg_hbm, wu_hbm, wd_hbm))]

    @pl.when(t == 0)
    def _():
        for c in fetch(e, s):
            c.start()

    @pl.when(jnp.logical_and(active, first_ref[t] == 1))
    def _():
        for c in fetch(e, s):
            c.wait()

        @pl.when(nxt_ref[t] != e)
        def _():
            for c in fetch(nxt_ref[t], 1 - s):
                c.start()

        wg_bf[...] = wbuf[0, s].astype(BF16)
        wu_bf[...] = wbuf[1, s].astype(BF16)
        wd_bf[...] = wbuf[2, s].astype(BF16)

    @pl.when(active)
    def _():
        x = xs_ref[...].astype(BF16)
        g = jnp.minimum(jnp.dot(x, wg_bf[...], preferred_element_type=F32) + bg_ref[...], SWIGLU_LIMIT)
        u = jnp.clip(jnp.dot(x, wu_bf[...], preferred_element_type=F32) + bu_ref[...],
                     -SWIGLU_LIMIT, SWIGLU_LIMIT)
        act = (u + 1.0) * g * jax.nn.sigmoid(SWIGLU_ALPHA * g)
        y_ref[...] = jnp.dot(act.astype(BF16), wd_bf[...], preferred_element_type=F32) + bd_ref[...]

    @pl.when(jnp.logical_not(active))
    def _():
        y_ref[...] = jnp.zeros_like(y_ref)


def _experts(layer, tile_expert, n_active, tile_first, tile_slot, tile_next, xs, wg, bg, wu, bu, wd, bd):
    m_pad = xs.shape[0]
    n_tiles = m_pad // TM_E

    def xmap(t, te, na, *_):
        return (jnp.minimum(t, na[0] - 1), 0)

    def bmap(t, te, *_):
        return (layer, te[t], 0, 0)

    grid_spec = pltpu.PrefetchScalarGridSpec(
        num_scalar_prefetch=5,
        grid=(n_tiles,),
        in_specs=[
            pl.BlockSpec((TM_E, D_MODEL), xmap),
            pl.BlockSpec((None, None, 1, D_MODEL), bmap),
            pl.BlockSpec((None, None, 1, D_MODEL), bmap),
            pl.BlockSpec((None, None, 1, D_MODEL), bmap),
            pl.BlockSpec(memory_space=pl.ANY),
            pl.BlockSpec(memory_space=pl.ANY),
            pl.BlockSpec(memory_space=pl.ANY),
        ],
        out_specs=pl.BlockSpec((TM_E, D_MODEL), lambda t, *_: (t, 0)),
        scratch_shapes=[
            pltpu.VMEM((3, 2, D_MODEL, D_MODEL), F32),
            pltpu.VMEM((D_MODEL, D_MODEL), BF16),
            pltpu.VMEM((D_MODEL, D_MODEL), BF16),
            pltpu.VMEM((D_MODEL, D_MODEL), BF16),
            pltpu.SemaphoreType.DMA((3, 2)),
        ],
    )
    return pl.pallas_call(
        functools.partial(_experts_kernel, layer),
        grid_spec=grid_spec,
        out_shape=jax.ShapeDtypeStruct((m_pad, D_MODEL), F32),
        compiler_params=pltpu.CompilerParams(
            dimension_semantics=("arbitrary",), vmem_limit_bytes=VMEM_LIMIT),
        name="experts",
    )(tile_expert, n_active, tile_first, tile_slot, tile_next, xs, bg, bu, bd, wg, wu, wd)


def _combine_kernel(alpha, pos_ref, x1_ref, gates_ref, lng_ref, lnb_ref, ys_hbm, out_ref, buf, sem):
    for k in range(TOP_K):
        for r in range(TQ):
            pltpu.make_async_copy(
                ys_hbm.at[pl.ds(pos_ref[k, r], 1), :], buf.at[k, pl.ds(r, 1), :], sem).start(priority=r % 2)
    for k in range(TOP_K):
        pltpu.make_async_copy(ys_hbm.at[pl.ds(0, TQ), :], buf.at[k], sem).wait()

    gates = gates_ref[...]
    ffn = buf[0] * gates[:, 0:1]
    for k in range(1, TOP_K):
        ffn = ffn + buf[k] * gates[:, k:k + 1]
    out_ref[...] = _layer_norm(alpha * x1_ref[...] + ffn, lng_ref[...], lnb_ref[...])


def _combine(pos, x1, gates_tm, lng, lnb, ys, *, alpha):
    n = x1.shape[0]
    return pl.pallas_call(
        functools.partial(_combine_kernel, alpha),
        grid=(n // TQ,),
        in_specs=[
            pl.BlockSpec((TOP_K, TQ), lambda i: (0, i), memory_space=pltpu.SMEM),
            pl.BlockSpec((TQ, D_MODEL), lambda i: (i, 0)),
            pl.BlockSpec((TQ, TOP_K), lambda i: (i, 0)),
            pl.BlockSpec((1, D_MODEL), lambda i: (0, 0)),
            pl.BlockSpec((1, D_MODEL), lambda i: (0, 0)),
            pl.BlockSpec(memory_space=pl.ANY),
        ],
        out_specs=pl.BlockSpec((TQ, D_MODEL), lambda i: (i, 0)),
        out_shape=jax.ShapeDtypeStruct((n, D_MODEL), F32),
        scratch_shapes=[pltpu.VMEM((TOP_K, TQ, D_MODEL), F32), pltpu.SemaphoreType.DMA(())],
        compiler_params=pltpu.CompilerParams(
            dimension_semantics=("arbitrary",), vmem_limit_bytes=VMEM_LIMIT),
        name="combine",
    )(pos, x1, gates_tm, lng, lnb, ys)


def kernel(x, w_in, b_in, w_pool, pool_scale, rpb, conv_dw, conv_dw_b, conv_ln_g, conv_ln_b,
           w_conv_pw, b_conv_pw, w_out, b_out, ln1_g, ln1_b, w_router, b_router,
           w_gate, b_gate, w_up, b_up, w_down, b_down, ln2_g, ln2_b):
    batch, seq, d = x.shape
    depth = w_in.shape[0]
    n = batch * seq
    rows = seq // GRID_W
    alpha = (2.0 * depth) ** 0.25
    off_q, off_k, off_v = POOL_W, POOL_W + NA_W, POOL_W + 2 * NA_W
    off_ca = off_v + NA_W
    m_pad = n * TOP_K + N_EXPERTS * TM_E
    n_tiles = m_pad // TM_E

    tri = (np.arange(TQ)[:, None] < np.arange(TQ)[None, :]).astype(np.float32)
    tri = jnp.asarray(tri, BF16)
    row2 = lambda v: v.reshape(1, -1)
    bias_tabs = _attn_bias_tables(rpb, rows)
    experts_iota = jnp.arange(N_EXPERTS, dtype=I32)
    tile_starts = jnp.arange(n_tiles, dtype=I32) * TM_E
    b_gate4 = b_gate.reshape(depth, N_EXPERTS, 1, d)
    b_up4 = b_up.reshape(depth, N_EXPERTS, 1, d)
    b_down4 = b_down.reshape(depth, N_EXPERTS, 1, d)

    h = x.reshape(n, d)
    for l in range(depth):
        wqkv = w_in[l][:, off_q:off_ca].astype(BF16)
        bqkv = row2(b_in[l][off_q:off_ca])
        wrest = jnp.concatenate([w_in[l][:, :off_q], w_in[l][:, off_ca:]], axis=1).astype(BF16)
        brest = row2(jnp.concatenate([b_in[l][:off_q], b_in[l][off_ca:]]))
        qkv, rest = _inproj(h, wqkv, bqkv, wrest, brest)

        wpool_bd = jax.scipy.linalg.block_diag(*[w_pool[l][g] for g in range(len(POOL_WINDOWS))]).astype(BF16)
        x1, top_i, top_p, rank, counts = _mixer(
            h, qkv, rest, bias_tabs, wpool_bd, row2(pool_scale[l]),
            conv_dw[l].reshape(CONV_K, CONV_W), row2(conv_dw_b[l]), row2(conv_ln_g[l]), row2(conv_ln_b[l]),
            w_conv_pw[l].astype(BF16), row2(b_conv_pw[l]),
            w_out[l].astype(BF16), row2(b_out[l]), row2(ln1_g[l]), row2(ln1_b[l]),
            w_router[l].T.astype(BF16), b_router[l].reshape(N_EXPERTS, 1), tri,
            layer=l, seq=seq, alpha=alpha)

        cnt = counts[:, 0]
        cpad = ((cnt + TM_E - 1) // TM_E) * TM_E
        ends = jnp.cumsum(cpad)
        off = ends - cpad
        hot = top_i[None] == experts_iota[:, None, None]
        pos = rank + jnp.sum(jnp.where(hot, off[:, None, None], 0), axis=0)
        n_active = (ends[-1] // TM_E).astype(I32)
        last_start = jnp.minimum(tile_starts, ends[-1] - TM_E)
        tile_expert = jnp.sum((ends[None, :] <= last_start[:, None]).astype(I32), axis=1)
        owns = cnt > 0
        e_slot = (jnp.cumsum(owns.astype(I32)) - 1) % 2
        later = jnp.logical_and(experts_iota[None, :] > experts_iota[:, None], owns[None, :])
        e_next = jnp.min(jnp.where(later, experts_iota[None, :], N_EXPERTS), axis=1)
        e_next = jnp.where(e_next == N_EXPERTS, experts_iota, e_next)
        tile_hot = tile_expert[:, None] == experts_iota[None, :]
        pick = lambda v: jnp.sum(jnp.where(tile_hot, v[None, :], 0), axis=1).astype(I32)
        tile_first = (pick(off) == tile_starts).astype(I32)
        tile_slot = pick(e_slot)
        tile_next = pick(e_next)

        xs = _dispatch(x1, pos, jnp.zeros((m_pad, d), F32) if l == 0 else xs)
        ys = _experts(l, tile_expert, n_active.reshape(1), tile_first, tile_slot, tile_next, xs,
                      w_gate, b_gate4, w_up, b_up4, w_down, b_down4)
        h = _combine(pos, x1, top_p.T, row2(ln2_g[l]), row2(ln2_b[l]), ys, alpha=alpha)
    return h.reshape(batch, seq, d)
```

```python
import functools

import jax
import jax.numpy as jnp
import numpy as np
from jax import lax
from jax.experimental import pallas as pl
from jax.experimental.pallas import tpu as pltpu

F32 = jnp.float32
BF16 = jnp.bfloat16
I32 = jnp.int32

D_MODEL = 1024
GRID_W = 64
POOL_W = 256
POOL_WINDOWS = (2, 4, 8, 16)
POOL_GROUP_W = 64
NA_W = 512
NA_HEADS = 8
NA_HEAD_DIM = 64
NA_ROWS = 8
NA_COLS = 16
CONV_W = 256
CONV_K = 31
N_EXPERTS = 32
TOP_K = 4
SWIGLU_ALPHA = 1.702
SWIGLU_LIMIT = 7.0
LN_EPS = 1e-5
NEG_INF = -1e30

LANES = 128
TM_IN = 512
TQ = 256
TQ_ROWS = TQ // GRID_W
HALO = 16
SUBLANES = 8
N_CBLK = GRID_W // SUBLANES
CBLK_ROWS = TQ_ROWS * SUBLANES
N_QGRP = GRID_W // NA_COLS
QGRP = TQ // N_QGRP
KBAND_COLS = 2 * NA_COLS
KBAND = TQ_ROWS * KBAND_COLS
NKEY = 3 * KBAND
TM_E = 512
VMEM_LIMIT = 56 * 1024 * 1024


def _layer_norm(v, g, b):
    mu = jnp.mean(v, axis=-1, keepdims=True)
    c = v - mu
    var = jnp.mean(c * c, axis=-1, keepdims=True)
    return c * lax.rsqrt(var + LN_EPS) * g + b


def _band_start_cblk(g):
    return min(max(2 * g - 1, 0), N_CBLK - KBAND_COLS // SUBLANES)


def _stored_row_groups():
    return [r * N_CBLK + cb for cb in range(N_CBLK) for r in range(TQ_ROWS)]


def _inproj_kernel(x_ref, wqkv_ref, bqkv_ref, wrest_ref, brest_ref, qkv_ref, rest_ref):
    x = x_ref[...].astype(BF16)
    rest_ref[...] = jnp.dot(x, wrest_ref[...], preferred_element_type=F32) + brest_ref[...]
    groups = [x_ref[pl.ds(tile * TQ + gn * SUBLANES, SUBLANES), :]
              for tile in range(TM_IN // TQ) for gn in _stored_row_groups()]
    xp = jnp.concatenate(groups, axis=0).astype(BF16)
    qkv = jnp.dot(xp, wqkv_ref[...], preferred_element_type=F32) + bqkv_ref[...]
    qkv_ref[...] = qkv.astype(BF16)


def _inproj(x, wqkv, bqkv, wrest, brest):
    n = x.shape[0]
    return pl.pallas_call(
        _inproj_kernel,
        grid=(n // TM_IN,),
        in_specs=[
            pl.BlockSpec((TM_IN, D_MODEL), lambda i: (i, 0)),
            pl.BlockSpec((D_MODEL, 3 * NA_W), lambda i: (0, 0)),
            pl.BlockSpec((1, 3 * NA_W), lambda i: (0, 0)),
            pl.BlockSpec((D_MODEL, POOL_W + 2 * CONV_W), lambda i: (0, 0)),
            pl.BlockSpec((1, POOL_W + 2 * CONV_W), lambda i: (0, 0)),
        ],
        out_specs=[
            pl.BlockSpec((TM_IN, 3 * NA_W), lambda i: (i, 0)),
            pl.BlockSpec((TM_IN, POOL_W + 2 * CONV_W), lambda i: (i, 0)),
        ],
        out_shape=[
            jax.ShapeDtypeStruct((n, 3 * NA_W), BF16),
            jax.ShapeDtypeStruct((n, POOL_W + 2 * CONV_W), F32),
        ],
        compiler_params=pltpu.CompilerParams(
            dimension_semantics=("arbitrary",), vmem_limit_bytes=VMEM_LIMIT),
        name="inproj",
    )(x, wqkv, bqkv, wrest, brest)


def _attn_bias_tables(rpb, rows):
    kr_win = min(NA_ROWS, rows)
    n_tiles = rows // TQ_ROWS
    n_r, n_c = 2 * NA_ROWS - 1, 2 * NA_COLS - 1
    n_layers = rpb.shape[0]
    c_hots, c_oks = [], []
    for g in range(N_QGRP):
        qc = g * NA_COLS + np.arange(NA_COLS)
        kc = _band_start_cblk(g) * SUBLANES + np.arange(KBAND_COLS)
        sc = np.clip(qc - NA_COLS // 2, 0, GRID_W - NA_COLS)[:, None]
        c_oks.append((kc[None] >= sc) & (kc[None] < sc + NA_COLS))
        c_off = np.clip(kc[None] - qc[:, None] + NA_COLS - 1, 0, n_c - 1)
        c_hots.append((c_off[..., None] == np.arange(n_c)).astype(np.float32))
    r_hots, r_oks = [], []
    for tile in (0, 1, n_tiles - 1):
        qr = tile * TQ_ROWS + np.arange(TQ_ROWS)
        slot_tile = np.array([tile - 1, tile, tile + 1])
        kr = slot_tile[:, None] * TQ_ROWS + np.arange(TQ_ROWS)[None, :]
        slot_ok = ((slot_tile >= 0) & (slot_tile < n_tiles))[:, None]
        sr = np.clip(qr - kr_win // 2, 0, rows - kr_win)[:, None, None]
        r_oks.append((kr[None] >= sr) & (kr[None] < sr + kr_win) & slot_ok[None])
        r_off = np.clip(kr[None] - qr[:, None, None] + NA_ROWS - 1, 0, n_r - 1)
        r_hots.append((r_off[..., None] == np.arange(n_r)).astype(np.float32))
    c_hot = jnp.asarray(np.stack(c_hots))
    r_hot = jnp.asarray(np.stack(r_hots))
    c_ok = np.stack(c_oks).reshape(N_QGRP, 2, SUBLANES, KBAND_COLS // SUBLANES, SUBLANES)
    r_ok = np.stack(r_oks)
    mask = (r_ok[:, None, None, :, None, :, None, :, None]
            & c_ok[None, :, :, None, :, None, :, None, :])
    mask = jnp.asarray(mask)[None, :, None, :, None]
    hi = lax.Precision.HIGHEST
    rpb5 = rpb.astype(F32).reshape(n_layers, NA_HEADS // 2, 2, n_r, n_c)
    by_col = jnp.einsum("lpjrc,gmnc->lpjrgmn", rpb5, c_hot, precision=hi)
    by_col = by_col.reshape(n_layers, NA_HEADS // 2, 2, n_r, N_QGRP, 2, SUBLANES,
                            KBAND_COLS // SUBLANES, SUBLANES)
    bias = jnp.einsum("vqsyr,lpjrgaxbz->lvpgjaqxsbyz", r_hot, by_col, precision=hi)
    bias = jnp.where(mask, bias, NEG_INF)
    return bias.reshape(n_layers, 3, NA_HEADS // 2, N_QGRP, 2 * QGRP, NKEY)


def _mixer_kernel(tiles_per_seq, alpha,
                  x_ref, q_ref, kp_ref, kc_ref, kn_ref, vp_ref, vc_ref, vn_ref,
                  rc_ref, rp_ref, rn_ref, bias_ref,
                  wpool_ref, pscale_ref, dw_ref, dwb_ref, clng_ref, clnb_ref, wpw_ref, bpw_ref,
                  wout_ref, bout_ref, lng_ref, lnb_ref, wr_ref, br_ref, tri_ref,
                  x1_ref, topi_ref, topp_ref, rank_ref, counts_ref,
                  halo_ref, shift_ref, base_ref):
    i = pl.program_id(0)
    ib = i % tiles_per_seq
    has_prev = ib > 0
    has_next = ib < tiles_per_seq - 1

    @pl.when(i == 0)
    def _():
        base_ref[...] = jnp.zeros_like(base_ref)

    halo_ref[0:HALO, :] = jnp.where(has_prev, rp_ref[...], 0.0)
    halo_ref[HALO:HALO + TQ, :] = rc_ref[...]
    halo_ref[HALO + TQ:, :] = jnp.where(has_next, rn_ref[...], 0.0)

    def u_at(off):
        return halo_ref[pl.ds(HALO + off, TQ), 0:POOL_W]

    t_seq = ib * TQ + lax.broadcasted_iota(I32, (TQ, 1), 0)
    seq_len = tiles_per_seq * TQ
    group = lax.broadcasted_iota(I32, (1, POOL_W), 1) // POOL_GROUP_W
    u0 = u_at(0)
    acc = u0
    mean = jnp.zeros((TQ, POOL_W), F32)
    done = 0
    for g, w in enumerate(POOL_WINDOWS):
        half = w // 2
        for o in range(done + 1, half + 1):
            acc = acc + u_at(-o) + (u_at(o - 1) if o > 1 else 0.0)
        done = half
        cnt = (jnp.minimum(t_seq + half, seq_len) - jnp.maximum(t_seq - half, 0)).astype(F32)
        mean = jnp.where(group == g, acc / cnt, mean)
    d = (mean - u0).astype(BF16)
    y_pool = jnp.dot(d, wpool_ref[...], preferred_element_type=F32) * pscale_ref[...]
    mix = jnp.dot(y_pool.astype(BF16), wout_ref[0:POOL_W, :], preferred_element_type=F32)

    a = halo_ref[:, POOL_W:POOL_W + CONV_W]
    gate = halo_ref[:, POOL_W + CONV_W:]
    halo_ref[:, 0:CONV_W] = a * jax.nn.sigmoid(gate)
    for b in range(1, SUBLANES):
        shift_ref[b - 1] = halo_ref[pl.ds(b, TQ + 2 * HALO - SUBLANES), 0:CONV_W]
    conv = jnp.zeros((TQ, CONV_W), F32) + dwb_ref[...]
    for k in range(CONV_K):
        a8, b = divmod(HALO - CONV_K // 2 + k, SUBLANES)
        tap = (halo_ref[pl.ds(a8 * SUBLANES, TQ), 0:CONV_W] if b == 0
               else shift_ref[b - 1, pl.ds(a8 * SUBLANES, TQ), :])
        conv = conv + tap * dw_ref[k:k + 1, :]
    hc = _layer_norm(conv, clng_ref[...], clnb_ref[...])
    hc = hc * jax.nn.sigmoid(hc)
    y_conv = jnp.dot(hc.astype(BF16), wpw_ref[...], preferred_element_type=F32) + bpw_ref[...]
    mix = mix + jnp.dot(y_conv.astype(BF16), wout_ref[POOL_W + NA_W:, :], preferred_element_type=F32)

    low = lax.broadcasted_iota(I32, (QGRP, LANES), 1) < NA_HEAD_DIM
    scale = NA_HEAD_DIM ** -0.5
    for p in range(NA_HEADS // 2):
        cs = slice(p * LANES, (p + 1) * LANES)
        bands = [pl.ds(_band_start_cblk(g) * CBLK_ROWS, KBAND) for g in range(N_QGRP)]
        s_grp = []
        for g in range(N_QGRP):
            qp = q_ref[pl.ds(g * QGRP, QGRP), cs].astype(F32) * scale
            qs = jnp.concatenate([jnp.where(low, qp, 0.0), jnp.where(low, 0.0, qp)], axis=0).astype(BF16)
            kk = jnp.concatenate([kp_ref[bands[g], cs], kc_ref[bands[g], cs], kn_ref[bands[g], cs]], axis=0)
            s_grp.append(lax.dot_general(qs, kk, (((1,), (1,)), ((), ())), preferred_element_type=F32))
        s = jnp.concatenate(s_grp, axis=0) + bias_ref[0, p].reshape(N_QGRP * 2 * QGRP, NKEY)
        m = jnp.max(s, axis=-1, keepdims=True)
        e = jnp.exp(s - m)
        rl = 1.0 / jnp.sum(e, axis=-1, keepdims=True)
        e = e.astype(BF16)
        o_grp = []
        for g in range(N_QGRP):
            vv = jnp.concatenate([vp_ref[bands[g], cs], vc_ref[bands[g], cs], vn_ref[bands[g], cs]], axis=0)
            rows = slice(g * 2 * QGRP, (g + 1) * 2 * QGRP)
            o = jnp.dot(e[rows], vv, preferred_element_type=F32) * rl[rows]
            o_grp.append(jnp.where(low, o[0:QGRP], o[QGRP:]))
        o_pair = jnp.concatenate(
            [o_grp[cb // 2][(cb % 2) * CBLK_ROWS + r * SUBLANES:(cb % 2) * CBLK_ROWS + (r + 1) * SUBLANES]
             for r in range(TQ_ROWS) for cb in range(N_CBLK)], axis=0)
        r0 = POOL_W + p * LANES
        mix = mix + jnp.dot(o_pair.astype(BF16), wout_ref[r0:r0 + LANES, :], preferred_element_type=F32)

    x1 = _layer_norm(alpha * x_ref[...] + mix + bout_ref[...], lng_ref[...], lnb_ref[...])
    x1_ref[...] = x1

    lt = lax.dot_general(wr_ref[...], x1.astype(BF16), (((1,), (1,)), ((), ())),
                         preferred_element_type=F32) + br_ref[...]
    eidx = lax.broadcasted_iota(I32, (N_EXPERTS, TQ), 0)
    work = lt
    vals, idxs = [], []
    for _ in range(TOP_K):
        mk = jnp.max(work, axis=0, keepdims=True)
        ik = jnp.min(jnp.where(work == mk, eidx, N_EXPERTS), axis=0, keepdims=True)
        vals.append(mk)
        idxs.append(ik)
        work = jnp.where(eidx == ik, -jnp.inf, work)
    ex = [jnp.exp(v - vals[0]) for v in vals]
    den = ex[0] + ex[1] + ex[2] + ex[3]
    topp_ref[...] = jnp.concatenate([e_ / den for e_ in ex], axis=0)
    topi_ref[...] = jnp.concatenate(idxs, axis=0)

    run = base_ref[...]
    ranks = []
    for k in range(TOP_K):
        hot = (eidx == idxs[k]).astype(F32)
        before = jnp.dot(hot.astype(BF16), tri_ref[...], preferred_element_type=F32)
        ranks.append(jnp.sum(hot * (run + before), axis=0, keepdims=True))
        run = run + jnp.sum(hot, axis=1, keepdims=True)
    rank_ref[...] = jnp.concatenate(ranks, axis=0).astype(I32)
    base_ref[...] = run
    counts_ref[...] = jnp.broadcast_to(run, counts_ref.shape).astype(I32)


def _mixer(x, qkv, rest, bias_tab, wpool_bd, pscale, dw, dwb, clng, clnb, wpw, bpw,
           wout, bout, lng, lnb, wr_t, br, tri, *, layer, seq, alpha):
    n = x.shape[0]
    nt = n // TQ
    tps = seq // TQ
    hb = TQ // HALO

    def prev_t(i):
        return jnp.where(i % tps == 0, i, i - 1)

    def next_t(i):
        return jnp.where(i % tps == tps - 1, i, i + 1)

    def variant(i):
        ib = i % tps
        return jnp.where(ib == 0, 0, jnp.where(ib == tps - 1, 2, 1))

    def const(shape):
        return pl.BlockSpec(shape, lambda i: tuple(0 for _ in shape))

    in_specs = [
        pl.BlockSpec((TQ, D_MODEL), lambda i: (i, 0)),
        pl.BlockSpec((TQ, NA_W), lambda i: (i, 0)),
        pl.BlockSpec((TQ, NA_W), lambda i: (prev_t(i), 1)),
        pl.BlockSpec((TQ, NA_W), lambda i: (i, 1)),
        pl.BlockSpec((TQ, NA_W), lambda i: (next_t(i), 1)),
        pl.BlockSpec((TQ, NA_W), lambda i: (prev_t(i), 2)),
        pl.BlockSpec((TQ, NA_W), lambda i: (i, 2)),
        pl.BlockSpec((TQ, NA_W), lambda i: (next_t(i), 2)),
        pl.BlockSpec((TQ, POOL_W + 2 * CONV_W), lambda i: (i, 0)),
        pl.BlockSpec((HALO, POOL_W + 2 * CONV_W), lambda i: (jnp.maximum(i * hb - 1, 0), 0)),
        pl.BlockSpec((HALO, POOL_W + 2 * CONV_W), lambda i: (jnp.minimum((i + 1) * hb, nt * hb - 1), 0)),
        pl.BlockSpec((None, 1, NA_HEADS // 2, N_QGRP, 2 * QGRP, NKEY),
                     lambda i: (layer, variant(i), 0, 0, 0, 0)),
        const((POOL_W, POOL_W)), const((1, POOL_W)),
        const((CONV_K, CONV_W)), const((1, CONV_W)), const((1, CONV_W)), const((1, CONV_W)),
        const((CONV_W, CONV_W)), const((1, CONV_W)),
        const((D_MODEL, D_MODEL)), const((1, D_MODEL)), const((1, D_MODEL)), const((1, D_MODEL)),
        const((N_EXPERTS, D_MODEL)), const((N_EXPERTS, 1)), const((TQ, TQ)),
    ]
    out_specs = [
        pl.BlockSpec((TQ, D_MODEL), lambda i: (i, 0)),
        pl.BlockSpec((TOP_K, TQ), lambda i: (0, i)),
        pl.BlockSpec((TOP_K, TQ), lambda i: (0, i)),
        pl.BlockSpec((TOP_K, TQ), lambda i: (0, i)),
        pl.BlockSpec((N_EXPERTS, LANES), lambda i: (0, 0)),
    ]
    out_shape = [
        jax.ShapeDtypeStruct((n, D_MODEL), F32),
        jax.ShapeDtypeStruct((TOP_K, n), I32),
        jax.ShapeDtypeStruct((TOP_K, n), F32),
        jax.ShapeDtypeStruct((TOP_K, n), I32),
        jax.ShapeDtypeStruct((N_EXPERTS, LANES), I32),
    ]
    return pl.pallas_call(
        functools.partial(_mixer_kernel, tps, alpha),
        grid=(nt,),
        in_specs=in_specs,
        out_specs=out_specs,
        out_shape=out_shape,
        scratch_shapes=[
            pltpu.VMEM((TQ + 2 * HALO, POOL_W + 2 * CONV_W), F32),
            pltpu.VMEM((SUBLANES - 1, TQ + 2 * HALO - SUBLANES, CONV_W), F32),
            pltpu.VMEM((N_EXPERTS, 1), F32),
        ],
        compiler_params=pltpu.CompilerParams(
            dimension_semantics=("arbitrary",), vmem_limit_bytes=VMEM_LIMIT),
        name="mixer",
    )(x, qkv, qkv, qkv, qkv, qkv, qkv, qkv, rest, rest, rest, bias_tab,
      wpool_bd, pscale, dw, dwb, clng, clnb, wpw, bpw, wout, bout, lng, lnb, wr_t, br, tri)


def _dispatch_kernel(pos_ref, x_ref, xs_init_hbm, xs_hbm, sem):
    del xs_init_hbm
    for k in range(TOP_K):
        for r in range(TQ):
            pltpu.make_async_copy(
                x_ref.at[pl.ds(r, 1), :], xs_hbm.at[pl.ds(pos_ref[k, r], 1), :], sem).start(priority=r % 2)
    for k in range(TOP_K):
        pltpu.make_async_copy(x_ref, xs_hbm.at[pl.ds(0, TQ), :], sem).wait()


def _dispatch(x1, pos, xs_init):
    n = x1.shape[0]
    return pl.pallas_call(
        _dispatch_kernel,
        grid=(n // TQ,),
        in_specs=[
            pl.BlockSpec((TOP_K, TQ), lambda i: (0, i), memory_space=pltpu.SMEM),
            pl.BlockSpec((TQ, D_MODEL), lambda i: (i, 0)),
            pl.BlockSpec(memory_space=pl.ANY),
        ],
        out_specs=pl.BlockSpec(memory_space=pl.ANY),
        out_shape=jax.ShapeDtypeStruct(xs_init.shape, xs_init.dtype),
        scratch_shapes=[pltpu.SemaphoreType.DMA(())],
        input_output_aliases={2: 0},
        compiler_params=pltpu.CompilerParams(dimension_semantics=("arbitrary",)),
        name="dispatch",
    )(pos, x1, xs_init)


def _experts_kernel(layer, te_ref, na_ref, first_ref, slot_ref, nxt_ref,
                    xs_ref, bg_ref, bu_ref, bd_ref, wg_hbm, wu_hbm, wd_hbm,
                    y_ref, wbuf, wg_bf, wu_bf, wd_bf, sem):
    t = pl.program_id(0)
    active = t < na_ref[0]
    e = te_ref[t]
    s = slot_ref[t]

    def fetch(expert, slot):
        return [pltpu.make_async_copy(w.at[layer, expert], wbuf.at[j, slot], sem.at[j, slot])
                for j, w in enumerate((wg_hbm, wu_hbm, wd_hbm))]

    @pl.when(t == 0)
    def _():
        for c in fetch(e, s):
            c.start()

    @pl.when(jnp.logical_and(active, first_ref[t] == 1))
    def _():
        for c in fetch(e, s):
            c.wait()

        @pl.when(nxt_ref[t] != e)
        def _():
            for c in fetch(nxt_ref[t], 1 - s):
                c.start()

        wg_bf[...] = wbuf[0, s].astype(BF16)
        wu_bf[...] = wbuf[1, s].astype(BF16)
        wd_bf[...] = wbuf[2, s].astype(BF16)

    @pl.when(active)
    def _():
        x = xs_ref[...].astype(BF16)
        g = jnp.minimum(jnp.dot(x, wg_bf[...], preferred_element_type=F32) + bg_ref[...], SWIGLU_LIMIT)
        u = jnp.clip(jnp.dot(x, wu_bf[...], preferred_element_type=F32) + bu_ref[...],
                     -SWIGLU_LIMIT, SWIGLU_LIMIT)
        act = (u + 1.0) * g * jax.nn.sigmoid(SWIGLU_ALPHA * g)
        y_ref[...] = jnp.dot(act.astype(BF16), wd_bf[...], preferred_element_type=F32) + bd_ref[...]

    @pl.when(jnp.logical_not(active))
    def _():
        y_ref[...] = jnp.zeros_like(y_ref)


def _experts(layer, tile_expert, n_active, tile_first, tile_slot, tile_next, xs, wg, bg, wu, bu, wd, bd):
    m_pad = xs.shape[0]
    n_tiles = m_pad // TM_E

    def xmap(t, te, na, *_):
        return (jnp.minimum(t, na[0] - 1), 0)

    def bmap(t, te, *_):
        return (layer, te[t], 0, 0)

    grid_spec = pltpu.PrefetchScalarGridSpec(
        num_scalar_prefetch=5,
        grid=(n_tiles,),
        in_specs=[
            pl.BlockSpec((TM_E, D_MODEL), xmap),
            pl.BlockSpec((None, None, 1, D_MODEL), bmap),
            pl.BlockSpec((None, None, 1, D_MODEL), bmap),
            pl.BlockSpec((None, None, 1, D_MODEL), bmap),
            pl.BlockSpec(memory_space=pl.ANY),
            pl.BlockSpec(memory_space=pl.ANY),
            pl.BlockSpec(memory_space=pl.ANY),
        ],
        out_specs=pl.BlockSpec((TM_E, D_MODEL), lambda t, *_: (t, 0)),
        scratch_shapes=[
            pltpu.VMEM((3, 2, D_MODEL, D_MODEL), F32),
            pltpu.VMEM((D_MODEL, D_MODEL), BF16),
            pltpu.VMEM((D_MODEL, D_MODEL), BF16),
            pltpu.VMEM((D_MODEL, D_MODEL), BF16),
            pltpu.SemaphoreType.DMA((3, 2)),
        ],
    )
    return pl.pallas_call(
        functools.partial(_experts_kernel, layer),
        grid_spec=grid_spec,
        out_shape=jax.ShapeDtypeStruct((m_pad, D_MODEL), F32),
        compiler_params=pltpu.CompilerParams(
            dimension_semantics=("arbitrary",), vmem_limit_bytes=VMEM_LIMIT),
        name="experts",
    )(tile_expert, n_active, tile_first, tile_slot, tile_next, xs, bg, bu, bd, wg, wu, wd)


def _combine_kernel(alpha, pos_ref, x1_ref, gates_ref, lng_ref, lnb_ref, ys_hbm, out_ref, buf, sem):
    for k in range(TOP_K):
        for r in range(TQ):
            pltpu.make_async_copy(
                ys_hbm.at[pl.ds(pos_ref[k, r], 1), :], buf.at[k, pl.ds(r, 1), :], sem).start(priority=r % 2)
    for k in range(TOP_K):
        pltpu.make_async_copy(ys_hbm.at[pl.ds(0, TQ), :], buf.at[k], sem).wait()

    gates = gates_ref[...]
    ffn = buf[0] * gates[:, 0:1]
    for k in range(1, TOP_K):
        ffn = ffn + buf[k] * gates[:, k:k + 1]
    out_ref[...] = _layer_norm(alpha * x1_ref[...] + ffn, lng_ref[...], lnb_ref[...])


def _combine(pos, x1, gates_tm, lng, lnb, ys, *, alpha):
    n = x1.shape[0]
    return pl.pallas_call(
        functools.partial(_combine_kernel, alpha),
        grid=(n // TQ,),
        in_specs=[
            pl.BlockSpec((TOP_K, TQ), lambda i: (0, i), memory_space=pltpu.SMEM),
            pl.BlockSpec((TQ, D_MODEL), lambda i: (i, 0)),
            pl.BlockSpec((TQ, TOP_K), lambda i: (i, 0)),
            pl.BlockSpec((1, D_MODEL), lambda i: (0, 0)),
            pl.BlockSpec((1, D_MODEL), lambda i: (0, 0)),
            pl.BlockSpec(memory_space=pl.ANY),
        ],
        out_specs=pl.BlockSpec((TQ, D_MODEL), lambda i: (i, 0)),
        out_shape=jax.ShapeDtypeStruct((n, D_MODEL), F32),
        scratch_shapes=[pltpu.VMEM((TOP_K, TQ, D_MODEL), F32), pltpu.SemaphoreType.DMA(())],
        compiler_params=pltpu.CompilerParams(
            dimension_semantics=("arbitrary",), vmem_limit_bytes=VMEM_LIMIT),
        name="combine",
    )(pos, x1, gates_tm, lng, lnb, ys)


def kernel(x, w_in, b_in, w_pool, pool_scale, rpb, conv_dw, conv_dw_b, conv_ln_g, conv_ln_b,
           w_conv_pw, b_conv_pw, w_out, b_out, ln1_g, ln1_b, w_router, b_router,
           w_gate, b_gate, w_up, b_up, w_down, b_down, ln2_g, ln2_b):
    batch, seq, d = x.shape
    depth = w_in.shape[0]
    n = batch * seq
    rows = seq // GRID_W
    alpha = (2.0 * depth) ** 0.25
    off_q, off_k, off_v = POOL_W, POOL_W + NA_W, POOL_W + 2 * NA_W
    off_ca = off_v + NA_W
    m_pad = n * TOP_K + N_EXPERTS * TM_E
    n_tiles = m_pad // TM_E

    tri = (np.arange(TQ)[:, None] < np.arange(TQ)[None, :]).astype(np.float32)
    tri = jnp.asarray(tri, BF16)
    row2 = lambda v: v.reshape(1, -1)
    bias_tabs = _attn_bias_tables(rpb, rows)
    experts_iota = jnp.arange(N_EXPERTS, dtype=I32)
    tile_starts = jnp.arange(n_tiles, dtype=I32) * TM_E
    b_gate4 = b_gate.reshape(depth, N_EXPERTS, 1, d)
    b_up4 = b_up.reshape(depth, N_EXPERTS, 1, d)
    b_down4 = b_down.reshape(depth, N_EXPERTS, 1, d)

    h = x.reshape(n, d)
    for l in range(depth):
        wqkv = w_in[l][:, off_q:off_ca].astype(BF16)
        bqkv = row2(b_in[l][off_q:off_ca])
        wrest = jnp.concatenate([w_in[l][:, :off_q], w_in[l][:, off_ca:]], axis=1).astype(BF16)
        brest = row2(jnp.concatenate([b_in[l][:off_q], b_in[l][off_ca:]]))
        qkv, rest = _inproj(h, wqkv, bqkv, wrest, brest)

        wpool_bd = jax.scipy.linalg.block_diag(*[w_pool[l][g] for g in range(len(POOL_WINDOWS))]).astype(BF16)
        x1, top_i, top_p, rank, counts = _mixer(
            h, qkv, rest, bias_tabs, wpool_bd, row2(pool_scale[l]),
            conv_dw[l].reshape(CONV_K, CONV_W), row2(conv_dw_b[l]), row2(conv_ln_g[l]), row2(conv_ln_b[l]),
            w_conv_pw[l].astype(BF16), row2(b_conv_pw[l]),
            w_out[l].astype(BF16), row2(b_out[l]), row2(ln1_g[l]), row2(ln1_b[l]),
            w_router[l].T.astype(BF16), b_router[l].reshape(N_EXPERTS, 1), tri,
            layer=l, seq=seq, alpha=alpha)

        cnt = counts[:, 0]
        cpad = ((cnt + TM_E - 1) // TM_E) * TM_E
        ends = jnp.cumsum(cpad)
        off = ends - cpad
        hot = top_i[None] == experts_iota[:, None, None]
        pos = rank + jnp.sum(jnp.where(hot, off[:, None, None], 0), axis=0)
        n_active = (ends[-1] // TM_E).astype(I32)
        last_start = jnp.minimum(tile_starts, ends[-1] - TM_E)
        tile_expert = jnp.sum((ends[None, :] <= last_start[:, None]).astype(I32), axis=1)
        owns = cnt > 0
        e_slot = (jnp.cumsum(owns.astype(I32)) - 1) % 2
        later = jnp.logical_and(experts_iota[None, :] > experts_iota[:, None], owns[None, :])
        e_next = jnp.min(jnp.where(later, experts_iota[None, :], N_EXPERTS), axis=1)
        e_next = jnp.where(e_next == N_EXPERTS, experts_iota, e_next)
        tile_hot = tile_expert[:, None] == experts_iota[None, :]
        pick = lambda v: jnp.sum(jnp.where(tile_hot, v[None, :], 0), axis=1).astype(I32)
        tile_first = (pick(off) == tile_starts).astype(I32)
        tile_slot = pick(e_slot)
        tile_next = pick(e_next)

        xs = _dispatch(x1, pos, jnp.zeros((m_pad, d), F32) if l == 0 else xs)
        ys = _experts(l, tile_expert, n_active.reshape(1), tile_first, tile_slot, tile_next, xs,
                      w_gate, b_gate4, w_up, b_up4, w_down, b_down4)
        h = _combine(pos, x1, top_p.T, row2(ln2_g[l]), row2(ln2_b[l]), ys, alpha=alpha)
    return h.reshape(batch, seq, d)
```

```python
import functools

import jax
import jax.numpy as jnp
import numpy as np
from jax import lax
from jax.experimental import pallas as pl
from jax.experimental.pallas import tpu as pltpu

F32 = jnp.float32
BF16 = jnp.bfloat16
I32 = jnp.int32

D_MODEL = 1024
GRID_W = 64
POOL_W = 256
POOL_WINDOWS = (2, 4, 8, 16)
POOL_GROUP_W = 64
NA_W = 512
NA_HEADS = 8
NA_HEAD_DIM = 64
NA_ROWS = 8
NA_COLS = 16
CONV_W = 256
CONV_K = 31
N_EXPERTS = 32
TOP_K = 4
SWIGLU_ALPHA = 1.702
SWIGLU_LIMIT = 7.0
LN_EPS = 1e-5
NEG_INF = -1e30

LANES = 128
TM_IN = 512
TQ = 256
TQ_ROWS = TQ // GRID_W
HALO = 16
SUBLANES = 8
N_CBLK = GRID_W // SUBLANES
CBLK_ROWS = TQ_ROWS * SUBLANES
N_QGRP = GRID_W // NA_COLS
QGRP = TQ // N_QGRP
KBAND_COLS = 2 * NA_COLS
KBAND = TQ_ROWS * KBAND_COLS
NKEY = 3 * KBAND
TM_E = 512
VMEM_LIMIT = 56 * 1024 * 1024


def _layer_norm(v, g, b):
    mu = jnp.mean(v, axis=-1, keepdims=True)
    c = v - mu
    var = jnp.mean(c * c, axis=-1, keepdims=True)
    return c * lax.rsqrt(var + LN_EPS) * g + b


def _band_start_cblk(g):
    return min(max(2 * g - 1, 0), N_CBLK - KBAND_COLS // SUBLANES)


def _stored_row_groups():
    return [r * N_CBLK + cb for cb in range(N_CBLK) for r in range(TQ_ROWS)]


def _inproj_kernel(x_ref, wqkv_ref, bqkv_ref, wrest_ref, brest_ref, qkv_ref, rest_ref):
    x = x_ref[...].astype(BF16)
    rest_ref[...] = jnp.dot(x, wrest_ref[...], preferred_element_type=F32) + brest_ref[...]
    groups = [x_ref[pl.ds(tile * TQ + gn * SUBLANES, SUBLANES), :]
              for tile in range(TM_IN // TQ) for gn in _stored_row_groups()]
    xp = jnp.concatenate(groups, axis=0).astype(BF16)
    qkv = jnp.dot(xp, wqkv_ref[...], preferred_element_type=F32) + bqkv_ref[...]
    qkv_ref[...] = qkv.astype(BF16)


def _inproj(x, wqkv, bqkv, wrest, brest):
    n = x.shape[0]
    return pl.pallas_call(
        _inproj_kernel,
        grid=(n // TM_IN,),
        in_specs=[
            pl.BlockSpec((TM_IN, D_MODEL), lambda i: (i, 0)),
            pl.BlockSpec((D_MODEL, 3 * NA_W), lambda i: (0, 0)),
            pl.BlockSpec((1, 3 * NA_W), lambda i: (0, 0)),
            pl.BlockSpec((D_MODEL, POOL_W + 2 * CONV_W), lambda i: (0, 0)),
            pl.BlockSpec((1, POOL_W + 2 * CONV_W), lambda i: (0, 0)),
        ],
        out_specs=[
            pl.BlockSpec((TM_IN, 3 * NA_W), lambda i: (i, 0)),
            pl.BlockSpec((TM_IN, POOL_W + 2 * CONV_W), lambda i: (i, 0)),
        ],
        out_shape=[
            jax.ShapeDtypeStruct((n, 3 * NA_W), BF16),
            jax.ShapeDtypeStruct((n, POOL_W + 2 * CONV_W), F32),
        ],
        compiler_params=pltpu.CompilerParams(
            dimension_semantics=("arbitrary",), vmem_limit_bytes=VMEM_LIMIT),
        name="inproj",
    )(x, wqkv, bqkv, wrest, brest)


def _attn_bias_tables(rpb, rows):
    kr_win = min(NA_ROWS, rows)
    n_tiles = rows // TQ_ROWS
    n_r, n_c = 2 * NA_ROWS - 1, 2 * NA_COLS - 1
    n_layers = rpb.shape[0]
    c_hots, c_oks = [], []
    for g in range(N_QGRP):
        qc = g * NA_COLS + np.arange(NA_COLS)
        kc = _band_start_cblk(g) * SUBLANES + np.arange(KBAND_COLS)
        sc = np.clip(qc - NA_COLS // 2, 0, GRID_W - NA_COLS)[:, None]
        c_oks.append((kc[None] >= sc) & (kc[None] < sc + NA_COLS))
        c_off = np.clip(kc[None] - qc[:, None] + NA_COLS - 1, 0, n_c - 1)
        c_hots.append((c_off[..., None] == np.arange(n_c)).astype(np.float32))
    r_hots, r_oks = [], []
    for tile in (0, 1, n_tiles - 1):
        qr = tile * TQ_ROWS + np.arange(TQ_ROWS)
        slot_tile = np.array([tile - 1, tile, tile + 1])
        kr = slot_tile[:, None] * TQ_ROWS + np.arange(TQ_ROWS)[None, :]
        slot_ok = ((slot_tile >= 0) & (slot_tile < n_tiles))[:, None]
        sr = np.clip(qr - kr_win // 2, 0, rows - kr_win)[:, None, None]
        r_oks.append((kr[None] >= sr) & (kr[None] < sr + kr_win) & slot_ok[None])
        r_off = np.clip(kr[None] - qr[:, None, None] + NA_ROWS - 1, 0, n_r - 1)
        r_hots.append((r_off[..., None] == np.arange(n_r)).astype(np.float32))
    c_hot = jnp.asarray(np.stack(c_hots))
    r_off = np.stack([np.argmax(h, axis=-1) for h in r_hots])
    c_ok = np.stack(c_oks).reshape(N_QGRP, 2, SUBLANES, KBAND_COLS // SUBLANES, SUBLANES)
    r_ok = np.stack(r_oks)
    ok = (r_ok[:, None, None, :, None, :, None, :, None]
          & c_ok[None, :, :, None, :, None, :, None, :])
    ok = np.broadcast_to(ok[:, :, None], (3, N_QGRP, 2) + ok.shape[2:])
    ok = jnp.asarray(ok.reshape(3, N_QGRP, 2 * QGRP, NKEY).astype(np.float32))
    rpb5 = rpb.astype(F32).reshape(n_layers, NA_HEADS // 2, 2, n_r, n_c)
    by_col = jnp.einsum("lpjrc,gmnc->lpgjmrn", rpb5, c_hot, precision=lax.Precision.HIGHEST)
    by_col = by_col.reshape(n_layers * (NA_HEADS // 2) * N_QGRP, 2 * NA_COLS, n_r * KBAND_COLS)
    nb = KBAND_COLS // SUBLANES
    expand = np.zeros((TQ_ROWS, 2, 2, TQ_ROWS, SUBLANES, 2, 2, SUBLANES), np.float32)
    for q in range(TQ_ROWS):
        for j in range(2):
            for a in range(2):
                for x in range(SUBLANES):
                    expand[q, j, a, q, x, j, a, x] = 1.0
    expand = expand.reshape(TQ_ROWS, 2 * QGRP, 2 * NA_COLS)
    select = np.zeros((3, TQ_ROWS, n_r, nb, SUBLANES, 3, nb, TQ_ROWS, SUBLANES), np.float32)
    for v in range(3):
        for q in range(TQ_ROWS):
            for s in range(3):
                for y in range(TQ_ROWS):
                    for b in range(nb):
                        for z in range(SUBLANES):
                            select[v, q, r_off[v, q, s, y], b, z, s, b, y, z] = 1.0
    select = select.reshape(3, TQ_ROWS, n_r * KBAND_COLS, NKEY)
    lane_pad = -(n_r * KBAND_COLS) % LANES
    by_col = jnp.pad(by_col, ((0, 0), (0, 0), (0, lane_pad)))
    select = np.pad(select, ((0, 0), (0, 0), (0, lane_pad), (0, 0)))
    tab = _bias_table_call(by_col, jnp.asarray(expand, BF16), jnp.asarray(select, BF16), ok)
    return tab.reshape(n_layers, NA_HEADS // 2, N_QGRP, 3, 2 * QGRP, NKEY)


def _bias_table_kernel(u_ref, expand_ref, select_ref, ok_ref, out_ref):
    u = u_ref[...]
    parts = []
    for _ in range(3):
        piece = u.astype(BF16)
        parts.append(piece)
        u = u - piece.astype(F32)
    n_k = u.shape[1]
    pieces = jnp.concatenate(parts, axis=1)
    acc = [jnp.zeros((2 * QGRP, NKEY), F32) for _ in range(3)]
    for q in range(TQ_ROWS):
        rows = jnp.dot(expand_ref[q], pieces, preferred_element_type=F32).astype(BF16)
        rows = jnp.concatenate([rows[:, k * n_k:(k + 1) * n_k] for k in range(3)], axis=0)
        for v in range(3):
            sel = jnp.dot(rows, select_ref[v, q], preferred_element_type=F32)
            acc[v] = acc[v] + (sel[0:2 * QGRP] + sel[2 * QGRP:4 * QGRP] + sel[4 * QGRP:])
    for v in range(3):
        out_ref[v] = jnp.where(ok_ref[v, 0] > 0.0, acc[v], NEG_INF)


def _bias_table_call(by_col, expand, select, ok):
    n, n_u, n_k = by_col.shape
    return pl.pallas_call(
        _bias_table_kernel,
        grid=(n,),
        in_specs=[
            pl.BlockSpec((None, n_u, n_k), lambda i: (i, 0, 0)),
            pl.BlockSpec(expand.shape, lambda i: (0, 0, 0)),
            pl.BlockSpec(select.shape, lambda i: (0, 0, 0, 0)),
            pl.BlockSpec((3, 1, 2 * QGRP, NKEY), lambda i: (0, i % N_QGRP, 0, 0)),
        ],
        out_specs=pl.BlockSpec((None, 3, 2 * QGRP, NKEY), lambda i: (i, 0, 0, 0)),
        out_shape=jax.ShapeDtypeStruct((n, 3, 2 * QGRP, NKEY), F32),
        compiler_params=pltpu.CompilerParams(
            dimension_semantics=("arbitrary",), vmem_limit_bytes=VMEM_LIMIT),
        name="bias_table",
    )(by_col, expand, select, ok)


def _mixer_kernel(tiles_per_seq, alpha,
                  x_ref, q_ref, kp_ref, kc_ref, kn_ref, vp_ref, vc_ref, vn_ref,
                  rc_ref, rp_ref, rn_ref, bias_ref,
                  wpool_ref, pscale_ref, dw_ref, dwb_ref, clng_ref, clnb_ref, wpw_ref, bpw_ref,
                  wout_ref, bout_ref, lng_ref, lnb_ref, wr_ref, br_ref, tri_ref,
                  x1_ref, topi_ref, topp_ref, rank_ref, counts_ref,
                  halo_ref, shift_ref, base_ref):
    i = pl.program_id(0)
    ib = i % tiles_per_seq
    has_prev = ib > 0
    has_next = ib < tiles_per_seq - 1

    @pl.when(i == 0)
    def _():
        base_ref[...] = jnp.zeros_like(base_ref)

    halo_ref[0:HALO, :] = jnp.where(has_prev, rp_ref[...], 0.0)
    halo_ref[HALO:HALO + TQ, :] = rc_ref[...]
    halo_ref[HALO + TQ:, :] = jnp.where(has_next, rn_ref[...], 0.0)

    def u_at(off):
        return halo_ref[pl.ds(HALO + off, TQ), 0:POOL_W]

    t_seq = ib * TQ + lax.broadcasted_iota(I32, (TQ, 1), 0)
    seq_len = tiles_per_seq * TQ
    group = lax.broadcasted_iota(I32, (1, POOL_W), 1) // POOL_GROUP_W
    u0 = u_at(0)
    acc = u0
    mean = jnp.zeros((TQ, POOL_W), F32)
    done = 0
    for g, w in enumerate(POOL_WINDOWS):
        half = w // 2
        for o in range(done + 1, half + 1):
            acc = acc + u_at(-o) + (u_at(o - 1) if o > 1 else 0.0)
        done = half
        cnt = (jnp.minimum(t_seq + half, seq_len) - jnp.maximum(t_seq - half, 0)).astype(F32)
        mean = jnp.where(group == g, acc / cnt, mean)
    d = (mean - u0).astype(BF16)
    y_pool = jnp.dot(d, wpool_ref[...], preferred_element_type=F32) * pscale_ref[...]
    mix = jnp.dot(y_pool.astype(BF16), wout_ref[0:POOL_W, :], preferred_element_type=F32)

    a = halo_ref[:, POOL_W:POOL_W + CONV_W]
    gate = halo_ref[:, POOL_W + CONV_W:]
    halo_ref[:, 0:CONV_W] = a * jax.nn.sigmoid(gate)
    for b in range(1, SUBLANES):
        shift_ref[b - 1] = halo_ref[pl.ds(b, TQ + 2 * HALO - SUBLANES), 0:CONV_W]
    conv = jnp.zeros((TQ, CONV_W), F32) + dwb_ref[...]
    for k in range(CONV_K):
        a8, b = divmod(HALO - CONV_K // 2 + k, SUBLANES)
        tap = (halo_ref[pl.ds(a8 * SUBLANES, TQ), 0:CONV_W] if b == 0
               else shift_ref[b - 1, pl.ds(a8 * SUBLANES, TQ), :])
        conv = conv + tap * dw_ref[k:k + 1, :]
    hc = _layer_norm(conv, clng_ref[...], clnb_ref[...])
    hc = hc * jax.nn.sigmoid(hc)
    y_conv = jnp.dot(hc.astype(BF16), wpw_ref[...], preferred_element_type=F32) + bpw_ref[...]
    mix = mix + jnp.dot(y_conv.astype(BF16), wout_ref[POOL_W + NA_W:, :], preferred_element_type=F32)

    low = lax.broadcasted_iota(I32, (QGRP, LANES), 1) < NA_HEAD_DIM
    scale = NA_HEAD_DIM ** -0.5
    for p in range(NA_HEADS // 2):
        cs = slice(p * LANES, (p + 1) * LANES)
        bands = [pl.ds(_band_start_cblk(g) * CBLK_ROWS, KBAND) for g in range(N_QGRP)]
        s_grp = []
        for g in range(N_QGRP):
            qp = q_ref[pl.ds(g * QGRP, QGRP), cs].astype(F32) * scale
            qs = jnp.concatenate([jnp.where(low, qp, 0.0), jnp.where(low, 0.0, qp)], axis=0).astype(BF16)
            kk = jnp.concatenate([kp_ref[bands[g], cs], kc_ref[bands[g], cs], kn_ref[bands[g], cs]], axis=0)
            s_grp.append(lax.dot_general(qs, kk, (((1,), (1,)), ((), ())), preferred_element_type=F32))
        s = jnp.concatenate(s_grp, axis=0) + bias_ref[p, :, 0].reshape(N_QGRP * 2 * QGRP, NKEY)
        m = jnp.max(s, axis=-1, keepdims=True)
        e = jnp.exp(s - m)
        rl = 1.0 / jnp.sum(e, axis=-1, keepdims=True)
        e = e.astype(BF16)
        o_grp = []
        for g in range(N_QGRP):
            vv = jnp.concatenate([vp_ref[bands[g], cs], vc_ref[bands[g], cs], vn_ref[bands[g], cs]], axis=0)
            rows = slice(g * 2 * QGRP, (g + 1) * 2 * QGRP)
            o = jnp.dot(e[rows], vv, preferred_element_type=F32) * rl[rows]
            o_grp.append(jnp.where(low, o[0:QGRP], o[QGRP:]))
        o_pair = jnp.concatenate(
            [o_grp[cb // 2][(cb % 2) * CBLK_ROWS + r * SUBLANES:(cb % 2) * CBLK_ROWS + (r + 1) * SUBLANES]
             for r in range(TQ_ROWS) for cb in range(N_CBLK)], axis=0)
        r0 = POOL_W + p * LANES
        mix = mix + jnp.dot(o_pair.astype(BF16), wout_ref[r0:r0 + LANES, :], preferred_element_type=F32)

    x1 = _layer_norm(alpha * x_ref[...] + mix + bout_ref[...], lng_ref[...], lnb_ref[...])
    x1_ref[...] = x1

    lt = lax.dot_general(wr_ref[...], x1.astype(BF16), (((1,), (1,)), ((), ())),
                         preferred_element_type=F32) + br_ref[...]
    eidx = lax.broadcasted_iota(I32, (N_EXPERTS, TQ), 0)
    work = lt
    vals, idxs = [], []
    for _ in range(TOP_K):
        mk = jnp.max(work, axis=0, keepdims=True)
        ik = jnp.min(jnp.where(work == mk, eidx, N_EXPERTS), axis=0, keepdims=True)
        vals.append(mk)
        idxs.append(ik)
        work = jnp.where(eidx == ik, -jnp.inf, work)
    ex = [jnp.exp(v - vals[0]) for v in vals]
    den = ex[0] + ex[1] + ex[2] + ex[3]
    topp_ref[...] = jnp.concatenate([e_ / den for e_ in ex], axis=0)
    topi_ref[...] = jnp.concatenate(idxs, axis=0)

    run = base_ref[...]
    ranks = []
    for k in range(TOP_K):
        hot = (eidx == idxs[k]).astype(F32)
        before = jnp.dot(hot.astype(BF16), tri_ref[...], preferred_element_type=F32)
        ranks.append(jnp.sum(hot * (run + before), axis=0, keepdims=True))
        run = run + jnp.sum(hot, axis=1, keepdims=True)
    rank_ref[...] = jnp.concatenate(ranks, axis=0).astype(I32)
    base_ref[...] = run
    counts_ref[...] = jnp.broadcast_to(run, counts_ref.shape).astype(I32)


def _mixer(x, qkv, rest, bias_tab, wpool_bd, pscale, dw, dwb, clng, clnb, wpw, bpw,
           wout, bout, lng, lnb, wr_t, br, tri, *, layer, seq, alpha):
    n = x.shape[0]
    nt = n // TQ
    tps = seq // TQ
    hb = TQ // HALO

    def prev_t(i):
        return jnp.where(i % tps == 0, i, i - 1)

    def next_t(i):
        return jnp.where(i % tps == tps - 1, i, i + 1)

    def variant(i):
        ib = i % tps
        return jnp.where(ib == 0, 0, jnp.where(ib == tps - 1, 2, 1))

    def const(shape):
        return pl.BlockSpec(shape, lambda i: tuple(0 for _ in shape))

    in_specs = [
        pl.BlockSpec((TQ, D_MODEL), lambda i: (i, 0)),
        pl.BlockSpec((TQ, NA_W), lambda i: (i, 0)),
        pl.BlockSpec((TQ, NA_W), lambda i: (prev_t(i), 1)),
        pl.BlockSpec((TQ, NA_W), lambda i: (i, 1)),
        pl.BlockSpec((TQ, NA_W), lambda i: (next_t(i), 1)),
        pl.BlockSpec((TQ, NA_W), lambda i: (prev_t(i), 2)),
        pl.BlockSpec((TQ, NA_W), lambda i: (i, 2)),
        pl.BlockSpec((TQ, NA_W), lambda i: (next_t(i), 2)),
        pl.BlockSpec((TQ, POOL_W + 2 * CONV_W), lambda i: (i, 0)),
        pl.BlockSpec((HALO, POOL_W + 2 * CONV_W), lambda i: (jnp.maximum(i * hb - 1, 0), 0)),
        pl.BlockSpec((HALO, POOL_W + 2 * CONV_W), lambda i: (jnp.minimum((i + 1) * hb, nt * hb - 1), 0)),
        pl.BlockSpec((None, NA_HEADS // 2, N_QGRP, 1, 2 * QGRP, NKEY),
                     lambda i: (layer, 0, 0, variant(i), 0, 0)),
        const((POOL_W, POOL_W)), const((1, POOL_W)),
        const((CONV_K, CONV_W)), const((1, CONV_W)), const((1, CONV_W)), const((1, CONV_W)),
        const((CONV_W, CONV_W)), const((1, CONV_W)),
        const((D_MODEL, D_MODEL)), const((1, D_MODEL)), const((1, D_MODEL)), const((1, D_MODEL)),
        const((N_EXPERTS, D_MODEL)), const((N_EXPERTS, 1)), const((TQ, TQ)),
    ]
    out_specs = [
        pl.BlockSpec((TQ, D_MODEL), lambda i: (i, 0)),
        pl.BlockSpec((TOP_K, TQ), lambda i: (0, i)),
        pl.BlockSpec((TOP_K, TQ), lambda i: (0, i)),
        pl.BlockSpec((TOP_K, TQ), lambda i: (0, i)),
        pl.BlockSpec((N_EXPERTS, LANES), lambda i: (0, 0)),
    ]
    out_shape = [
        jax.ShapeDtypeStruct((n, D_MODEL), F32),
        jax.ShapeDtypeStruct((TOP_K, n), I32),
        jax.ShapeDtypeStruct((TOP_K, n), F32),
        jax.ShapeDtypeStruct((TOP_K, n), I32),
        jax.ShapeDtypeStruct((N_EXPERTS, LANES), I32),
    ]
    return pl.pallas_call(
        functools.partial(_mixer_kernel, tps, alpha),
        grid=(nt,),
        in_specs=in_specs,
        out_specs=out_specs,
        out_shape=out_shape,
        scratch_shapes=[
            pltpu.VMEM((TQ + 2 * HALO, POOL_W + 2 * CONV_W), F32),
            pltpu.VMEM((SUBLANES - 1, TQ + 2 * HALO - SUBLANES, CONV_W), F32),
            pltpu.VMEM((N_EXPERTS, 1), F32),
        ],
        compiler_params=pltpu.CompilerParams(
            dimension_semantics=("arbitrary",), vmem_limit_bytes=VMEM_LIMIT),
        name="mixer",
    )(x, qkv, qkv, qkv, qkv, qkv, qkv, qkv, rest, rest, rest, bias_tab,
      wpool_bd, pscale, dw, dwb, clng, clnb, wpw, bpw, wout, bout, lng, lnb, wr_t, br, tri)


def _dispatch_kernel(pos_ref, x_ref, xs_init_hbm, xs_hbm, sem):
    del xs_init_hbm
    for k in range(TOP_K):
        for r in range(TQ):
            pltpu.make_async_copy(
                x_ref.at[pl.ds(r, 1), :], xs_hbm.at[pl.ds(pos_ref[k, r], 1), :], sem).start(priority=r % 2)
    for k in range(TOP_K):
        pltpu.make_async_copy(x_ref, xs_hbm.at[pl.ds(0, TQ), :], sem).wait()


def _dispatch(x1, pos, xs_init):
    n = x1.shape[0]
    return pl.pallas_call(
        _dispatch_kernel,
        grid=(n // TQ,),
        in_specs=[
            pl.BlockSpec((TOP_K, TQ), lambda i: (0, i), memory_space=pltpu.SMEM),
            pl.BlockSpec((TQ, D_MODEL), lambda i: (i, 0)),
            pl.BlockSpec(memory_space=pl.ANY),
        ],
        out_specs=pl.BlockSpec(memory_space=pl.ANY),
        out_shape=jax.ShapeDtypeStruct(xs_init.shape, xs_init.dtype),
        scratch_shapes=[pltpu.SemaphoreType.DMA(())],
        input_output_aliases={2: 0},
        compiler_params=pltpu.CompilerParams(dimension_semantics=("arbitrary",)),
        name="dispatch",
    )(pos, x1, xs_init)


def _experts_kernel(layer, te_ref, na_ref, first_ref, slot_ref, nxt_ref,
                    xs_ref, bg_ref, bu_ref, bd_ref, wg_hbm, wu_hbm, wd_hbm,
                    y_ref, wbuf, wg_bf, wu_bf, wd_bf, sem):
    t = pl.program_id(0)
    active = t < na_ref[0]
    e = te_ref[t]
    s = slot_ref[t]

    def fetch(expert, slot):
        return [pltpu.make_async_copy(w.at[layer, expert], wbuf.at[j, slot], sem.at[j, slot])
                for j, w in enumerate((wg_hbm, wu_hbm, wd_hbm))]

    @pl.when(t == 0)
    def _():
        for c in fetch(e, s):
            c.start()

    @pl.when(jnp.logical_and(active, first_ref[t] == 1))
    def _():
        for c in fetch(e, s):
            c.wait()

        @pl.when(nxt_ref[t] != e)
        def _():
            for c in fetch(nxt_ref[t], 1 - s):
                c.start()

        wg_bf[...] = wbuf[0, s].astype(BF16)
        wu_bf[...] = wbuf[1, s].astype(BF16)
        wd_bf[...] = wbuf[2, s].astype(BF16)

    @pl.when(active)
    def _():
        x = xs_ref[...].astype(BF16)
        g = jnp.minimum(jnp.dot(x, wg_bf[...], preferred_element_type=F32) + bg_ref[...], SWIGLU_LIMIT)
        u = jnp.clip(jnp.dot(x, wu_bf[...], preferred_element_type=F32) + bu_ref[...],
                     -SWIGLU_LIMIT, SWIGLU_LIMIT)
        act = (u + 1.0) * g * jax.nn.sigmoid(SWIGLU_ALPHA * g)
        y_ref[...] = jnp.dot(act.astype(BF16), wd_bf[...], preferred_element_type=F32) + bd_ref[...]

    @pl.when(jnp.logical_not(active))
    def _():
        y_ref[...] = jnp.zeros_like(y_ref)


def _experts(layer, tile_expert, n_active, tile_first, tile_slot, tile_next, xs, wg, bg, wu, bu, wd, bd):
    m_pad = xs.shape[0]
    n_tiles = m_pad // TM_E

    def xmap(t, te, na, *_):
        return (jnp.minimum(t, na[0] - 1), 0)

    def bmap(t, te, *_):
        return (layer, te[t], 0, 0)

    grid_spec = pltpu.PrefetchScalarGridSpec(
        num_scalar_prefetch=5,
        grid=(n_tiles,),
        in_specs=[
            pl.BlockSpec((TM_E, D_MODEL), xmap),
            pl.BlockSpec((None, None, 1, D_MODEL), bmap),
            pl.BlockSpec((None, None, 1, D_MODEL), bmap),
            pl.BlockSpec((None, None, 1, D_MODEL), bmap),
            pl.BlockSpec(memory_space=pl.ANY),
            pl.BlockSpec(memory_space=pl.ANY),
            pl.BlockSpec(memory_space=pl.ANY),
        ],
        out_specs=pl.BlockSpec((TM_E, D_MODEL), lambda t, *_: (t, 0)),
        scratch_shapes=[
            pltpu.VMEM((3, 2, D_MODEL, D_MODEL), F32),
            pltpu.VMEM((D_MODEL, D_MODEL), BF16),
            pltpu.VMEM((D_MODEL, D_MODEL), BF16),
            pltpu.VMEM((D_MODEL, D_MODEL), BF16),
            pltpu.SemaphoreType.DMA((3, 2)),
        ],
    )
    return pl.pallas_call(
        functools.partial(_experts_kernel, layer),
        grid_spec=grid_spec,
        out_shape=jax.ShapeDtypeStruct((m_pad, D_MODEL), F32),
        compiler_params=pltpu.CompilerParams(
            dimension_semantics=("arbitrary",), vmem_limit_bytes=VMEM_LIMIT),
        name="experts",
    )(tile_expert, n_active, tile_first, tile_slot, tile_next, xs, bg, bu, bd, wg, wu, wd)


def _combine_kernel(alpha, pos_ref, x1_ref, gates_ref, lng_ref, lnb_ref, ys_hbm, out_ref, buf, sem):
    for k in range(TOP_K):
        for r in range(TQ):
            pltpu.make_async_copy(
                ys_hbm.at[pl.ds(pos_ref[k, r], 1), :], buf.at[k, pl.ds(r, 1), :], sem).start(priority=r % 2)
    for k in range(TOP_K):
        pltpu.make_async_copy(ys_hbm.at[pl.ds(0, TQ), :], buf.at[k], sem).wait()

    gates = gates_ref[...]
    ffn = buf[0] * gates[:, 0:1]
    for k in range(1, TOP_K):
        ffn = ffn + buf[k] * gates[:, k:k + 1]
    out_ref[...] = _layer_norm(alpha * x1_ref[...] + ffn, lng_ref[...], lnb_ref[...])


def _combine(pos, x1, gates_tm, lng, lnb, ys, *, alpha):
    n = x1.shape[0]
    return pl.pallas_call(
        functools.partial(_combine_kernel, alpha),
        grid=(n // TQ,),
        in_specs=[
            pl.BlockSpec((TOP_K, TQ), lambda i: (0, i), memory_space=pltpu.SMEM),
            pl.BlockSpec((TQ, D_MODEL), lambda i: (i, 0)),
            pl.BlockSpec((TQ, TOP_K), lambda i: (i, 0)),
            pl.BlockSpec((1, D_MODEL), lambda i: (0, 0)),
            pl.BlockSpec((1, D_MODEL), lambda i: (0, 0)),
            pl.BlockSpec(memory_space=pl.ANY),
        ],
        out_specs=pl.BlockSpec((TQ, D_MODEL), lambda i: (i, 0)),
        out_shape=jax.ShapeDtypeStruct((n, D_MODEL), F32),
        scratch_shapes=[pltpu.VMEM((TOP_K, TQ, D_MODEL), F32), pltpu.SemaphoreType.DMA(())],
        compiler_params=pltpu.CompilerParams(
            dimension_semantics=("arbitrary",), vmem_limit_bytes=VMEM_LIMIT),
        name="combine",
    )(pos, x1, gates_tm, lng, lnb, ys)


def kernel(x, w_in, b_in, w_pool, pool_scale, rpb, conv_dw, conv_dw_b, conv_ln_g, conv_ln_b,
           w_conv_pw, b_conv_pw, w_out, b_out, ln1_g, ln1_b, w_router, b_router,
           w_gate, b_gate, w_up, b_up, w_down, b_down, ln2_g, ln2_b):
    batch, seq, d = x.shape
    depth = w_in.shape[0]
    n = batch * seq
    rows = seq // GRID_W
    alpha = (2.0 * depth) ** 0.25
    off_q, off_k, off_v = POOL_W, POOL_W + NA_W, POOL_W + 2 * NA_W
    off_ca = off_v + NA_W
    m_pad = n * TOP_K + N_EXPERTS * TM_E
    n_tiles = m_pad // TM_E

    tri = (np.arange(TQ)[:, None] < np.arange(TQ)[None, :]).astype(np.float32)
    tri = jnp.asarray(tri, BF16)
    row2 = lambda v: v.reshape(1, -1)
    bias_tabs = _attn_bias_tables(rpb, rows)
    experts_iota = jnp.arange(N_EXPERTS, dtype=I32)
    tile_starts = jnp.arange(n_tiles, dtype=I32) * TM_E
    b_gate4 = b_gate.reshape(depth, N_EXPERTS, 1, d)
    b_up4 = b_up.reshape(depth, N_EXPERTS, 1, d)
    b_down4 = b_down.reshape(depth, N_EXPERTS, 1, d)

    h = x.reshape(n, d)
    for l in range(depth):
        wqkv = w_in[l][:, off_q:off_ca].astype(BF16)
        bqkv = row2(b_in[l][off_q:off_ca])
        wrest = jnp.concatenate([w_in[l][:, :off_q], w_in[l][:, off_ca:]], axis=1).astype(BF16)
        brest = row2(jnp.concatenate([b_in[l][:off_q], b_in[l][off_ca:]]))
        qkv, rest = _inproj(h, wqkv, bqkv, wrest, brest)

        wpool_bd = jax.scipy.linalg.block_diag(*[w_pool[l][g] for g in range(len(POOL_WINDOWS))]).astype(BF16)
        x1, top_i, top_p, rank, counts = _mixer(
            h, qkv, rest, bias_tabs, wpool_bd, row2(pool_scale[l]),
            conv_dw[l].reshape(CONV_K, CONV_W), row2(conv_dw_b[l]), row2(conv_ln_g[l]), row2(conv_ln_b[l]),
            w_conv_pw[l].astype(BF16), row2(b_conv_pw[l]),
            w_out[l].astype(BF16), row2(b_out[l]), row2(ln1_g[l]), row2(ln1_b[l]),
            w_router[l].T.astype(BF16), b_router[l].reshape(N_EXPERTS, 1), tri,
            layer=l, seq=seq, alpha=alpha)

        cnt = counts[:, 0]
        cpad = ((cnt + TM_E - 1) // TM_E) * TM_E
        ends = jnp.cumsum(cpad)
        off = ends - cpad
        hot = top_i[None] == experts_iota[:, None, None]
        pos = rank + jnp.sum(jnp.where(hot, off[:, None, None], 0), axis=0)
        n_active = (ends[-1] // TM_E).astype(I32)
        last_start = jnp.minimum(tile_starts, ends[-1] - TM_E)
        tile_expert = jnp.sum((ends[None, :] <= last_start[:, None]).astype(I32), axis=1)
        owns = cnt > 0
        e_slot = (jnp.cumsum(owns.astype(I32)) - 1) % 2
        later = jnp.logical_and(experts_iota[None, :] > experts_iota[:, None], owns[None, :])
        e_next = jnp.min(jnp.where(later, experts_iota[None, :], N_EXPERTS), axis=1)
        e_next = jnp.where(e_next == N_EXPERTS, experts_iota, e_next)
        tile_hot = tile_expert[:, None] == experts_iota[None, :]
        pick = lambda v: jnp.sum(jnp.where(tile_hot, v[None, :], 0), axis=1).astype(I32)
        tile_first = (pick(off) == tile_starts).astype(I32)
        tile_slot = pick(e_slot)
        tile_next = pick(e_next)

        xs = _dispatch(x1, pos, jnp.zeros((m_pad, d), F32) if l == 0 else xs)
        ys = _experts(l, tile_expert, n_active.reshape(1), tile_first, tile_slot, tile_next, xs,
                      w_gate, b_gate4, w_up, b_up4, w_down, b_down4)
        h = _combine(pos, x1, top_p.T, row2(ln2_g[l]), row2(ln2_b[l]), ys, alpha=alpha)
    return h.reshape(batch, seq, d)
```

```python
import functools

import jax
import jax.numpy as jnp
import numpy as np
from jax import lax
from jax.experimental import pallas as pl
from jax.experimental.pallas import tpu as pltpu

F32 = jnp.float32
BF16 = jnp.bfloat16
I32 = jnp.int32

D_MODEL = 1024
GRID_W = 64
POOL_W = 256
POOL_WINDOWS = (2, 4, 8, 16)
POOL_GROUP_W = 64
NA_W = 512
NA_HEADS = 8
NA_HEAD_DIM = 64
NA_ROWS = 8
NA_COLS = 16
CONV_W = 256
CONV_K = 31
N_EXPERTS = 32
TOP_K = 4
SWIGLU_ALPHA = 1.702
SWIGLU_LIMIT = 7.0
LN_EPS = 1e-5
NEG_INF = -1e30

LANES = 128
TM_IN = 512
TQ = 256
TQ_ROWS = TQ // GRID_W
HALO = 16
SUBLANES = 8
N_CBLK = GRID_W // SUBLANES
CBLK_ROWS = TQ_ROWS * SUBLANES
N_QGRP = GRID_W // NA_COLS
QGRP = TQ // N_QGRP
KBAND_COLS = 2 * NA_COLS
KBAND = TQ_ROWS * KBAND_COLS
NKEY = 3 * KBAND
TM_E = 512
TM_CHUNK = 128
VMEM_LIMIT = 56 * 1024 * 1024


def _layer_norm(v, g, b):
    mu = jnp.mean(v, axis=-1, keepdims=True)
    c = v - mu
    var = jnp.mean(c * c, axis=-1, keepdims=True)
    return c * lax.rsqrt(var + LN_EPS) * g + b


def _band_start_cblk(g):
    return min(max(2 * g - 1, 0), N_CBLK - KBAND_COLS // SUBLANES)


def _stored_row_groups():
    return [r * N_CBLK + cb for cb in range(N_CBLK) for r in range(TQ_ROWS)]


def _inproj_kernel(x_ref, wqkv_ref, bqkv_ref, wrest_ref, brest_ref, qkv_ref, rest_ref):
    x = x_ref[...].astype(BF16)
    rest_ref[...] = jnp.dot(x, wrest_ref[...], preferred_element_type=F32) + brest_ref[...]
    groups = [x_ref[pl.ds(tile * TQ + gn * SUBLANES, SUBLANES), :]
              for tile in range(TM_IN // TQ) for gn in _stored_row_groups()]
    xp = jnp.concatenate(groups, axis=0).astype(BF16)
    qkv = jnp.dot(xp, wqkv_ref[...], preferred_element_type=F32) + bqkv_ref[...]
    qkv_ref[...] = qkv.astype(BF16)


def _inproj(x, wqkv, bqkv, wrest, brest):
    n = x.shape[0]
    return pl.pallas_call(
        _inproj_kernel,
        grid=(n // TM_IN,),
        in_specs=[
            pl.BlockSpec((TM_IN, D_MODEL), lambda i: (i, 0)),
            pl.BlockSpec((D_MODEL, 3 * NA_W), lambda i: (0, 0)),
            pl.BlockSpec((1, 3 * NA_W), lambda i: (0, 0)),
            pl.BlockSpec((D_MODEL, POOL_W + 2 * CONV_W), lambda i: (0, 0)),
            pl.BlockSpec((1, POOL_W + 2 * CONV_W), lambda i: (0, 0)),
        ],
        out_specs=[
            pl.BlockSpec((TM_IN, 3 * NA_W), lambda i: (i, 0)),
            pl.BlockSpec((TM_IN, POOL_W + 2 * CONV_W), lambda i: (i, 0)),
        ],
        out_shape=[
            jax.ShapeDtypeStruct((n, 3 * NA_W), BF16),
            jax.ShapeDtypeStruct((n, POOL_W + 2 * CONV_W), F32),
        ],
        compiler_params=pltpu.CompilerParams(
            dimension_semantics=("arbitrary",), vmem_limit_bytes=VMEM_LIMIT),
        name="inproj",
    )(x, wqkv, bqkv, wrest, brest)


def _attn_bias_tables(rpb, rows):
    kr_win = min(NA_ROWS, rows)
    n_tiles = rows // TQ_ROWS
    n_r, n_c = 2 * NA_ROWS - 1, 2 * NA_COLS - 1
    n_layers = rpb.shape[0]
    c_hots, c_oks = [], []
    for g in range(N_QGRP):
        qc = g * NA_COLS + np.arange(NA_COLS)
        kc = _band_start_cblk(g) * SUBLANES + np.arange(KBAND_COLS)
        sc = np.clip(qc - NA_COLS // 2, 0, GRID_W - NA_COLS)[:, None]
        c_oks.append((kc[None] >= sc) & (kc[None] < sc + NA_COLS))
        c_off = np.clip(kc[None] - qc[:, None] + NA_COLS - 1, 0, n_c - 1)
        c_hots.append((c_off[..., None] == np.arange(n_c)).astype(np.float32))
    r_hots, r_oks = [], []
    for tile in (0, 1, n_tiles - 1):
        qr = tile * TQ_ROWS + np.arange(TQ_ROWS)
        slot_tile = np.array([tile - 1, tile, tile + 1])
        kr = slot_tile[:, None] * TQ_ROWS + np.arange(TQ_ROWS)[None, :]
        slot_ok = ((slot_tile >= 0) & (slot_tile < n_tiles))[:, None]
        sr = np.clip(qr - kr_win // 2, 0, rows - kr_win)[:, None, None]
        r_oks.append((kr[None] >= sr) & (kr[None] < sr + kr_win) & slot_ok[None])
        r_off = np.clip(kr[None] - qr[:, None, None] + NA_ROWS - 1, 0, n_r - 1)
        r_hots.append((r_off[..., None] == np.arange(n_r)).astype(np.float32))
    c_hot = jnp.asarray(np.stack(c_hots))
    r_off = np.stack([np.argmax(h, axis=-1) for h in r_hots])
    c_ok = np.stack(c_oks).reshape(N_QGRP, 2, SUBLANES, KBAND_COLS // SUBLANES, SUBLANES)
    r_ok = np.stack(r_oks)
    ok = (r_ok[:, None, None, :, None, :, None, :, None]
          & c_ok[None, :, :, None, :, None, :, None, :])
    ok = np.broadcast_to(ok[:, :, None], (3, N_QGRP, 2) + ok.shape[2:])
    ok = jnp.asarray(ok.reshape(3, N_QGRP, 2 * QGRP, NKEY).astype(np.float32))
    rpb5 = rpb.astype(F32).reshape(n_layers, NA_HEADS // 2, 2, n_r, n_c)
    by_col = jnp.einsum("lpjrc,gmnc->lpgjmrn", rpb5, c_hot, precision=lax.Precision.HIGHEST)
    by_col = by_col.reshape(n_layers * (NA_HEADS // 2) * N_QGRP, 2 * NA_COLS, n_r * KBAND_COLS)
    nb = KBAND_COLS // SUBLANES
    expand = np.zeros((TQ_ROWS, 2, 2, TQ_ROWS, SUBLANES, 2, 2, SUBLANES), np.float32)
    for q in range(TQ_ROWS):
        for j in range(2):
            for a in range(2):
                for x in range(SUBLANES):
                    expand[q, j, a, q, x, j, a, x] = 1.0
    expand = expand.reshape(TQ_ROWS, 2 * QGRP, 2 * NA_COLS)
    select = np.zeros((3, TQ_ROWS, n_r, nb, SUBLANES, 3, nb, TQ_ROWS, SUBLANES), np.float32)
    for v in range(3):
        for q in range(TQ_ROWS):
            for s in range(3):
                for y in range(TQ_ROWS):
                    for b in range(nb):
                        for z in range(SUBLANES):
                            select[v, q, r_off[v, q, s, y], b, z, s, b, y, z] = 1.0
    select = select.reshape(3, TQ_ROWS, n_r * KBAND_COLS, NKEY)
    lane_pad = -(n_r * KBAND_COLS) % LANES
    by_col = jnp.pad(by_col, ((0, 0), (0, 0), (0, lane_pad)))
    select = np.pad(select, ((0, 0), (0, 0), (0, lane_pad), (0, 0)))
    tab = _bias_table_call(by_col, jnp.asarray(expand, BF16), jnp.asarray(select, BF16), ok)
    return tab.reshape(n_layers, NA_HEADS // 2, N_QGRP, 3, 2 * QGRP, NKEY)


def _bias_table_kernel(u_ref, expand_ref, select_ref, ok_ref, out_ref):
    u = u_ref[...]
    parts = []
    for _ in range(3):
        piece = u.astype(BF16)
        parts.append(piece)
        u = u - piece.astype(F32)
    n_k = u.shape[1]
    pieces = jnp.concatenate(parts, axis=1)
    acc = [jnp.zeros((2 * QGRP, NKEY), F32) for _ in range(3)]
    for q in range(TQ_ROWS):
        rows = jnp.dot(expand_ref[q], pieces, preferred_element_type=F32).astype(BF16)
        rows = jnp.concatenate([rows[:, k * n_k:(k + 1) * n_k] for k in range(3)], axis=0)
        for v in range(3):
            sel = jnp.dot(rows, select_ref[v, q], preferred_element_type=F32)
            acc[v] = acc[v] + (sel[0:2 * QGRP] + sel[2 * QGRP:4 * QGRP] + sel[4 * QGRP:])
    for v in range(3):
        out_ref[v] = jnp.where(ok_ref[v, 0] > 0.0, acc[v], NEG_INF)


def _bias_table_call(by_col, expand, select, ok):
    n, n_u, n_k = by_col.shape
    return pl.pallas_call(
        _bias_table_kernel,
        grid=(n,),
        in_specs=[
            pl.BlockSpec((None, n_u, n_k), lambda i: (i, 0, 0)),
            pl.BlockSpec(expand.shape, lambda i: (0, 0, 0)),
            pl.BlockSpec(select.shape, lambda i: (0, 0, 0, 0)),
            pl.BlockSpec((3, 1, 2 * QGRP, NKEY), lambda i: (0, i % N_QGRP, 0, 0)),
        ],
        out_specs=pl.BlockSpec((None, 3, 2 * QGRP, NKEY), lambda i: (i, 0, 0, 0)),
        out_shape=jax.ShapeDtypeStruct((n, 3, 2 * QGRP, NKEY), F32),
        compiler_params=pltpu.CompilerParams(
            dimension_semantics=("arbitrary",), vmem_limit_bytes=VMEM_LIMIT),
        name="bias_table",
    )(by_col, expand, select, ok)


def _mixer_kernel(tiles_per_seq, alpha,
                  x_ref, q_ref, kp_ref, kc_ref, kn_ref, vp_ref, vc_ref, vn_ref,
                  rc_ref, rp_ref, rn_ref, bias_ref,
                  wpool_ref, pscale_ref, dw_ref, dwb_ref, clng_ref, clnb_ref, wpw_ref, bpw_ref,
                  wout_ref, bout_ref, lng_ref, lnb_ref, wr_ref, br_ref, tri_ref,
                  x1_ref, topi_ref, topp_ref, rank_ref, counts_ref,
                  halo_ref, shift_ref, base_ref):
    i = pl.program_id(0)
    ib = i % tiles_per_seq
    has_prev = ib > 0
    has_next = ib < tiles_per_seq - 1

    @pl.when(i == 0)
    def _():
        base_ref[...] = jnp.zeros_like(base_ref)

    halo_ref[0:HALO, :] = jnp.where(has_prev, rp_ref[...], 0.0)
    halo_ref[HALO:HALO + TQ, :] = rc_ref[...]
    halo_ref[HALO + TQ:, :] = jnp.where(has_next, rn_ref[...], 0.0)

    def fill_shifts():
        for b in range(1, SUBLANES):
            shift_ref[b - 1] = halo_ref[pl.ds(b, TQ + 2 * HALO - SUBLANES), 0:CONV_W]

    def window(start):
        a8, b = divmod(start, SUBLANES)
        if b == 0:
            return halo_ref[pl.ds(a8 * SUBLANES, TQ), 0:CONV_W]
        return shift_ref[b - 1, pl.ds(a8 * SUBLANES, TQ), :]

    fill_shifts()

    def u_at(off):
        return window(HALO + off)

    t_seq = ib * TQ + lax.broadcasted_iota(I32, (TQ, 1), 0)
    seq_len = tiles_per_seq * TQ
    group = lax.broadcasted_iota(I32, (1, POOL_W), 1) // POOL_GROUP_W
    u0 = u_at(0)
    acc = u0
    mean = jnp.zeros((TQ, POOL_W), F32)
    done = 0
    for g, w in enumerate(POOL_WINDOWS):
        half = w // 2
        for o in range(done + 1, half + 1):
            acc = acc + u_at(-o) + (u_at(o - 1) if o > 1 else 0.0)
        done = half
        cnt = (jnp.minimum(t_seq + half, seq_len) - jnp.maximum(t_seq - half, 0)).astype(F32)
        mean = jnp.where(group == g, acc / cnt, mean)
    d = (mean - u0).astype(BF16)
    y_pool = jnp.dot(d, wpool_ref[...], preferred_element_type=F32) * pscale_ref[...]
    mix = jnp.dot(y_pool.astype(BF16), wout_ref[0:POOL_W, :], preferred_element_type=F32)

    a = halo_ref[:, POOL_W:POOL_W + CONV_W]
    gate = halo_ref[:, POOL_W + CONV_W:]
    halo_ref[:, 0:CONV_W] = a * jax.nn.sigmoid(gate)
    fill_shifts()
    conv = jnp.zeros((TQ, CONV_W), F32) + dwb_ref[...]
    for k in range(CONV_K):
        conv = conv + window(HALO - CONV_K // 2 + k) * dw_ref[k:k + 1, :]
    hc = _layer_norm(conv, clng_ref[...], clnb_ref[...])
    hc = hc * jax.nn.sigmoid(hc)
    y_conv = jnp.dot(hc.astype(BF16), wpw_ref[...], preferred_element_type=F32) + bpw_ref[...]
    mix = mix + jnp.dot(y_conv.astype(BF16), wout_ref[POOL_W + NA_W:, :], preferred_element_type=F32)

    low = lax.broadcasted_iota(I32, (QGRP, LANES), 1) < NA_HEAD_DIM
    scale = NA_HEAD_DIM ** -0.5
    for p in range(NA_HEADS // 2):
        cs = slice(p * LANES, (p + 1) * LANES)
        bands = [pl.ds(_band_start_cblk(g) * CBLK_ROWS, KBAND) for g in range(N_QGRP)]
        s_grp = []
        for g in range(N_QGRP):
            qp = q_ref[pl.ds(g * QGRP, QGRP), cs].astype(F32) * scale
            qs = jnp.concatenate([jnp.where(low, qp, 0.0), jnp.where(low, 0.0, qp)], axis=0).astype(BF16)
            kk = jnp.concatenate([kp_ref[bands[g], cs], kc_ref[bands[g], cs], kn_ref[bands[g], cs]], axis=0)
            s_grp.append(lax.dot_general(qs, kk, (((1,), (1,)), ((), ())), preferred_element_type=F32))
        s = jnp.concatenate(s_grp, axis=0) + bias_ref[p, :, 0].reshape(N_QGRP * 2 * QGRP, NKEY)
        m = jnp.max(s, axis=-1, keepdims=True)
        e = jnp.exp(s - m)
        rl = 1.0 / jnp.sum(e, axis=-1, keepdims=True)
        e = e.astype(BF16)
        o_grp = []
        for g in range(N_QGRP):
            vv = jnp.concatenate([vp_ref[bands[g], cs], vc_ref[bands[g], cs], vn_ref[bands[g], cs]], axis=0)
            rows = slice(g * 2 * QGRP, (g + 1) * 2 * QGRP)
            o = jnp.dot(e[rows], vv, preferred_element_type=F32) * rl[rows]
            o_grp.append(jnp.where(low, o[0:QGRP], o[QGRP:]))
        o_pair = jnp.concatenate(
            [o_grp[cb // 2][(cb % 2) * CBLK_ROWS + r * SUBLANES:(cb % 2) * CBLK_ROWS + (r + 1) * SUBLANES]
             for r in range(TQ_ROWS) for cb in range(N_CBLK)], axis=0)
        r0 = POOL_W + p * LANES
        mix = mix + jnp.dot(o_pair.astype(BF16), wout_ref[r0:r0 + LANES, :], preferred_element_type=F32)

    x1 = _layer_norm(alpha * x_ref[...] + mix + bout_ref[...], lng_ref[...], lnb_ref[...])
    x1_ref[...] = x1

    lt = lax.dot_general(wr_ref[...], x1.astype(BF16), (((1,), (1,)), ((), ())),
                         preferred_element_type=F32) + br_ref[...]
    eidx = lax.broadcasted_iota(I32, (N_EXPERTS, TQ), 0)
    work = lt
    vals, idxs = [], []
    for _ in range(TOP_K):
        mk = jnp.max(work, axis=0, keepdims=True)
        ik = jnp.min(jnp.where(work == mk, eidx, N_EXPERTS), axis=0, keepdims=True)
        vals.append(mk)
        idxs.append(ik)
        work = jnp.where(eidx == ik, -jnp.inf, work)
    ex = [jnp.exp(v - vals[0]) for v in vals]
    den = ex[0] + ex[1] + ex[2] + ex[3]
    topp_ref[...] = jnp.concatenate([e_ / den for e_ in ex], axis=0)
    topi_ref[...] = jnp.concatenate(idxs, axis=0)

    run = base_ref[...]
    ranks = []
    for k in range(TOP_K):
        hot = (eidx == idxs[k]).astype(F32)
        before = jnp.dot(hot.astype(BF16), tri_ref[...], preferred_element_type=F32)
        ranks.append(jnp.sum(hot * (run + before), axis=0, keepdims=True))
        run = run + jnp.sum(hot, axis=1, keepdims=True)
    rank_ref[...] = jnp.concatenate(ranks, axis=0).astype(I32)
    base_ref[...] = run
    counts_ref[...] = jnp.broadcast_to(run, counts_ref.shape).astype(I32)


def _mixer(x, qkv, rest, bias_tab, wpool_bd, pscale, dw, dwb, clng, clnb, wpw, bpw,
           wout, bout, lng, lnb, wr_t, br, tri, *, layer, seq, alpha):
    n = x.shape[0]
    nt = n // TQ
    tps = seq // TQ
    hb = TQ // HALO

    def prev_t(i):
        return jnp.where(i % tps == 0, i, i - 1)

    def next_t(i):
        return jnp.where(i % tps == tps - 1, i, i + 1)

    def variant(i):
        ib = i % tps
        return jnp.where(ib == 0, 0, jnp.where(ib == tps - 1, 2, 1))

    def const(shape):
        return pl.BlockSpec(shape, lambda i: tuple(0 for _ in shape))

    in_specs = [
        pl.BlockSpec((TQ, D_MODEL), lambda i: (i, 0)),
        pl.BlockSpec((TQ, NA_W), lambda i: (i, 0)),
        pl.BlockSpec((TQ, NA_W), lambda i: (prev_t(i), 1)),
        pl.BlockSpec((TQ, NA_W), lambda i: (i, 1)),
        pl.BlockSpec((TQ, NA_W), lambda i: (next_t(i), 1)),
        pl.BlockSpec((TQ, NA_W), lambda i: (prev_t(i), 2)),
        pl.BlockSpec((TQ, NA_W), lambda i: (i, 2)),
        pl.BlockSpec((TQ, NA_W), lambda i: (next_t(i), 2)),
        pl.BlockSpec((TQ, POOL_W + 2 * CONV_W), lambda i: (i, 0)),
        pl.BlockSpec((HALO, POOL_W + 2 * CONV_W), lambda i: (jnp.maximum(i * hb - 1, 0), 0)),
        pl.BlockSpec((HALO, POOL_W + 2 * CONV_W), lambda i: (jnp.minimum((i + 1) * hb, nt * hb - 1), 0)),
        pl.BlockSpec((None, NA_HEADS // 2, N_QGRP, 1, 2 * QGRP, NKEY),
                     lambda i: (layer, 0, 0, variant(i), 0, 0)),
        const((POOL_W, POOL_W)), const((1, POOL_W)),
        const((CONV_K, CONV_W)), const((1, CONV_W)), const((1, CONV_W)), const((1, CONV_W)),
        const((CONV_W, CONV_W)), const((1, CONV_W)),
        const((D_MODEL, D_MODEL)), const((1, D_MODEL)), const((1, D_MODEL)), const((1, D_MODEL)),
        const((N_EXPERTS, D_MODEL)), const((N_EXPERTS, 1)), const((TQ, TQ)),
    ]
    out_specs = [
        pl.BlockSpec((TQ, D_MODEL), lambda i: (i, 0)),
        pl.BlockSpec((TOP_K, TQ), lambda i: (0, i)),
        pl.BlockSpec((TOP_K, TQ), lambda i: (0, i)),
        pl.BlockSpec((TOP_K, TQ), lambda i: (0, i)),
        pl.BlockSpec((N_EXPERTS, LANES), lambda i: (0, 0)),
    ]
    out_shape = [
        jax.ShapeDtypeStruct((n, D_MODEL), F32),
        jax.ShapeDtypeStruct((TOP_K, n), I32),
        jax.ShapeDtypeStruct((TOP_K, n), F32),
        jax.ShapeDtypeStruct((TOP_K, n), I32),
        jax.ShapeDtypeStruct((N_EXPERTS, LANES), I32),
    ]
    return pl.pallas_call(
        functools.partial(_mixer_kernel, tps, alpha),
        grid=(nt,),
        in_specs=in_specs,
        out_specs=out_specs,
        out_shape=out_shape,
        scratch_shapes=[
            pltpu.VMEM((TQ + 2 * HALO, POOL_W + 2 * CONV_W), F32),
            pltpu.VMEM((SUBLANES - 1, TQ + 2 * HALO - SUBLANES, CONV_W), F32),
            pltpu.VMEM((N_EXPERTS, 1), F32),
        ],
        compiler_params=pltpu.CompilerParams(
            dimension_semantics=("arbitrary",), vmem_limit_bytes=VMEM_LIMIT),
        name="mixer",
    )(x, qkv, qkv, qkv, qkv, qkv, qkv, qkv, rest, rest, rest, bias_tab,
      wpool_bd, pscale, dw, dwb, clng, clnb, wpw, bpw, wout, bout, lng, lnb, wr_t, br, tri)


def _dispatch_kernel(pos_ref, x_ref, xs_init_hbm, xs_hbm, sem):
    del xs_init_hbm
    for k in range(TOP_K):
        for r in range(TQ):
            pltpu.make_async_copy(
                x_ref.at[pl.ds(r, 1), :], xs_hbm.at[pl.ds(pos_ref[k, r], 1), :], sem).start(priority=r % 2)
    for k in range(TOP_K):
        pltpu.make_async_copy(x_ref, xs_hbm.at[pl.ds(0, TQ), :], sem).wait()


def _dispatch(x1, pos, xs_init):
    n = x1.shape[0]
    return pl.pallas_call(
        _dispatch_kernel,
        grid=(n // TQ,),
        in_specs=[
            pl.BlockSpec((TOP_K, TQ), lambda i: (0, i), memory_space=pltpu.SMEM),
            pl.BlockSpec((TQ, D_MODEL), lambda i: (i, 0)),
            pl.BlockSpec(memory_space=pl.ANY),
        ],
        out_specs=pl.BlockSpec(memory_space=pl.ANY),
        out_shape=jax.ShapeDtypeStruct(xs_init.shape, xs_init.dtype),
        scratch_shapes=[pltpu.SemaphoreType.DMA(())],
        input_output_aliases={2: 0},
        compiler_params=pltpu.CompilerParams(dimension_semantics=("arbitrary",)),
        name="dispatch",
    )(pos, x1, xs_init)


def _experts_kernel(layer, te_ref, na_ref, first_ref, slot_ref, nxt_ref, rows_ref,
                    xs_ref, bg_ref, bu_ref, bd_ref, wg_hbm, wu_hbm, wd_hbm,
                    y_ref, wbuf, wg_bf, wu_bf, wd_bf, sem):
    t = pl.program_id(0)
    active = t < na_ref[0]
    e = te_ref[t]
    s = slot_ref[t]

    def fetch(expert, slot):
        return [pltpu.make_async_copy(w.at[layer, expert], wbuf.at[j, slot], sem.at[j, slot])
                for j, w in enumerate((wg_hbm, wu_hbm, wd_hbm))]

    @pl.when(t == 0)
    def _():
        for c in fetch(e, s):
            c.start()

    @pl.when(jnp.logical_and(active, first_ref[t] == 1))
    def _():
        for c in fetch(e, s):
            c.wait()

        @pl.when(nxt_ref[t] != e)
        def _():
            for c in fetch(nxt_ref[t], 1 - s):
                c.start()

        wg_bf[...] = wbuf[0, s].astype(BF16)
        wu_bf[...] = wbuf[1, s].astype(BF16)
        wd_bf[...] = wbuf[2, s].astype(BF16)

    for chunks in range(1, TM_E // TM_CHUNK + 1):
        m = chunks * TM_CHUNK

        @pl.when(jnp.logical_and(active, rows_ref[t] == chunks))
        def _(m=m):
            x = xs_ref[0:m, :].astype(BF16)
            g = jnp.minimum(jnp.dot(x, wg_bf[...], preferred_element_type=F32) + bg_ref[...], SWIGLU_LIMIT)
            u = jnp.clip(jnp.dot(x, wu_bf[...], preferred_element_type=F32) + bu_ref[...],
                         -SWIGLU_LIMIT, SWIGLU_LIMIT)
            act = (u + 1.0) * g * jax.nn.sigmoid(SWIGLU_ALPHA * g)
            y_ref[0:m, :] = jnp.dot(act.astype(BF16), wd_bf[...], preferred_element_type=F32) + bd_ref[...]
            if m < TM_E:
                y_ref[m:, :] = jnp.zeros((TM_E - m, D_MODEL), F32)

    @pl.when(jnp.logical_not(active))
    def _():
        y_ref[...] = jnp.zeros_like(y_ref)


def _experts(layer, tile_expert, n_active, tile_first, tile_slot, tile_next, tile_chunks,
             xs, wg, bg, wu, bu, wd, bd):
    m_pad = xs.shape[0]
    n_tiles = m_pad // TM_E

    def xmap(t, te, na, *_):
        return (jnp.minimum(t, na[0] - 1), 0)

    def bmap(t, te, *_):
        return (layer, te[t], 0, 0)

    grid_spec = pltpu.PrefetchScalarGridSpec(
        num_scalar_prefetch=6,
        grid=(n_tiles,),
        in_specs=[
            pl.BlockSpec((TM_E, D_MODEL), xmap),
            pl.BlockSpec((None, None, 1, D_MODEL), bmap),
            pl.BlockSpec((None, None, 1, D_MODEL), bmap),
            pl.BlockSpec((None, None, 1, D_MODEL), bmap),
            pl.BlockSpec(memory_space=pl.ANY),
            pl.BlockSpec(memory_space=pl.ANY),
            pl.BlockSpec(memory_space=pl.ANY),
        ],
        out_specs=pl.BlockSpec((TM_E, D_MODEL), lambda t, *_: (t, 0)),
        scratch_shapes=[
            pltpu.VMEM((3, 2, D_MODEL, D_MODEL), F32),
            pltpu.VMEM((D_MODEL, D_MODEL), BF16),
            pltpu.VMEM((D_MODEL, D_MODEL), BF16),
            pltpu.VMEM((D_MODEL, D_MODEL), BF16),
            pltpu.SemaphoreType.DMA((3, 2)),
        ],
    )
    return pl.pallas_call(
        functools.partial(_experts_kernel, layer),
        grid_spec=grid_spec,
        out_shape=jax.ShapeDtypeStruct((m_pad, D_MODEL), F32),
        compiler_params=pltpu.CompilerParams(
            dimension_semantics=("arbitrary",), vmem_limit_bytes=VMEM_LIMIT),
        name="experts",
    )(tile_expert, n_active, tile_first, tile_slot, tile_next, tile_chunks, xs, bg, bu, bd, wg, wu, wd)


def _combine_kernel(alpha, pos_ref, x1_ref, gates_ref, lng_ref, lnb_ref, ys_hbm, out_ref, buf, sem):
    for k in range(TOP_K):
        for r in range(TQ):
            pltpu.make_async_copy(
                ys_hbm.at[pl.ds(pos_ref[k, r], 1), :], buf.at[k, pl.ds(r, 1), :], sem).start(priority=r % 2)
    for k in range(TOP_K):
        pltpu.make_async_copy(ys_hbm.at[pl.ds(0, TQ), :], buf.at[k], sem).wait()

    gates = gates_ref[...]
    ffn = buf[0] * gates[:, 0:1]
    for k in range(1, TOP_K):
        ffn = ffn + buf[k] * gates[:, k:k + 1]
    out_ref[...] = _layer_norm(alpha * x1_ref[...] + ffn, lng_ref[...], lnb_ref[...])


def _combine(pos, x1, gates_tm, lng, lnb, ys, *, alpha):
    n = x1.shape[0]
    return pl.pallas_call(
        functools.partial(_combine_kernel, alpha),
        grid=(n // TQ,),
        in_specs=[
            pl.BlockSpec((TOP_K, TQ), lambda i: (0, i), memory_space=pltpu.SMEM),
            pl.BlockSpec((TQ, D_MODEL), lambda i: (i, 0)),
            pl.BlockSpec((TQ, TOP_K), lambda i: (i, 0)),
            pl.BlockSpec((1, D_MODEL), lambda i: (0, 0)),
            pl.BlockSpec((1, D_MODEL), lambda i: (0, 0)),
            pl.BlockSpec(memory_space=pl.ANY),
        ],
        out_specs=pl.BlockSpec((TQ, D_MODEL), lambda i: (i, 0)),
        out_shape=jax.ShapeDtypeStruct((n, D_MODEL), F32),
        scratch_shapes=[pltpu.VMEM((TOP_K, TQ, D_MODEL), F32), pltpu.SemaphoreType.DMA(())],
        compiler_params=pltpu.CompilerParams(
            dimension_semantics=("arbitrary",), vmem_limit_bytes=VMEM_LIMIT),
        name="combine",
    )(pos, x1, gates_tm, lng, lnb, ys)


def kernel(x, w_in, b_in, w_pool, pool_scale, rpb, conv_dw, conv_dw_b, conv_ln_g, conv_ln_b,
           w_conv_pw, b_conv_pw, w_out, b_out, ln1_g, ln1_b, w_router, b_router,
           w_gate, b_gate, w_up, b_up, w_down, b_down, ln2_g, ln2_b):
    batch, seq, d = x.shape
    depth = w_in.shape[0]
    n = batch * seq
    rows = seq // GRID_W
    alpha = (2.0 * depth) ** 0.25
    off_q, off_k, off_v = POOL_W, POOL_W + NA_W, POOL_W + 2 * NA_W
    off_ca = off_v + NA_W
    m_pad = n * TOP_K + N_EXPERTS * TM_E
    n_tiles = m_pad // TM_E

    tri = (np.arange(TQ)[:, None] < np.arange(TQ)[None, :]).astype(np.float32)
    tri = jnp.asarray(tri, BF16)
    row2 = lambda v: v.reshape(1, -1)
    bias_tabs = _attn_bias_tables(rpb, rows)
    experts_iota = jnp.arange(N_EXPERTS, dtype=I32)
    tile_starts = jnp.arange(n_tiles, dtype=I32) * TM_E
    b_gate4 = b_gate.reshape(depth, N_EXPERTS, 1, d)
    b_up4 = b_up.reshape(depth, N_EXPERTS, 1, d)
    b_down4 = b_down.reshape(depth, N_EXPERTS, 1, d)

    h = x.reshape(n, d)
    for l in range(depth):
        wqkv = w_in[l][:, off_q:off_ca].astype(BF16)
        bqkv = row2(b_in[l][off_q:off_ca])
        wrest = jnp.concatenate([w_in[l][:, :off_q], w_in[l][:, off_ca:]], axis=1).astype(BF16)
        brest = row2(jnp.concatenate([b_in[l][:off_q], b_in[l][off_ca:]]))
        qkv, rest = _inproj(h, wqkv, bqkv, wrest, brest)

        wpool_bd = jax.scipy.linalg.block_diag(*[w_pool[l][g] for g in range(len(POOL_WINDOWS))]).astype(BF16)
        x1, top_i, top_p, rank, counts = _mixer(
            h, qkv, rest, bias_tabs, wpool_bd, row2(pool_scale[l]),
            conv_dw[l].reshape(CONV_K, CONV_W), row2(conv_dw_b[l]), row2(conv_ln_g[l]), row2(conv_ln_b[l]),
            w_conv_pw[l].astype(BF16), row2(b_conv_pw[l]),
            w_out[l].astype(BF16), row2(b_out[l]), row2(ln1_g[l]), row2(ln1_b[l]),
            w_router[l].T.astype(BF16), b_router[l].reshape(N_EXPERTS, 1), tri,
            layer=l, seq=seq, alpha=alpha)

        cnt = counts[:, 0]
        cpad = ((cnt + TM_E - 1) // TM_E) * TM_E
        ends = jnp.cumsum(cpad)
        off = ends - cpad
        hot = top_i[None] == experts_iota[:, None, None]
        pos = rank + jnp.sum(jnp.where(hot, off[:, None, None], 0), axis=0)
        n_active = (ends[-1] // TM_E).astype(I32)
        last_start = jnp.minimum(tile_starts, ends[-1] - TM_E)
        tile_expert = jnp.sum((ends[None, :] <= last_start[:, None]).astype(I32), axis=1)
        owns = cnt > 0
        e_slot = (jnp.cumsum(owns.astype(I32)) - 1) % 2
        later = jnp.logical_and(experts_iota[None, :] > experts_iota[:, None], owns[None, :])
        e_next = jnp.min(jnp.where(later, experts_iota[None, :], N_EXPERTS), axis=1)
        e_next = jnp.where(e_next == N_EXPERTS, experts_iota, e_next)
        tile_hot = tile_expert[:, None] == experts_iota[None, :]
        pick = lambda v: jnp.sum(jnp.where(tile_hot, v[None, :], 0), axis=1).astype(I32)
        tile_first = (pick(off) == tile_starts).astype(I32)
        tile_slot = pick(e_slot)
        tile_next = pick(e_next)

        xs = _dispatch(x1, pos, jnp.zeros((m_pad, d), F32) if l == 0 else xs)
        real_rows = jnp.clip(pick(off + cnt) - tile_starts, 0, TM_E)
        tile_chunks = (real_rows + TM_CHUNK - 1) // TM_CHUNK
        ys = _experts(l, tile_expert, n_active.reshape(1), tile_first, tile_slot, tile_next, tile_chunks, xs,
                      w_gate, b_gate4, w_up, b_up4, w_down, b_down4)
        h = _combine(pos, x1, top_p.T, row2(ln2_g[l]), row2(ln2_b[l]), ys, alpha=alpha)
    return h.reshape(batch, seq, d)
```

```python
import functools

import jax
import jax.numpy as jnp
import numpy as np
from jax import lax
from jax.experimental import pallas as pl
from jax.experimental.pallas import tpu as pltpu

F32 = jnp.float32
BF16 = jnp.bfloat16
I32 = jnp.int32

D_MODEL = 1024
GRID_W = 64
POOL_W = 256
POOL_WINDOWS = (2, 4, 8, 16)
POOL_GROUP_W = 64
NA_W = 512
NA_HEADS = 8
NA_HEAD_DIM = 64
NA_ROWS = 8
NA_COLS = 16
CONV_W = 256
CONV_K = 31
N_EXPERTS = 32
TOP_K = 4
SWIGLU_ALPHA = 1.702
SWIGLU_LIMIT = 7.0
LN_EPS = 1e-5
NEG_INF = -1e30

LANES = 128
TM_IN = 512
TQ = 256
TQ_ROWS = TQ // GRID_W
HALO = 16
SUBLANES = 8
N_CBLK = GRID_W // SUBLANES
CBLK_ROWS = TQ_ROWS * SUBLANES
N_QGRP = GRID_W // NA_COLS
QGRP = TQ // N_QGRP
KBAND_COLS = 2 * NA_COLS
KBAND = TQ_ROWS * KBAND_COLS
NKEY = 3 * KBAND
TM_E = 512
TM_CHUNK = 128
VMEM_LIMIT = 56 * 1024 * 1024


def _layer_norm(v, g, b):
    mu = jnp.mean(v, axis=-1, keepdims=True)
    c = v - mu
    var = jnp.mean(c * c, axis=-1, keepdims=True)
    return c * lax.rsqrt(var + LN_EPS) * g + b


def _band_start_cblk(g):
    return min(max(2 * g - 1, 0), N_CBLK - KBAND_COLS // SUBLANES)


def _stored_row_groups():
    return [r * N_CBLK + cb for cb in range(N_CBLK) for r in range(TQ_ROWS)]


def _inproj_kernel(x_ref, wqkv_ref, bqkv_ref, wrest_ref, brest_ref, qkv_ref, rest_ref):
    x = x_ref[...].astype(BF16)
    rest_ref[...] = jnp.dot(x, wrest_ref[...], preferred_element_type=F32) + brest_ref[...]
    groups = [x_ref[pl.ds(tile * TQ + gn * SUBLANES, SUBLANES), :]
              for tile in range(TM_IN // TQ) for gn in _stored_row_groups()]
    xp = jnp.concatenate(groups, axis=0).astype(BF16)
    qkv = jnp.dot(xp, wqkv_ref[...], preferred_element_type=F32) + bqkv_ref[...]
    qkv_ref[...] = qkv.astype(BF16)


def _inproj(x, wqkv, bqkv, wrest, brest):
    n = x.shape[0]
    return pl.pallas_call(
        _inproj_kernel,
        grid=(n // TM_IN,),
        in_specs=[
            pl.BlockSpec((TM_IN, D_MODEL), lambda i: (i, 0)),
            pl.BlockSpec((D_MODEL, 3 * NA_W), lambda i: (0, 0)),
            pl.BlockSpec((1, 3 * NA_W), lambda i: (0, 0)),
            pl.BlockSpec((D_MODEL, POOL_W + 2 * CONV_W), lambda i: (0, 0)),
            pl.BlockSpec((1, POOL_W + 2 * CONV_W), lambda i: (0, 0)),
        ],
        out_specs=[
            pl.BlockSpec((TM_IN, 3 * NA_W), lambda i: (i, 0)),
            pl.BlockSpec((TM_IN, POOL_W + 2 * CONV_W), lambda i: (i, 0)),
        ],
        out_shape=[
            jax.ShapeDtypeStruct((n, 3 * NA_W), BF16),
            jax.ShapeDtypeStruct((n, POOL_W + 2 * CONV_W), F32),
        ],
        compiler_params=pltpu.CompilerParams(
            dimension_semantics=("arbitrary",), vmem_limit_bytes=VMEM_LIMIT),
        name="inproj",
    )(x, wqkv, bqkv, wrest, brest)


def _attn_bias_tables(rpb, rows):
    kr_win = min(NA_ROWS, rows)
    n_tiles = rows // TQ_ROWS
    n_r, n_c = 2 * NA_ROWS - 1, 2 * NA_COLS - 1
    n_layers = rpb.shape[0]
    c_hots, c_oks = [], []
    for g in range(N_QGRP):
        qc = g * NA_COLS + np.arange(NA_COLS)
        kc = _band_start_cblk(g) * SUBLANES + np.arange(KBAND_COLS)
        sc = np.clip(qc - NA_COLS // 2, 0, GRID_W - NA_COLS)[:, None]
        c_oks.append((kc[None] >= sc) & (kc[None] < sc + NA_COLS))
        c_off = np.clip(kc[None] - qc[:, None] + NA_COLS - 1, 0, n_c - 1)
        c_hots.append((c_off[..., None] == np.arange(n_c)).astype(np.float32))
    r_hots, r_oks = [], []
    for tile in (0, 1, n_tiles - 1):
        qr = tile * TQ_ROWS + np.arange(TQ_ROWS)
        slot_tile = np.array([tile - 1, tile, tile + 1])
        kr = slot_tile[:, None] * TQ_ROWS + np.arange(TQ_ROWS)[None, :]
        slot_ok = ((slot_tile >= 0) & (slot_tile < n_tiles))[:, None]
        sr = np.clip(qr - kr_win // 2, 0, rows - kr_win)[:, None, None]
        r_oks.append((kr[None] >= sr) & (kr[None] < sr + kr_win) & slot_ok[None])
        r_off = np.clip(kr[None] - qr[:, None, None] + NA_ROWS - 1, 0, n_r - 1)
        r_hots.append((r_off[..., None] == np.arange(n_r)).astype(np.float32))
    c_hot = jnp.asarray(np.stack(c_hots))
    r_off = np.stack([np.argmax(h, axis=-1) for h in r_hots])
    c_ok = np.stack(c_oks).reshape(N_QGRP, 2, SUBLANES, KBAND_COLS // SUBLANES, SUBLANES)
    r_ok = np.stack(r_oks)
    ok = (r_ok[:, None, None, :, None, :, None, :, None]
          & c_ok[None, :, :, None, :, None, :, None, :])
    ok = np.broadcast_to(ok[:, :, None], (3, N_QGRP, 2) + ok.shape[2:])
    ok = jnp.asarray(ok.reshape(3, N_QGRP, 2 * QGRP, NKEY).astype(np.float32))
    rpb5 = rpb.astype(F32).reshape(n_layers, NA_HEADS // 2, 2, n_r, n_c)
    by_col = jnp.einsum("lpjrc,gmnc->lpgjmrn", rpb5, c_hot, precision=lax.Precision.HIGHEST)
    by_col = by_col.reshape(n_layers * (NA_HEADS // 2) * N_QGRP, 2 * NA_COLS, n_r * KBAND_COLS)
    nb = KBAND_COLS // SUBLANES
    expand = np.zeros((TQ_ROWS, 2, 2, TQ_ROWS, SUBLANES, 2, 2, SUBLANES), np.float32)
    for q in range(TQ_ROWS):
        for j in range(2):
            for a in range(2):
                for x in range(SUBLANES):
                    expand[q, j, a, q, x, j, a, x] = 1.0
    expand = expand.reshape(TQ_ROWS, 2 * QGRP, 2 * NA_COLS)
    select = np.zeros((3, TQ_ROWS, n_r, nb, SUBLANES, 3, nb, TQ_ROWS, SUBLANES), np.float32)
    for v in range(3):
        for q in range(TQ_ROWS):
            for s in range(3):
                for y in range(TQ_ROWS):
                    for b in range(nb):
                        for z in range(SUBLANES):
                            select[v, q, r_off[v, q, s, y], b, z, s, b, y, z] = 1.0
    select = select.reshape(3, TQ_ROWS, n_r * KBAND_COLS, NKEY)
    lane_pad = -(n_r * KBAND_COLS) % LANES
    by_col = jnp.pad(by_col, ((0, 0), (0, 0), (0, lane_pad)))
    select = np.pad(select, ((0, 0), (0, 0), (0, lane_pad), (0, 0)))
    tab = _bias_table_call(by_col, jnp.asarray(expand, BF16), jnp.asarray(select, BF16), ok)
    return tab.reshape(n_layers, NA_HEADS // 2, N_QGRP, 3, 2 * QGRP, NKEY)


def _bias_table_kernel(u_ref, expand_ref, select_ref, ok_ref, out_ref):
    u = u_ref[...]
    parts = []
    for _ in range(3):
        piece = u.astype(BF16)
        parts.append(piece)
        u = u - piece.astype(F32)
    n_k = u.shape[1]
    pieces = jnp.concatenate(parts, axis=1)
    acc = [jnp.zeros((2 * QGRP, NKEY), F32) for _ in range(3)]
    for q in range(TQ_ROWS):
        rows = jnp.dot(expand_ref[q], pieces, preferred_element_type=F32).astype(BF16)
        rows = jnp.concatenate([rows[:, k * n_k:(k + 1) * n_k] for k in range(3)], axis=0)
        for v in range(3):
            sel = jnp.dot(rows, select_ref[v, q], preferred_element_type=F32)
            acc[v] = acc[v] + (sel[0:2 * QGRP] + sel[2 * QGRP:4 * QGRP] + sel[4 * QGRP:])
    for v in range(3):
        out_ref[v] = jnp.where(ok_ref[v, 0] > 0.0, acc[v], NEG_INF)


def _bias_table_call(by_col, expand, select, ok):
    n, n_u, n_k = by_col.shape
    return pl.pallas_call(
        _bias_table_kernel,
        grid=(n,),
        in_specs=[
            pl.BlockSpec((None, n_u, n_k), lambda i: (i, 0, 0)),
            pl.BlockSpec(expand.shape, lambda i: (0, 0, 0)),
            pl.BlockSpec(select.shape, lambda i: (0, 0, 0, 0)),
            pl.BlockSpec((3, 1, 2 * QGRP, NKEY), lambda i: (0, i % N_QGRP, 0, 0)),
        ],
        out_specs=pl.BlockSpec((None, 3, 2 * QGRP, NKEY), lambda i: (i, 0, 0, 0)),
        out_shape=jax.ShapeDtypeStruct((n, 3, 2 * QGRP, NKEY), F32),
        compiler_params=pltpu.CompilerParams(
            dimension_semantics=("arbitrary",), vmem_limit_bytes=VMEM_LIMIT),
        name="bias_table",
    )(by_col, expand, select, ok)


def _mixer_kernel(tiles_per_seq, alpha,
                  x_ref, q_ref, kp_ref, kc_ref, kn_ref, vp_ref, vc_ref, vn_ref,
                  rc_ref, rp_ref, rn_ref, bias_ref,
                  wpool_ref, pscale_ref, dw_ref, dwb_ref, clng_ref, clnb_ref, wpw_ref, bpw_ref,
                  wout_ref, bout_ref, lng_ref, lnb_ref, wr_ref, br_ref, tri_ref,
                  x1_ref, topi_ref, topp_ref, rank_ref, counts_ref,
                  halo_ref, shift_ref, base_ref):
    i = pl.program_id(0)
    ib = i % tiles_per_seq
    has_prev = ib > 0
    has_next = ib < tiles_per_seq - 1

    @pl.when(i == 0)
    def _():
        base_ref[...] = jnp.zeros_like(base_ref)

    halo_ref[0:HALO, :] = jnp.where(has_prev, rp_ref[...], 0.0)
    halo_ref[HALO:HALO + TQ, :] = rc_ref[...]
    halo_ref[HALO + TQ:, :] = jnp.where(has_next, rn_ref[...], 0.0)

    def fill_shifts():
        for b in range(1, SUBLANES):
            shift_ref[b - 1] = halo_ref[pl.ds(b, TQ + 2 * HALO - SUBLANES), 0:CONV_W]

    def window(start):
        a8, b = divmod(start, SUBLANES)
        if b == 0:
            return halo_ref[pl.ds(a8 * SUBLANES, TQ), 0:CONV_W]
        return shift_ref[b - 1, pl.ds(a8 * SUBLANES, TQ), :]

    fill_shifts()

    def u_at(off):
        return window(HALO + off)

    t_seq = ib * TQ + lax.broadcasted_iota(I32, (TQ, 1), 0)
    seq_len = tiles_per_seq * TQ
    group = lax.broadcasted_iota(I32, (1, POOL_W), 1) // POOL_GROUP_W
    u0 = u_at(0)
    acc = u0
    mean = jnp.zeros((TQ, POOL_W), F32)
    done = 0
    for g, w in enumerate(POOL_WINDOWS):
        half = w // 2
        for o in range(done + 1, half + 1):
            acc = acc + u_at(-o) + (u_at(o - 1) if o > 1 else 0.0)
        done = half
        cnt = (jnp.minimum(t_seq + half, seq_len) - jnp.maximum(t_seq - half, 0)).astype(F32)
        mean = jnp.where(group == g, acc / cnt, mean)
    d = (mean - u0).astype(BF16)
    y_pool = jnp.dot(d, wpool_ref[...], preferred_element_type=F32) * pscale_ref[...]
    mix = jnp.dot(y_pool.astype(BF16), wout_ref[0:POOL_W, :], preferred_element_type=F32)

    a = halo_ref[:, POOL_W:POOL_W + CONV_W]
    gate = halo_ref[:, POOL_W + CONV_W:]
    halo_ref[:, 0:CONV_W] = a * jax.nn.sigmoid(gate)
    fill_shifts()
    conv = jnp.zeros((TQ, CONV_W), F32) + dwb_ref[...]
    for k in range(CONV_K):
        conv = conv + window(HALO - CONV_K // 2 + k) * dw_ref[k:k + 1, :]
    hc = _layer_norm(conv, clng_ref[...], clnb_ref[...])
    hc = hc * jax.nn.sigmoid(hc)
    y_conv = jnp.dot(hc.astype(BF16), wpw_ref[...], preferred_element_type=F32) + bpw_ref[...]
    mix = mix + jnp.dot(y_conv.astype(BF16), wout_ref[POOL_W + NA_W:, :], preferred_element_type=F32)

    low = lax.broadcasted_iota(I32, (QGRP, LANES), 1) < NA_HEAD_DIM
    scale = NA_HEAD_DIM ** -0.5
    for p in range(NA_HEADS // 2):
        cs = slice(p * LANES, (p + 1) * LANES)
        bands = [pl.ds(_band_start_cblk(g) * CBLK_ROWS, KBAND) for g in range(N_QGRP)]
        s_grp = []
        for g in range(N_QGRP):
            qp = q_ref[pl.ds(g * QGRP, QGRP), cs].astype(F32) * scale
            qs = jnp.concatenate([jnp.where(low, qp, 0.0), jnp.where(low, 0.0, qp)], axis=0).astype(BF16)
            kk = jnp.concatenate([kp_ref[bands[g], cs], kc_ref[bands[g], cs], kn_ref[bands[g], cs]], axis=0)
            s_grp.append(lax.dot_general(qs, kk, (((1,), (1,)), ((), ())), preferred_element_type=F32))
        s = jnp.concatenate(s_grp, axis=0) + bias_ref[p, :, 0].reshape(N_QGRP * 2 * QGRP, NKEY)
        m = jnp.max(s, axis=-1, keepdims=True)
        e = jnp.exp(s - m)
        rl = 1.0 / jnp.sum(e, axis=-1, keepdims=True)
        e = e.astype(BF16)
        o_grp = []
        for g in range(N_QGRP):
            vv = jnp.concatenate([vp_ref[bands[g], cs], vc_ref[bands[g], cs], vn_ref[bands[g], cs]], axis=0)
            rows = slice(g * 2 * QGRP, (g + 1) * 2 * QGRP)
            o = jnp.dot(e[rows], vv, preferred_element_type=F32) * rl[rows]
            o_grp.append(jnp.where(low, o[0:QGRP], o[QGRP:]))
        o_pair = jnp.concatenate(
            [o_grp[cb // 2][(cb % 2) * CBLK_ROWS + r * SUBLANES:(cb % 2) * CBLK_ROWS + (r + 1) * SUBLANES]
             for r in range(TQ_ROWS) for cb in range(N_CBLK)], axis=0)
        r0 = POOL_W + p * LANES
        mix = mix + jnp.dot(o_pair.astype(BF16), wout_ref[r0:r0 + LANES, :], preferred_element_type=F32)

    x1 = _layer_norm(alpha * x_ref[...] + mix + bout_ref[...], lng_ref[...], lnb_ref[...])
    x1_ref[...] = x1

    lt = lax.dot_general(wr_ref[...], x1.astype(BF16), (((1,), (1,)), ((), ())),
                         preferred_element_type=F32) + br_ref[...]
    eidx = lax.broadcasted_iota(I32, (N_EXPERTS, TQ), 0)
    work = lt
    vals, idxs = [], []
    for _ in range(TOP_K):
        mk = jnp.max(work, axis=0, keepdims=True)
        ik = jnp.min(jnp.where(work == mk, eidx, N_EXPERTS), axis=0, keepdims=True)
        vals.append(mk)
        idxs.append(ik)
        work = jnp.where(eidx == ik, -jnp.inf, work)
    ex = [jnp.exp(v - vals[0]) for v in vals]
    den = ex[0] + ex[1] + ex[2] + ex[3]
    topp_ref[...] = jnp.concatenate([e_ / den for e_ in ex], axis=0)
    topi_ref[...] = jnp.concatenate(idxs, axis=0)

    run = base_ref[...]
    ranks = []
    for k in range(TOP_K):
        hot = (eidx == idxs[k]).astype(F32)
        before = jnp.dot(hot.astype(BF16), tri_ref[...], preferred_element_type=F32)
        ranks.append(jnp.sum(hot * (run + before), axis=0, keepdims=True))
        run = run + jnp.sum(hot, axis=1, keepdims=True)
    rank_ref[...] = jnp.concatenate(ranks, axis=0).astype(I32)
    base_ref[...] = run
    counts_ref[...] = jnp.broadcast_to(run, counts_ref.shape).astype(I32)


def _mixer(x, qkv, rest, bias_tab, wpool_bd, pscale, dw, dwb, clng, clnb, wpw, bpw,
           wout, bout, lng, lnb, wr_t, br, tri, *, layer, seq, alpha):
    n = x.shape[0]
    nt = n // TQ
    tps = seq // TQ
    hb = TQ // HALO

    def prev_t(i):
        return jnp.where(i % tps == 0, i, i - 1)

    def next_t(i):
        return jnp.where(i % tps == tps - 1, i, i + 1)

    def variant(i):
        ib = i % tps
        return jnp.where(ib == 0, 0, jnp.where(ib == tps - 1, 2, 1))

    def const(shape):
        return pl.BlockSpec(shape, lambda i: tuple(0 for _ in shape))

    in_specs = [
        pl.BlockSpec((TQ, D_MODEL), lambda i: (i, 0)),
        pl.BlockSpec((TQ, NA_W), lambda i: (i, 0)),
        pl.BlockSpec((TQ, NA_W), lambda i: (prev_t(i), 1)),
        pl.BlockSpec((TQ, NA_W), lambda i: (i, 1)),
        pl.BlockSpec((TQ, NA_W), lambda i: (next_t(i), 1)),
        pl.BlockSpec((TQ, NA_W), lambda i: (prev_t(i), 2)),
        pl.BlockSpec((TQ, NA_W), lambda i: (i, 2)),
        pl.BlockSpec((TQ, NA_W), lambda i: (next_t(i), 2)),
        pl.BlockSpec((TQ, POOL_W + 2 * CONV_W), lambda i: (i, 0)),
        pl.BlockSpec((HALO, POOL_W + 2 * CONV_W), lambda i: (jnp.maximum(i * hb - 1, 0), 0)),
        pl.BlockSpec((HALO, POOL_W + 2 * CONV_W), lambda i: (jnp.minimum((i + 1) * hb, nt * hb - 1), 0)),
        pl.BlockSpec((None, NA_HEADS // 2, N_QGRP, 1, 2 * QGRP, NKEY),
                     lambda i: (layer, 0, 0, variant(i), 0, 0)),
        const((POOL_W, POOL_W)), const((1, POOL_W)),
        const((CONV_K, CONV_W)), const((1, CONV_W)), const((1, CONV_W)), const((1, CONV_W)),
        const((CONV_W, CONV_W)), const((1, CONV_W)),
        const((D_MODEL, D_MODEL)), const((1, D_MODEL)), const((1, D_MODEL)), const((1, D_MODEL)),
        const((N_EXPERTS, D_MODEL)), const((N_EXPERTS, 1)), const((TQ, TQ)),
    ]
    out_specs = [
        pl.BlockSpec((TQ, D_MODEL), lambda i: (i, 0)),
        pl.BlockSpec((TOP_K, TQ), lambda i: (0, i)),
        pl.BlockSpec((TOP_K, TQ), lambda i: (0, i)),
        pl.BlockSpec((TOP_K, TQ), lambda i: (0, i)),
        pl.BlockSpec((N_EXPERTS, LANES), lambda i: (0, 0)),
    ]
    out_shape = [
        jax.ShapeDtypeStruct((n, D_MODEL), F32),
        jax.ShapeDtypeStruct((TOP_K, n), I32),
        jax.ShapeDtypeStruct((TOP_K, n), F32),
        jax.ShapeDtypeStruct((TOP_K, n), I32),
        jax.ShapeDtypeStruct((N_EXPERTS, LANES), I32),
    ]
    return pl.pallas_call(
        functools.partial(_mixer_kernel, tps, alpha),
        grid=(nt,),
        in_specs=in_specs,
        out_specs=out_specs,
        out_shape=out_shape,
        scratch_shapes=[
            pltpu.VMEM((TQ + 2 * HALO, POOL_W + 2 * CONV_W), F32),
            pltpu.VMEM((SUBLANES - 1, TQ + 2 * HALO - SUBLANES, CONV_W), F32),
            pltpu.VMEM((N_EXPERTS, 1), F32),
        ],
        compiler_params=pltpu.CompilerParams(
            dimension_semantics=("arbitrary",), vmem_limit_bytes=VMEM_LIMIT),
        name="mixer",
    )(x, qkv, qkv, qkv, qkv, qkv, qkv, qkv, rest, rest, rest, bias_tab,
      wpool_bd, pscale, dw, dwb, clng, clnb, wpw, bpw, wout, bout, lng, lnb, wr_t, br, tri)


PAD_BLOCKS = (8, 16, 32, 64, 128, 256)
assert PAD_BLOCKS[0] == SUBLANES and TM_E == 2 * PAD_BLOCKS[-1]


def _dispatch_kernel(pad_lo_ref, pad_hi_ref, pos_ref, x_ref, xs_hbm, zero_ref, sem, pad_sem):
    i = pl.program_id(0)

    def pad_fill(act):
        for e in range(N_EXPERTS):
            lo = pad_lo_ref[e]
            hi = pad_hi_ref[e]
            head_end = jnp.minimum((lo + (SUBLANES - 1)) & -SUBLANES, hi)
            for j in range(SUBLANES - 1):
                @pl.when(lo + j < head_end)
                def _(j=j, lo=lo):
                    act(pltpu.make_async_copy(
                        zero_ref.at[pl.ds(0, 1), :], xs_hbm.at[pl.ds(lo + j, 1), :], pad_sem))
            a = head_end
            for b in PAD_BLOCKS:
                take = (a & b) != 0

                @pl.when(take)
                def _(a=a, b=b):
                    act(pltpu.make_async_copy(
                        zero_ref.at[pl.ds(0, b), :], xs_hbm.at[pl.ds(pl.multiple_of(a, SUBLANES), b), :], pad_sem))
                a = jnp.where(take, a + b, a)
        blk = PAD_BLOCKS[-1]

        def tail(c, carry):
            act(pltpu.make_async_copy(
                zero_ref, xs_hbm.at[pl.ds(pl.multiple_of(c * blk, blk), blk), :], pad_sem))
            return carry
        lax.fori_loop(pad_hi_ref[N_EXPERTS - 1] // blk, xs_hbm.shape[0] // blk, tail, 0)

    @pl.when(i == 0)
    def _():
        zero_ref[...] = jnp.zeros_like(zero_ref)
        pad_fill(lambda c: c.start())

    for k in range(TOP_K):
        for r in range(TQ):
            pltpu.make_async_copy(
                x_ref.at[pl.ds(r, 1), :], xs_hbm.at[pl.ds(pos_ref[k, r], 1), :], sem).start(priority=r % 2)
    for k in range(TOP_K):
        pltpu.make_async_copy(x_ref, xs_hbm.at[pl.ds(0, TQ), :], sem).wait()

    @pl.when(i == 0)
    def _():
        pad_fill(lambda c: c.wait())


def _dispatch(x1, pos, pad_lo, pad_hi, m_pad):
    n = x1.shape[0]
    grid_spec = pltpu.PrefetchScalarGridSpec(
        num_scalar_prefetch=2,
        grid=(n // TQ,),
        in_specs=[
            pl.BlockSpec((TOP_K, TQ), lambda i, *_: (0, i), memory_space=pltpu.SMEM),
            pl.BlockSpec((TQ, D_MODEL), lambda i, *_: (i, 0)),
        ],
        out_specs=pl.BlockSpec(memory_space=pl.ANY),
        scratch_shapes=[
            pltpu.VMEM((PAD_BLOCKS[-1], D_MODEL), F32),
            pltpu.SemaphoreType.DMA(()),
            pltpu.SemaphoreType.DMA(()),
        ],
    )
    return pl.pallas_call(
        _dispatch_kernel,
        grid_spec=grid_spec,
        out_shape=jax.ShapeDtypeStruct((m_pad, D_MODEL), F32),
        compiler_params=pltpu.CompilerParams(dimension_semantics=("arbitrary",)),
        name="dispatch",
    )(pad_lo, pad_hi, pos, x1)


def _experts_kernel(layer, te_ref, na_ref, first_ref, slot_ref, nxt_ref, rows_ref,
                    xs_ref, bg_ref, bu_ref, bd_ref, wg_hbm, wu_hbm, wd_hbm,
                    y_ref, wbuf, wg_bf, wu_bf, wd_bf, sem):
    t = pl.program_id(0)
    active = t < na_ref[0]
    e = te_ref[t]
    s = slot_ref[t]

    def fetch(expert, slot):
        return [pltpu.make_async_copy(w.at[layer, expert], wbuf.at[j, slot], sem.at[j, slot])
                for j, w in enumerate((wg_hbm, wu_hbm, wd_hbm))]

    @pl.when(t == 0)
    def _():
        for c in fetch(e, s):
            c.start()

    @pl.when(jnp.logical_and(active, first_ref[t] == 1))
    def _():
        for c in fetch(e, s):
            c.wait()

        @pl.when(nxt_ref[t] != e)
        def _():
            for c in fetch(nxt_ref[t], 1 - s):
                c.start()

        wg_bf[...] = wbuf[0, s].astype(BF16)
        wu_bf[...] = wbuf[1, s].astype(BF16)
        wd_bf[...] = wbuf[2, s].astype(BF16)

    for chunks in range(1, TM_E // TM_CHUNK + 1):
        m = chunks * TM_CHUNK

        @pl.when(jnp.logical_and(active, rows_ref[t] == chunks))
        def _(m=m):
            x = xs_ref[0:m, :].astype(BF16)
            g = jnp.minimum(jnp.dot(x, wg_bf[...], preferred_element_type=F32) + bg_ref[...], SWIGLU_LIMIT)
            u = jnp.clip(jnp.dot(x, wu_bf[...], preferred_element_type=F32) + bu_ref[...],
                         -SWIGLU_LIMIT, SWIGLU_LIMIT)
            act = (u + 1.0) * g * jax.nn.sigmoid(SWIGLU_ALPHA * g)
            y_ref[0:m, :] = jnp.dot(act.astype(BF16), wd_bf[...], preferred_element_type=F32) + bd_ref[...]
            if m < TM_E:
                y_ref[m:, :] = jnp.zeros((TM_E - m, D_MODEL), F32)

    @pl.when(jnp.logical_not(active))
    def _():
        y_ref[...] = jnp.zeros_like(y_ref)


def _experts(layer, tile_expert, n_active, tile_first, tile_slot, tile_next, tile_chunks,
             xs, wg, bg, wu, bu, wd, bd):
    m_pad = xs.shape[0]
    n_tiles = m_pad // TM_E

    def xmap(t, te, na, *_):
        return (jnp.minimum(t, na[0] - 1), 0)

    def bmap(t, te, *_):
        return (layer, te[t], 0, 0)

    grid_spec = pltpu.PrefetchScalarGridSpec(
        num_scalar_prefetch=6,
        grid=(n_tiles,),
        in_specs=[
            pl.BlockSpec((TM_E, D_MODEL), xmap),
            pl.BlockSpec((None, None, 1, D_MODEL), bmap),
            pl.BlockSpec((None, None, 1, D_MODEL), bmap),
            pl.BlockSpec((None, None, 1, D_MODEL), bmap),
            pl.BlockSpec(memory_space=pl.ANY),
            pl.BlockSpec(memory_space=pl.ANY),
            pl.BlockSpec(memory_space=pl.ANY),
        ],
        out_specs=pl.BlockSpec((TM_E, D_MODEL), lambda t, *_: (t, 0)),
        scratch_shapes=[
            pltpu.VMEM((3, 2, D_MODEL, D_MODEL), F32),
            pltpu.VMEM((D_MODEL, D_MODEL), BF16),
            pltpu.VMEM((D_MODEL, D_MODEL), BF16),
            pltpu.VMEM((D_MODEL, D_MODEL), BF16),
            pltpu.SemaphoreType.DMA((3, 2)),
        ],
    )
    return pl.pallas_call(
        functools.partial(_experts_kernel, layer),
        grid_spec=grid_spec,
        out_shape=jax.ShapeDtypeStruct((m_pad, D_MODEL), F32),
        compiler_params=pltpu.CompilerParams(
            dimension_semantics=("arbitrary",), vmem_limit_bytes=VMEM_LIMIT),
        name="experts",
    )(tile_expert, n_active, tile_first, tile_slot, tile_next, tile_chunks, xs, bg, bu, bd, wg, wu, wd)


def _combine_kernel(n_tiles, alpha, pos_ref, x1_ref, gates_ref, lng_ref, lnb_ref, ys_hbm, out_ref, buf, sem):
    j = pl.program_id(0)

    @pl.when(j < n_tiles)
    def _():
        slot = j % 2
        for k in range(TOP_K):
            for r in range(TQ):
                pltpu.make_async_copy(
                    ys_hbm.at[pl.ds(pos_ref[k, r], 1), :], buf.at[slot, k, pl.ds(r, 1), :],
                    sem.at[slot]).start(priority=r % 2)

    @pl.when(j > 0)
    def _():
        slot = (j - 1) % 2
        for k in range(TOP_K):
            pltpu.make_async_copy(ys_hbm.at[pl.ds(0, TQ), :], buf.at[slot, k], sem.at[slot]).wait()
        gates = gates_ref[...]
        ffn = buf[slot, 0] * gates[:, 0:1]
        for k in range(1, TOP_K):
            ffn = ffn + buf[slot, k] * gates[:, k:k + 1]
        out_ref[...] = _layer_norm(alpha * x1_ref[...] + ffn, lng_ref[...], lnb_ref[...])


def _combine(pos, x1, gates_tm, lng, lnb, ys, *, alpha):
    n = x1.shape[0]
    n_tiles = n // TQ
    req = lambda j: jnp.minimum(j, n_tiles - 1)
    fin = lambda j: jnp.maximum(j - 1, 0)
    return pl.pallas_call(
        functools.partial(_combine_kernel, n_tiles, alpha),
        grid=(n_tiles + 1,),
        in_specs=[
            pl.BlockSpec((TOP_K, TQ), lambda j: (0, req(j)), memory_space=pltpu.SMEM),
            pl.BlockSpec((TQ, D_MODEL), lambda j: (fin(j), 0)),
            pl.BlockSpec((TQ, TOP_K), lambda j: (fin(j), 0)),
            pl.BlockSpec((1, D_MODEL), lambda j: (0, 0)),
            pl.BlockSpec((1, D_MODEL), lambda j: (0, 0)),
            pl.BlockSpec(memory_space=pl.ANY),
        ],
        out_specs=pl.BlockSpec((TQ, D_MODEL), lambda j: (fin(j), 0)),
        out_shape=jax.ShapeDtypeStruct((n, D_MODEL), F32),
        scratch_shapes=[pltpu.VMEM((2, TOP_K, TQ, D_MODEL), F32), pltpu.SemaphoreType.DMA((2,))],
        compiler_params=pltpu.CompilerParams(
            dimension_semantics=("arbitrary",), vmem_limit_bytes=VMEM_LIMIT),
        name="combine",
    )(pos, x1, gates_tm, lng, lnb, ys)


def kernel(x, w_in, b_in, w_pool, pool_scale, rpb, conv_dw, conv_dw_b, conv_ln_g, conv_ln_b,
           w_conv_pw, b_conv_pw, w_out, b_out, ln1_g, ln1_b, w_router, b_router,
           w_gate, b_gate, w_up, b_up, w_down, b_down, ln2_g, ln2_b):
    batch, seq, d = x.shape
    depth = w_in.shape[0]
    n = batch * seq
    rows = seq // GRID_W
    alpha = (2.0 * depth) ** 0.25
    off_q, off_k, off_v = POOL_W, POOL_W + NA_W, POOL_W + 2 * NA_W
    off_ca = off_v + NA_W
    m_pad = n * TOP_K + N_EXPERTS * TM_E
    n_tiles = m_pad // TM_E

    tri = (np.arange(TQ)[:, None] < np.arange(TQ)[None, :]).astype(np.float32)
    tri = jnp.asarray(tri, BF16)
    row2 = lambda v: v.reshape(1, -1)
    bias_tabs = _attn_bias_tables(rpb, rows)
    experts_iota = jnp.arange(N_EXPERTS, dtype=I32)
    tile_starts = jnp.arange(n_tiles, dtype=I32) * TM_E
    b_gate4 = b_gate.reshape(depth, N_EXPERTS, 1, d)
    b_up4 = b_up.reshape(depth, N_EXPERTS, 1, d)
    b_down4 = b_down.reshape(depth, N_EXPERTS, 1, d)

    h = x.reshape(n, d)
    for l in range(depth):
        wqkv = w_in[l][:, off_q:off_ca].astype(BF16)
        bqkv = row2(b_in[l][off_q:off_ca])
        wrest = jnp.concatenate([w_in[l][:, :off_q], w_in[l][:, off_ca:]], axis=1).astype(BF16)
        brest = row2(jnp.concatenate([b_in[l][:off_q], b_in[l][off_ca:]]))
        qkv, rest = _inproj(h, wqkv, bqkv, wrest, brest)

        wpool_bd = jax.scipy.linalg.block_diag(*[w_pool[l][g] for g in range(len(POOL_WINDOWS))]).astype(BF16)
        x1, top_i, top_p, rank, counts = _mixer(
            h, qkv, rest, bias_tabs, wpool_bd, row2(pool_scale[l]),
            conv_dw[l].reshape(CONV_K, CONV_W), row2(conv_dw_b[l]), row2(conv_ln_g[l]), row2(conv_ln_b[l]),
            w_conv_pw[l].astype(BF16), row2(b_conv_pw[l]),
            w_out[l].astype(BF16), row2(b_out[l]), row2(ln1_g[l]), row2(ln1_b[l]),
            w_router[l].T.astype(BF16), b_router[l].reshape(N_EXPERTS, 1), tri,
            layer=l, seq=seq, alpha=alpha)

        cnt = counts[:, 0]
        cpad = ((cnt + TM_E - 1) // TM_E) * TM_E
        ends = jnp.cumsum(cpad)
        off = ends - cpad
        hot = top_i[None] == experts_iota[:, None, None]
        pos = rank + jnp.sum(jnp.where(hot, off[:, None, None], 0), axis=0)
        n_active = (ends[-1] // TM_E).astype(I32)
        last_start = jnp.minimum(tile_starts, ends[-1] - TM_E)
        tile_expert = jnp.sum((ends[None, :] <= last_start[:, None]).astype(I32), axis=1)
        owns = cnt > 0
        e_slot = (jnp.cumsum(owns.astype(I32)) - 1) % 2
        later = jnp.logical_and(experts_iota[None, :] > experts_iota[:, None], owns[None, :])
        e_next = jnp.min(jnp.where(later, experts_iota[None, :], N_EXPERTS), axis=1)
        e_next = jnp.where(e_next == N_EXPERTS, experts_iota, e_next)
        tile_hot = tile_expert[:, None] == experts_iota[None, :]
        pick = lambda v: jnp.sum(jnp.where(tile_hot, v[None, :], 0), axis=1).astype(I32)
        tile_first = (pick(off) == tile_starts).astype(I32)
        tile_slot = pick(e_slot)
        tile_next = pick(e_next)

        xs = _dispatch(x1, pos, (off + cnt).astype(I32), ends.astype(I32), m_pad)
        real_rows = jnp.clip(pick(off + cnt) - tile_starts, 0, TM_E)
        tile_chunks = (real_rows + TM_CHUNK - 1) // TM_CHUNK
        ys = _experts(l, tile_expert, n_active.reshape(1), tile_first, tile_slot, tile_next, tile_chunks, xs,
                      w_gate, b_gate4, w_up, b_up4, w_down, b_down4)
        h = _combine(pos, x1, top_p.T, row2(ln2_g[l]), row2(ln2_b[l]), ys, alpha=alpha)
    return h.reshape(batch, seq, d)
```

```python
import functools

import jax
import jax.numpy as jnp
import numpy as np
from jax import lax
from jax.experimental import pallas as pl
from jax.experimental.pallas import tpu as pltpu

F32 = jnp.float32
BF16 = jnp.bfloat16
I32 = jnp.int32

D_MODEL = 1024
GRID_W = 64
POOL_W = 256
POOL_WINDOWS = (2, 4, 8, 16)
POOL_GROUP_W = 64
NA_W = 512
NA_HEADS = 8
NA_HEAD_DIM = 64
NA_ROWS = 8
NA_COLS = 16
CONV_W = 256
CONV_K = 31
N_EXPERTS = 32
TOP_K = 4
SWIGLU_ALPHA = 1.702
SWIGLU_LIMIT = 7.0
LN_EPS = 1e-5
NEG_INF = -1e30

LANES = 128
TM_IN = 512
TQ = 256
TQ_ROWS = TQ // GRID_W
HALO = 16
SUBLANES = 8
N_CBLK = GRID_W // SUBLANES
CBLK_ROWS = TQ_ROWS * SUBLANES
N_QGRP = GRID_W // NA_COLS
QGRP = TQ // N_QGRP
KBAND_COLS = 2 * NA_COLS
KBAND = TQ_ROWS * KBAND_COLS
NKEY = 3 * KBAND
TM_E = 512
TM_CHUNK = 128
VMEM_LIMIT = 56 * 1024 * 1024


def _layer_norm(v, g, b):
    mu = jnp.mean(v, axis=-1, keepdims=True)
    c = v - mu
    var = jnp.mean(c * c, axis=-1, keepdims=True)
    return c * lax.rsqrt(var + LN_EPS) * g + b


def _band_start_cblk(g):
    return min(max(2 * g - 1, 0), N_CBLK - KBAND_COLS // SUBLANES)


def _stored_row_groups():
    return [r * N_CBLK + cb for cb in range(N_CBLK) for r in range(TQ_ROWS)]


def _inproj_kernel(x_ref, wqkv_ref, bqkv_ref, wrest_ref, brest_ref, qkv_ref, rest_ref):
    x = x_ref[...].astype(BF16)
    rest_ref[...] = jnp.dot(x, wrest_ref[...], preferred_element_type=F32) + brest_ref[...]
    groups = [x_ref[pl.ds(tile * TQ + gn * SUBLANES, SUBLANES), :]
              for tile in range(TM_IN // TQ) for gn in _stored_row_groups()]
    xp = jnp.concatenate(groups, axis=0).astype(BF16)
    qkv = jnp.dot(xp, wqkv_ref[...], preferred_element_type=F32) + bqkv_ref[...]
    qkv_ref[...] = qkv.astype(BF16)


def _inproj(x, wqkv, bqkv, wrest, brest):
    n = x.shape[0]
    return pl.pallas_call(
        _inproj_kernel,
        grid=(n // TM_IN,),
        in_specs=[
            pl.BlockSpec((TM_IN, D_MODEL), lambda i: (i, 0)),
            pl.BlockSpec((D_MODEL, 3 * NA_W), lambda i: (0, 0)),
            pl.BlockSpec((1, 3 * NA_W), lambda i: (0, 0)),
            pl.BlockSpec((D_MODEL, POOL_W + 2 * CONV_W), lambda i: (0, 0)),
            pl.BlockSpec((1, POOL_W + 2 * CONV_W), lambda i: (0, 0)),
        ],
        out_specs=[
            pl.BlockSpec((TM_IN, 3 * NA_W), lambda i: (i, 0)),
            pl.BlockSpec((TM_IN, POOL_W + 2 * CONV_W), lambda i: (i, 0)),
        ],
        out_shape=[
            jax.ShapeDtypeStruct((n, 3 * NA_W), BF16),
            jax.ShapeDtypeStruct((n, POOL_W + 2 * CONV_W), F32),
        ],
        compiler_params=pltpu.CompilerParams(
            dimension_semantics=("arbitrary",), vmem_limit_bytes=VMEM_LIMIT),
        name="inproj",
    )(x, wqkv, bqkv, wrest, brest)


def _attn_bias_tables(rpb, rows):
    kr_win = min(NA_ROWS, rows)
    n_tiles = rows // TQ_ROWS
    n_r, n_c = 2 * NA_ROWS - 1, 2 * NA_COLS - 1
    n_layers = rpb.shape[0]
    c_hots, c_oks = [], []
    for g in range(N_QGRP):
        qc = g * NA_COLS + np.arange(NA_COLS)
        kc = _band_start_cblk(g) * SUBLANES + np.arange(KBAND_COLS)
        sc = np.clip(qc - NA_COLS // 2, 0, GRID_W - NA_COLS)[:, None]
        c_oks.append((kc[None] >= sc) & (kc[None] < sc + NA_COLS))
        c_off = np.clip(kc[None] - qc[:, None] + NA_COLS - 1, 0, n_c - 1)
        c_hots.append((c_off[..., None] == np.arange(n_c)).astype(np.float32))
    r_hots, r_oks = [], []
    for tile in (0, 1, n_tiles - 1):
        qr = tile * TQ_ROWS + np.arange(TQ_ROWS)
        slot_tile = np.array([tile - 1, tile, tile + 1])
        kr = slot_tile[:, None] * TQ_ROWS + np.arange(TQ_ROWS)[None, :]
        slot_ok = ((slot_tile >= 0) & (slot_tile < n_tiles))[:, None]
        sr = np.clip(qr - kr_win // 2, 0, rows - kr_win)[:, None, None]
        r_oks.append((kr[None] >= sr) & (kr[None] < sr + kr_win) & slot_ok[None])
        r_off = np.clip(kr[None] - qr[:, None, None] + NA_ROWS - 1, 0, n_r - 1)
        r_hots.append((r_off[..., None] == np.arange(n_r)).astype(np.float32))
    c_hot = jnp.asarray(np.stack(c_hots))
    r_off = np.stack([np.argmax(h, axis=-1) for h in r_hots])
    c_ok = np.stack(c_oks).reshape(N_QGRP, 2, SUBLANES, KBAND_COLS // SUBLANES, SUBLANES)
    r_ok = np.stack(r_oks)
    ok = (r_ok[:, None, None, :, None, :, None, :, None]
          & c_ok[None, :, :, None, :, None, :, None, :])
    ok = np.broadcast_to(ok[:, :, None], (3, N_QGRP, 2) + ok.shape[2:])
    ok = jnp.asarray(ok.reshape(3, N_QGRP, 2 * QGRP, NKEY).astype(np.float32))
    rpb5 = rpb.astype(F32).reshape(n_layers, NA_HEADS // 2, 2, n_r, n_c)
    by_col = jnp.einsum("lpjrc,gmnc->lpgjmrn", rpb5, c_hot, precision=lax.Precision.HIGHEST)
    by_col = by_col.reshape(n_layers * (NA_HEADS // 2) * N_QGRP, 2 * NA_COLS, n_r * KBAND_COLS)
    nb = KBAND_COLS // SUBLANES
    expand = np.zeros((TQ_ROWS, 2, 2, TQ_ROWS, SUBLANES, 2, 2, SUBLANES), np.float32)
    for q in range(TQ_ROWS):
        for j in range(2):
            for a in range(2):
                for x in range(SUBLANES):
                    expand[q, j, a, q, x, j, a, x] = 1.0
    expand = expand.reshape(TQ_ROWS, 2 * QGRP, 2 * NA_COLS)
    select = np.zeros((3, TQ_ROWS, n_r, nb, SUBLANES, 3, nb, TQ_ROWS, SUBLANES), np.float32)
    for v in range(3):
        for q in range(TQ_ROWS):
            for s in range(3):
                for y in range(TQ_ROWS):
                    for b in range(nb):
                        for z in range(SUBLANES):
                            select[v, q, r_off[v, q, s, y], b, z, s, b, y, z] = 1.0
    select = select.reshape(3, TQ_ROWS, n_r * KBAND_COLS, NKEY)
    lane_pad = -(n_r * KBAND_COLS) % LANES
    by_col = jnp.pad(by_col, ((0, 0), (0, 0), (0, lane_pad)))
    select = np.pad(select, ((0, 0), (0, 0), (0, lane_pad), (0, 0)))
    tab = _bias_table_call(by_col, jnp.asarray(expand, BF16), jnp.asarray(select, BF16), ok)
    return tab.reshape(n_layers, NA_HEADS // 2, N_QGRP, 3, 2 * QGRP, NKEY)


def _bias_table_kernel(u_ref, expand_ref, select_ref, ok_ref, out_ref):
    u = u_ref[...]
    parts = []
    for _ in range(3):
        piece = u.astype(BF16)
        parts.append(piece)
        u = u - piece.astype(F32)
    n_k = u.shape[1]
    pieces = jnp.concatenate(parts, axis=1)
    acc = [jnp.zeros((2 * QGRP, NKEY), F32) for _ in range(3)]
    for q in range(TQ_ROWS):
        rows = jnp.dot(expand_ref[q], pieces, preferred_element_type=F32).astype(BF16)
        rows = jnp.concatenate([rows[:, k * n_k:(k + 1) * n_k] for k in range(3)], axis=0)
        for v in range(3):
            sel = jnp.dot(rows, select_ref[v, q], preferred_element_type=F32)
            acc[v] = acc[v] + (sel[0:2 * QGRP] + sel[2 * QGRP:4 * QGRP] + sel[4 * QGRP:])
    for v in range(3):
        out_ref[v] = jnp.where(ok_ref[v, 0] > 0.0, acc[v], NEG_INF)


def _bias_table_call(by_col, expand, select, ok):
    n, n_u, n_k = by_col.shape
    return pl.pallas_call(
        _bias_table_kernel,
        grid=(n,),
        in_specs=[
            pl.BlockSpec((None, n_u, n_k), lambda i: (i, 0, 0)),
            pl.BlockSpec(expand.shape, lambda i: (0, 0, 0)),
            pl.BlockSpec(select.shape, lambda i: (0, 0, 0, 0)),
            pl.BlockSpec((3, 1, 2 * QGRP, NKEY), lambda i: (0, i % N_QGRP, 0, 0)),
        ],
        out_specs=pl.BlockSpec((None, 3, 2 * QGRP, NKEY), lambda i: (i, 0, 0, 0)),
        out_shape=jax.ShapeDtypeStruct((n, 3, 2 * QGRP, NKEY), F32),
        compiler_params=pltpu.CompilerParams(
            dimension_semantics=("arbitrary",), vmem_limit_bytes=VMEM_LIMIT),
        name="bias_table",
    )(by_col, expand, select, ok)


def _mixer_kernel(tiles_per_seq, alpha,
                  x_ref, q_ref, kp_ref, kc_ref, kn_ref, vp_ref, vc_ref, vn_ref,
                  rc_ref, rp_ref, rn_ref, bias_ref,
                  wpool_ref, pscale_ref, dw_ref, dwb_ref, clng_ref, clnb_ref, wpw_ref, bpw_ref,
                  wout_ref, bout_ref, lng_ref, lnb_ref, wr_ref, br_ref, tri_ref,
                  x1_ref, topi_ref, topp_ref, rank_ref, counts_ref,
                  halo_ref, shift_ref, base_ref):
    i = pl.program_id(0)
    ib = i % tiles_per_seq
    has_prev = ib > 0
    has_next = ib < tiles_per_seq - 1

    @pl.when(i == 0)
    def _():
        base_ref[...] = jnp.zeros_like(base_ref)

    halo_ref[0:HALO, :] = jnp.where(has_prev, rp_ref[...], 0.0)
    halo_ref[HALO:HALO + TQ, :] = rc_ref[...]
    halo_ref[HALO + TQ:, :] = jnp.where(has_next, rn_ref[...], 0.0)

    def fill_shifts():
        for b in range(1, SUBLANES):
            shift_ref[b - 1] = halo_ref[pl.ds(b, TQ + 2 * HALO - SUBLANES), 0:CONV_W]

    def window(start):
        a8, b = divmod(start, SUBLANES)
        if b == 0:
            return halo_ref[pl.ds(a8 * SUBLANES, TQ), 0:CONV_W]
        return shift_ref[b - 1, pl.ds(a8 * SUBLANES, TQ), :]

    fill_shifts()

    def u_at(off):
        return window(HALO + off)

    t_seq = ib * TQ + lax.broadcasted_iota(I32, (TQ, 1), 0)
    seq_len = tiles_per_seq * TQ
    group = lax.broadcasted_iota(I32, (1, POOL_W), 1) // POOL_GROUP_W
    u0 = u_at(0)
    acc = u0
    mean = jnp.zeros((TQ, POOL_W), F32)
    done = 0
    for g, w in enumerate(POOL_WINDOWS):
        half = w // 2
        for o in range(done + 1, half + 1):
            acc = acc + u_at(-o) + (u_at(o - 1) if o > 1 else 0.0)
        done = half
        cnt = (jnp.minimum(t_seq + half, seq_len) - jnp.maximum(t_seq - half, 0)).astype(F32)
        mean = jnp.where(group == g, acc / cnt, mean)
    d = (mean - u0).astype(BF16)
    y_pool = jnp.dot(d, wpool_ref[...], preferred_element_type=F32) * pscale_ref[...]
    mix = jnp.dot(y_pool.astype(BF16), wout_ref[0:POOL_W, :], preferred_element_type=F32)

    a = halo_ref[:, POOL_W:POOL_W + CONV_W]
    gate = halo_ref[:, POOL_W + CONV_W:]
    halo_ref[:, 0:CONV_W] = a * jax.nn.sigmoid(gate)
    fill_shifts()
    conv = jnp.zeros((TQ, CONV_W), F32) + dwb_ref[...]
    for k in range(CONV_K):
        conv = conv + window(HALO - CONV_K // 2 + k) * dw_ref[k:k + 1, :]
    hc = _layer_norm(conv, clng_ref[...], clnb_ref[...])
    hc = hc * jax.nn.sigmoid(hc)
    y_conv = jnp.dot(hc.astype(BF16), wpw_ref[...], preferred_element_type=F32) + bpw_ref[...]
    mix = mix + jnp.dot(y_conv.astype(BF16), wout_ref[POOL_W + NA_W:, :], preferred_element_type=F32)

    low = lax.broadcasted_iota(I32, (QGRP, LANES), 1) < NA_HEAD_DIM
    scale = NA_HEAD_DIM ** -0.5
    n_pairs = NA_HEADS // 2
    bands = [pl.ds(_band_start_cblk(g) * CBLK_ROWS, KBAND) for g in range(N_QGRP)]
    col = lambda p: slice(p * LANES, (p + 1) * LANES)
    s_blk = []
    for p in range(n_pairs):
        for g in range(N_QGRP):
            qp = q_ref[pl.ds(g * QGRP, QGRP), col(p)].astype(F32) * scale
            qs = jnp.concatenate([jnp.where(low, qp, 0.0), jnp.where(low, 0.0, qp)], axis=0).astype(BF16)
            kk = jnp.concatenate([kp_ref[bands[g], col(p)], kc_ref[bands[g], col(p)], kn_ref[bands[g], col(p)]],
                                 axis=0)
            s_blk.append(lax.dot_general(qs, kk, (((1,), (1,)), ((), ())), preferred_element_type=F32))
    blk_rows = 2 * QGRP
    s = jnp.concatenate(s_blk, axis=0) + bias_ref[...].reshape(n_pairs * N_QGRP * blk_rows, NKEY)
    m = jnp.max(s, axis=-1, keepdims=True)
    e = jnp.exp(s - m)
    rl = 1.0 / jnp.sum(e, axis=-1, keepdims=True)
    e = e.astype(BF16)
    for p in range(n_pairs):
        o_grp = []
        for g in range(N_QGRP):
            vv = jnp.concatenate([vp_ref[bands[g], col(p)], vc_ref[bands[g], col(p)], vn_ref[bands[g], col(p)]],
                                 axis=0)
            rows = slice((p * N_QGRP + g) * blk_rows, (p * N_QGRP + g + 1) * blk_rows)
            o = jnp.dot(e[rows], vv, preferred_element_type=F32) * rl[rows]
            o_grp.append(jnp.where(low, o[0:QGRP], o[QGRP:]))
        o_pair = jnp.concatenate(
            [o_grp[cb // 2][(cb % 2) * CBLK_ROWS + r * SUBLANES:(cb % 2) * CBLK_ROWS + (r + 1) * SUBLANES]
             for r in range(TQ_ROWS) for cb in range(N_CBLK)], axis=0)
        r0 = POOL_W + p * LANES
        mix = mix + jnp.dot(o_pair.astype(BF16), wout_ref[r0:r0 + LANES, :], preferred_element_type=F32)

    x1 = _layer_norm(alpha * x_ref[...] + mix + bout_ref[...], lng_ref[...], lnb_ref[...])
    x1_ref[...] = x1

    lt = lax.dot_general(wr_ref[...], x1.astype(BF16), (((1,), (1,)), ((), ())),
                         preferred_element_type=F32) + br_ref[...]
    eidx = lax.broadcasted_iota(I32, (N_EXPERTS, TQ), 0)
    work = lt
    vals, idxs = [], []
    for _ in range(TOP_K):
        mk = jnp.max(work, axis=0, keepdims=True)
        ik = jnp.min(jnp.where(work == mk, eidx, N_EXPERTS), axis=0, keepdims=True)
        vals.append(mk)
        idxs.append(ik)
        work = jnp.where(eidx == ik, -jnp.inf, work)
    ex = [jnp.exp(v - vals[0]) for v in vals]
    den = ex[0] + ex[1] + ex[2] + ex[3]
    topp_ref[...] = jnp.concatenate([e_ / den for e_ in ex], axis=0)
    topi_ref[...] = jnp.concatenate(idxs, axis=0)

    run = base_ref[...]
    ranks = []
    for k in range(TOP_K):
        hot = (eidx == idxs[k]).astype(F32)
        before = jnp.dot(hot.astype(BF16), tri_ref[...], preferred_element_type=F32)
        ranks.append(jnp.sum(hot * (run + before), axis=0, keepdims=True))
        run = run + jnp.sum(hot, axis=1, keepdims=True)
    rank_ref[...] = jnp.concatenate(ranks, axis=0).astype(I32)
    base_ref[...] = run
    counts_ref[...] = jnp.broadcast_to(run, counts_ref.shape).astype(I32)


def _mixer(x, qkv, rest, bias_tab, wpool_bd, pscale, dw, dwb, clng, clnb, wpw, bpw,
           wout, bout, lng, lnb, wr_t, br, tri, *, layer, seq, alpha):
    n = x.shape[0]
    nt = n // TQ
    tps = seq // TQ
    hb = TQ // HALO

    def prev_t(i):
        return jnp.where(i % tps == 0, i, i - 1)

    def next_t(i):
        return jnp.where(i % tps == tps - 1, i, i + 1)

    def variant(i):
        ib = i % tps
        return jnp.where(ib == 0, 0, jnp.where(ib == tps - 1, 2, 1))

    def const(shape):
        return pl.BlockSpec(shape, lambda i: tuple(0 for _ in shape))

    in_specs = [
        pl.BlockSpec((TQ, D_MODEL), lambda i: (i, 0)),
        pl.BlockSpec((TQ, NA_W), lambda i: (i, 0)),
        pl.BlockSpec((TQ, NA_W), lambda i: (prev_t(i), 1)),
        pl.BlockSpec((TQ, NA_W), lambda i: (i, 1)),
        pl.BlockSpec((TQ, NA_W), lambda i: (next_t(i), 1)),
        pl.BlockSpec((TQ, NA_W), lambda i: (prev_t(i), 2)),
        pl.BlockSpec((TQ, NA_W), lambda i: (i, 2)),
        pl.BlockSpec((TQ, NA_W), lambda i: (next_t(i), 2)),
        pl.BlockSpec((TQ, POOL_W + 2 * CONV_W), lambda i: (i, 0)),
        pl.BlockSpec((HALO, POOL_W + 2 * CONV_W), lambda i: (jnp.maximum(i * hb - 1, 0), 0)),
        pl.BlockSpec((HALO, POOL_W + 2 * CONV_W), lambda i: (jnp.minimum((i + 1) * hb, nt * hb - 1), 0)),
        pl.BlockSpec((None, NA_HEADS // 2, N_QGRP, 1, 2 * QGRP, NKEY),
                     lambda i: (layer, 0, 0, variant(i), 0, 0)),
        const((POOL_W, POOL_W)), const((1, POOL_W)),
        const((CONV_K, CONV_W)), const((1, CONV_W)), const((1, CONV_W)), const((1, CONV_W)),
        const((CONV_W, CONV_W)), const((1, CONV_W)),
        const((D_MODEL, D_MODEL)), const((1, D_MODEL)), const((1, D_MODEL)), const((1, D_MODEL)),
        const((N_EXPERTS, D_MODEL)), const((N_EXPERTS, 1)), const((TQ, TQ)),
    ]
    out_specs = [
        pl.BlockSpec((TQ, D_MODEL), lambda i: (i, 0)),
        pl.BlockSpec((TOP_K, TQ), lambda i: (0, i)),
        pl.BlockSpec((TOP_K, TQ), lambda i: (0, i)),
        pl.BlockSpec((TOP_K, TQ), lambda i: (0, i)),
        pl.BlockSpec((N_EXPERTS, LANES), lambda i: (0, 0)),
    ]
    out_shape = [
        jax.ShapeDtypeStruct((n, D_MODEL), F32),
        jax.ShapeDtypeStruct((TOP_K, n), I32),
        jax.ShapeDtypeStruct((TOP_K, n), F32),
        jax.ShapeDtypeStruct((TOP_K, n), I32),
        jax.ShapeDtypeStruct((N_EXPERTS, LANES), I32),
    ]
    return pl.pallas_call(
        functools.partial(_mixer_kernel, tps, alpha),
        grid=(nt,),
        in_specs=in_specs,
        out_specs=out_specs,
        out_shape=out_shape,
        scratch_shapes=[
            pltpu.VMEM((TQ + 2 * HALO, POOL_W + 2 * CONV_W), F32),
            pltpu.VMEM((SUBLANES - 1, TQ + 2 * HALO - SUBLANES, CONV_W), F32),
            pltpu.VMEM((N_EXPERTS, 1), F32),
        ],
        compiler_params=pltpu.CompilerParams(
            dimension_semantics=("arbitrary",), vmem_limit_bytes=VMEM_LIMIT),
        name="mixer",
    )(x, qkv, qkv, qkv, qkv, qkv, qkv, qkv, rest, rest, rest, bias_tab,
      wpool_bd, pscale, dw, dwb, clng, clnb, wpw, bpw, wout, bout, lng, lnb, wr_t, br, tri)


PAD_BLOCKS = (8, 16, 32, 64, 128, 256)
assert PAD_BLOCKS[0] == SUBLANES and TM_E == 2 * PAD_BLOCKS[-1]


def _dispatch_kernel(pad_lo_ref, pad_hi_ref, pos_ref, x_ref, xs_hbm, zero_ref, sem, pad_sem):
    i = pl.program_id(0)

    def pad_fill(act):
        for e in range(N_EXPERTS):
            lo = pad_lo_ref[e]
            hi = pad_hi_ref[e]
            head_end = jnp.minimum((lo + (SUBLANES - 1)) & -SUBLANES, hi)
            for j in range(SUBLANES - 1):
                @pl.when(lo + j < head_end)
                def _(j=j, lo=lo):
                    act(pltpu.make_async_copy(
                        zero_ref.at[pl.ds(0, 1), :], xs_hbm.at[pl.ds(lo + j, 1), :], pad_sem))
            a = head_end
            for b in PAD_BLOCKS:
                take = (a & b) != 0

                @pl.when(take)
                def _(a=a, b=b):
                    act(pltpu.make_async_copy(
                        zero_ref.at[pl.ds(0, b), :], xs_hbm.at[pl.ds(pl.multiple_of(a, SUBLANES), b), :], pad_sem))
                a = jnp.where(take, a + b, a)
        blk = PAD_BLOCKS[-1]

        def tail(c, carry):
            act(pltpu.make_async_copy(
                zero_ref, xs_hbm.at[pl.ds(pl.multiple_of(c * blk, blk), blk), :], pad_sem))
            return carry
        lax.fori_loop(pad_hi_ref[N_EXPERTS - 1] // blk, xs_hbm.shape[0] // blk, tail, 0)

    @pl.when(i == 0)
    def _():
        zero_ref[...] = jnp.zeros_like(zero_ref)
        pad_fill(lambda c: c.start())

    for k in range(TOP_K):
        for r in range(TQ):
            pltpu.make_async_copy(
                x_ref.at[pl.ds(r, 1), :], xs_hbm.at[pl.ds(pos_ref[k, r], 1), :], sem).start(priority=r % 2)
    for k in range(TOP_K):
        pltpu.make_async_copy(x_ref, xs_hbm.at[pl.ds(0, TQ), :], sem).wait()

    @pl.when(i == 0)
    def _():
        pad_fill(lambda c: c.wait())


def _dispatch(x1, pos, pad_lo, pad_hi, m_pad):
    n = x1.shape[0]
    grid_spec = pltpu.PrefetchScalarGridSpec(
        num_scalar_prefetch=2,
        grid=(n // TQ,),
        in_specs=[
            pl.BlockSpec((TOP_K, TQ), lambda i, *_: (0, i), memory_space=pltpu.SMEM),
            pl.BlockSpec((TQ, D_MODEL), lambda i, *_: (i, 0)),
        ],
        out_specs=pl.BlockSpec(memory_space=pl.ANY),
        scratch_shapes=[
            pltpu.VMEM((PAD_BLOCKS[-1], D_MODEL), F32),
            pltpu.SemaphoreType.DMA(()),
            pltpu.SemaphoreType.DMA(()),
        ],
    )
    return pl.pallas_call(
        _dispatch_kernel,
        grid_spec=grid_spec,
        out_shape=jax.ShapeDtypeStruct((m_pad, D_MODEL), F32),
        compiler_params=pltpu.CompilerParams(dimension_semantics=("arbitrary",)),
        name="dispatch",
    )(pad_lo, pad_hi, pos, x1)


def _experts_kernel(layer, te_ref, na_ref, first_ref, slot_ref, nxt_ref, rows_ref,
                    xs_ref, bg_ref, bu_ref, bd_ref, wg_hbm, wu_hbm, wd_hbm,
                    y_ref, wbuf, wg_bf, wu_bf, wd_bf, sem):
    t = pl.program_id(0)
    active = t < na_ref[0]
    e = te_ref[t]
    s = slot_ref[t]

    def fetch(expert, slot):
        return [pltpu.make_async_copy(w.at[layer, expert], wbuf.at[j, slot], sem.at[j, slot])
                for j, w in enumerate((wg_hbm, wu_hbm, wd_hbm))]

    @pl.when(t == 0)
    def _():
        for c in fetch(e, s):
            c.start()

    @pl.when(jnp.logical_and(active, first_ref[t] == 1))
    def _():
        for c in fetch(e, s):
            c.wait()

        @pl.when(nxt_ref[t] != e)
        def _():
            for c in fetch(nxt_ref[t], 1 - s):
                c.start()

        wg_bf[...] = wbuf[0, s].astype(BF16)
        wu_bf[...] = wbuf[1, s].astype(BF16)
        wd_bf[...] = wbuf[2, s].astype(BF16)

    for chunks in range(1, TM_E // TM_CHUNK + 1):
        m = chunks * TM_CHUNK

        @pl.when(jnp.logical_and(active, rows_ref[t] == chunks))
        def _(m=m):
            x = xs_ref[0:m, :].astype(BF16)
            g = jnp.minimum(jnp.dot(x, wg_bf[...], preferred_element_type=F32) + bg_ref[...], SWIGLU_LIMIT)
            u = jnp.clip(jnp.dot(x, wu_bf[...], preferred_element_type=F32) + bu_ref[...],
                         -SWIGLU_LIMIT, SWIGLU_LIMIT)
            act = (u + 1.0) * g * jax.nn.sigmoid(SWIGLU_ALPHA * g)
            y_ref[0:m, :] = jnp.dot(act.astype(BF16), wd_bf[...], preferred_element_type=F32) + bd_ref[...]
            if m < TM_E:
                y_ref[m:, :] = jnp.zeros((TM_E - m, D_MODEL), F32)

    @pl.when(jnp.logical_not(active))
    def _():
        y_ref[...] = jnp.zeros_like(y_ref)


def _experts(layer, tile_expert, n_active, tile_first, tile_slot, tile_next, tile_chunks,
             xs, wg, bg, wu, bu, wd, bd):
    m_pad = xs.shape[0]
    n_tiles = m_pad // TM_E

    def xmap(t, te, na, *_):
        return (jnp.minimum(t, na[0] - 1), 0)

    def bmap(t, te, *_):
        return (layer, te[t], 0, 0)

    grid_spec = pltpu.PrefetchScalarGridSpec(
        num_scalar_prefetch=6,
        grid=(n_tiles,),
        in_specs=[
            pl.BlockSpec((TM_E, D_MODEL), xmap),
            pl.BlockSpec((None, None, 1, D_MODEL), bmap),
            pl.BlockSpec((None, None, 1, D_MODEL), bmap),
            pl.BlockSpec((None, None, 1, D_MODEL), bmap),
            pl.BlockSpec(memory_space=pl.ANY),
            pl.BlockSpec(memory_space=pl.ANY),
            pl.BlockSpec(memory_space=pl.ANY),
        ],
        out_specs=pl.BlockSpec((TM_E, D_MODEL), lambda t, *_: (t, 0)),
        scratch_shapes=[
            pltpu.VMEM((3, 2, D_MODEL, D_MODEL), F32),
            pltpu.VMEM((D_MODEL, D_MODEL), BF16),
            pltpu.VMEM((D_MODEL, D_MODEL), BF16),
            pltpu.VMEM((D_MODEL, D_MODEL), BF16),
            pltpu.SemaphoreType.DMA((3, 2)),
        ],
    )
    return pl.pallas_call(
        functools.partial(_experts_kernel, layer),
        grid_spec=grid_spec,
        out_shape=jax.ShapeDtypeStruct((m_pad, D_MODEL), F32),
        compiler_params=pltpu.CompilerParams(
            dimension_semantics=("arbitrary",), vmem_limit_bytes=VMEM_LIMIT),
        name="experts",
    )(tile_expert, n_active, tile_first, tile_slot, tile_next, tile_chunks, xs, bg, bu, bd, wg, wu, wd)


def _combine_kernel(n_tiles, alpha, pos_ref, x1_ref, gates_ref, lng_ref, lnb_ref, ys_hbm, out_ref, buf, sem):
    j = pl.program_id(0)

    for slot in range(2):
        @pl.when(jnp.logical_and(j < n_tiles, j % 2 == slot))
        def _(slot=slot):
            for k in range(TOP_K):
                for r in range(TQ):
                    pltpu.make_async_copy(
                        ys_hbm.at[pl.ds(pos_ref[k, r], 1), :], buf.at[slot, k, pl.ds(r, 1), :],
                        sem.at[slot]).start(priority=r % 2)

    for slot in range(2):
        @pl.when(jnp.logical_and(j > 0, (j - 1) % 2 == slot))
        def _(slot=slot):
            for k in range(TOP_K):
                pltpu.make_async_copy(ys_hbm.at[pl.ds(0, TQ), :], buf.at[slot, k], sem.at[slot]).wait()
            gates = gates_ref[...]
            ffn = buf[slot, 0] * gates[:, 0:1]
            for k in range(1, TOP_K):
                ffn = ffn + buf[slot, k] * gates[:, k:k + 1]
            out_ref[...] = _layer_norm(alpha * x1_ref[...] + ffn, lng_ref[...], lnb_ref[...])


def _combine(pos, x1, gates_tm, lng, lnb, ys, *, alpha):
    n = x1.shape[0]
    n_tiles = n // TQ
    req = lambda j: jnp.minimum(j, n_tiles - 1)
    fin = lambda j: jnp.maximum(j - 1, 0)
    return pl.pallas_call(
        functools.partial(_combine_kernel, n_tiles, alpha),
        grid=(n_tiles + 1,),
        in_specs=[
            pl.BlockSpec((TOP_K, TQ), lambda j: (0, req(j)), memory_space=pltpu.SMEM),
            pl.BlockSpec((TQ, D_MODEL), lambda j: (fin(j), 0)),
            pl.BlockSpec((TQ, TOP_K), lambda j: (fin(j), 0)),
            pl.BlockSpec((1, D_MODEL), lambda j: (0, 0)),
            pl.BlockSpec((1, D_MODEL), lambda j: (0, 0)),
            pl.BlockSpec(memory_space=pl.ANY),
        ],
        out_specs=pl.BlockSpec((TQ, D_MODEL), lambda j: (fin(j), 0)),
        out_shape=jax.ShapeDtypeStruct((n, D_MODEL), F32),
        scratch_shapes=[pltpu.VMEM((2, TOP_K, TQ, D_MODEL), F32), pltpu.SemaphoreType.DMA((2,))],
        compiler_params=pltpu.CompilerParams(
            dimension_semantics=("arbitrary",), vmem_limit_bytes=VMEM_LIMIT),
        name="combine",
    )(pos, x1, gates_tm, lng, lnb, ys)


def kernel(x, w_in, b_in, w_pool, pool_scale, rpb, conv_dw, conv_dw_b, conv_ln_g, conv_ln_b,
           w_conv_pw, b_conv_pw, w_out, b_out, ln1_g, ln1_b, w_router, b_router,
           w_gate, b_gate, w_up, b_up, w_down, b_down, ln2_g, ln2_b):
    batch, seq, d = x.shape
    depth = w_in.shape[0]
    n = batch * seq
    rows = seq // GRID_W
    alpha = (2.0 * depth) ** 0.25
    off_q, off_k, off_v = POOL_W, POOL_W + NA_W, POOL_W + 2 * NA_W
    off_ca = off_v + NA_W
    m_pad = n * TOP_K + N_EXPERTS * TM_E
    n_tiles = m_pad // TM_E

    tri = (np.arange(TQ)[:, None] < np.arange(TQ)[None, :]).astype(np.float32)
    tri = jnp.asarray(tri, BF16)
    row2 = lambda v: v.reshape(1, -1)
    bias_tabs = _attn_bias_tables(rpb, rows)
    experts_iota = jnp.arange(N_EXPERTS, dtype=I32)
    tile_starts = jnp.arange(n_tiles, dtype=I32) * TM_E
    b_gate4 = b_gate.reshape(depth, N_EXPERTS, 1, d)
    b_up4 = b_up.reshape(depth, N_EXPERTS, 1, d)
    b_down4 = b_down.reshape(depth, N_EXPERTS, 1, d)

    h = x.reshape(n, d)
    for l in range(depth):
        wqkv = w_in[l][:, off_q:off_ca].astype(BF16)
        bqkv = row2(b_in[l][off_q:off_ca])
        wrest = jnp.concatenate([w_in[l][:, :off_q], w_in[l][:, off_ca:]], axis=1).astype(BF16)
        brest = row2(jnp.concatenate([b_in[l][:off_q], b_in[l][off_ca:]]))
        qkv, rest = _inproj(h, wqkv, bqkv, wrest, brest)

        wpool_bd = jax.scipy.linalg.block_diag(*[w_pool[l][g] for g in range(len(POOL_WINDOWS))]).astype(BF16)
        x1, top_i, top_p, rank, counts = _mixer(
            h, qkv, rest, bias_tabs, wpool_bd, row2(pool_scale[l]),
            conv_dw[l].reshape(CONV_K, CONV_W), row2(conv_dw_b[l]), row2(conv_ln_g[l]), row2(conv_ln_b[l]),
            w_conv_pw[l].astype(BF16), row2(b_conv_pw[l]),
            w_out[l].astype(BF16), row2(b_out[l]), row2(ln1_g[l]), row2(ln1_b[l]),
            w_router[l].T.astype(BF16), b_router[l].reshape(N_EXPERTS, 1), tri,
            layer=l, seq=seq, alpha=alpha)

        cnt = counts[:, 0]
        cpad = ((cnt + TM_E - 1) // TM_E) * TM_E
        ends = jnp.cumsum(cpad)
        off = ends - cpad
        hot = top_i[None] == experts_iota[:, None, None]
        pos = rank + jnp.sum(jnp.where(hot, off[:, None, None], 0), axis=0)
        n_active = (ends[-1] // TM_E).astype(I32)
        last_start = jnp.minimum(tile_starts, ends[-1] - TM_E)
        tile_expert = jnp.sum((ends[None, :] <= last_start[:, None]).astype(I32), axis=1)
        owns = cnt > 0
        e_slot = (jnp.cumsum(owns.astype(I32)) - 1) % 2
        later = jnp.logical_and(experts_iota[None, :] > experts_iota[:, None], owns[None, :])
        e_next = jnp.min(jnp.where(later, experts_iota[None, :], N_EXPERTS), axis=1)
        e_next = jnp.where(e_next == N_EXPERTS, experts_iota, e_next)
        tile_hot = tile_expert[:, None] == experts_iota[None, :]
        pick = lambda v: jnp.sum(jnp.where(tile_hot, v[None, :], 0), axis=1).astype(I32)
        tile_first = (pick(off) == tile_starts).astype(I32)
        tile_slot = pick(e_slot)
        tile_next = pick(e_next)

        xs = _dispatch(x1, pos, (off + cnt).astype(I32), ends.astype(I32), m_pad)
        real_rows = jnp.clip(pick(off + cnt) - tile_starts, 0, TM_E)
        tile_chunks = (real_rows + TM_CHUNK - 1) // TM_CHUNK
        ys = _experts(l, tile_expert, n_active.reshape(1), tile_first, tile_slot, tile_next, tile_chunks, xs,
                      w_gate, b_gate4, w_up, b_up4, w_down, b_down4)
        h = _combine(pos, x1, top_p.T, row2(ln2_g[l]), row2(ln2_b[l]), ys, alpha=alpha)
    return h.reshape(batch, seq, d)
```

```python
import functools

import jax
import jax.numpy as jnp
import numpy as np
from jax import lax
from jax.experimental import pallas as pl
from jax.experimental.pallas import tpu as pltpu

F32 = jnp.float32
BF16 = jnp.bfloat16
I32 = jnp.int32

D_MODEL = 1024
GRID_W = 64
POOL_W = 256
POOL_WINDOWS = (2, 4, 8, 16)
POOL_GROUP_W = 64
NA_W = 512
NA_HEADS = 8
NA_HEAD_DIM = 64
NA_ROWS = 8
NA_COLS = 16
CONV_W = 256
CONV_K = 31
N_EXPERTS = 32
TOP_K = 4
SWIGLU_ALPHA = 1.702
SWIGLU_LIMIT = 7.0
LN_EPS = 1e-5
NEG_INF = -1e30

LANES = 128
TM_IN = 512
TQ = 256
TQ_ROWS = TQ // GRID_W
HALO = 16
SUBLANES = 8
N_CBLK = GRID_W // SUBLANES
CBLK_ROWS = TQ_ROWS * SUBLANES
N_QGRP = GRID_W // NA_COLS
QGRP = TQ // N_QGRP
KBAND_COLS = 2 * NA_COLS
KBAND = TQ_ROWS * KBAND_COLS
NKEY = 3 * KBAND
TM_E = 512
TM_CHUNK = 128
VMEM_LIMIT = 56 * 1024 * 1024


def _layer_norm(v, g, b):
    mu = jnp.mean(v, axis=-1, keepdims=True)
    c = v - mu
    var = jnp.mean(c * c, axis=-1, keepdims=True)
    return c * lax.rsqrt(var + LN_EPS) * g + b


def _band_start_cblk(g):
    return min(max(2 * g - 1, 0), N_CBLK - KBAND_COLS // SUBLANES)


def _stored_row_groups():
    return [r * N_CBLK + cb for cb in range(N_CBLK) for r in range(TQ_ROWS)]


def _inproj_kernel(x_ref, wqkv_ref, bqkv_ref, wrest_ref, brest_ref, qkv_ref, rest_ref):
    x = x_ref[...].astype(BF16)
    rest_ref[...] = jnp.dot(x, wrest_ref[...], preferred_element_type=F32) + brest_ref[...]
    groups = [x_ref[pl.ds(tile * TQ + gn * SUBLANES, SUBLANES), :]
              for tile in range(TM_IN // TQ) for gn in _stored_row_groups()]
    xp = jnp.concatenate(groups, axis=0).astype(BF16)
    qkv = jnp.dot(xp, wqkv_ref[...], preferred_element_type=F32) + bqkv_ref[...]
    qkv_ref[...] = qkv.astype(BF16)


def _inproj(x, wqkv, bqkv, wrest, brest):
    n = x.shape[0]
    return pl.pallas_call(
        _inproj_kernel,
        grid=(n // TM_IN,),
        in_specs=[
            pl.BlockSpec((TM_IN, D_MODEL), lambda i: (i, 0)),
            pl.BlockSpec((D_MODEL, 3 * NA_W), lambda i: (0, 0)),
            pl.BlockSpec((1, 3 * NA_W), lambda i: (0, 0)),
            pl.BlockSpec((D_MODEL, POOL_W + 2 * CONV_W), lambda i: (0, 0)),
            pl.BlockSpec((1, POOL_W + 2 * CONV_W), lambda i: (0, 0)),
        ],
        out_specs=[
            pl.BlockSpec((TM_IN, 3 * NA_W), lambda i: (i, 0)),
            pl.BlockSpec((TM_IN, POOL_W + 2 * CONV_W), lambda i: (i, 0)),
        ],
        out_shape=[
            jax.ShapeDtypeStruct((n, 3 * NA_W), BF16),
            jax.ShapeDtypeStruct((n, POOL_W + 2 * CONV_W), F32),
        ],
        compiler_params=pltpu.CompilerParams(
            dimension_semantics=("arbitrary",), vmem_limit_bytes=VMEM_LIMIT),
        name="inproj",
    )(x, wqkv, bqkv, wrest, brest)


def _attn_bias_tables(rpb, rows):
    kr_win = min(NA_ROWS, rows)
    n_tiles = rows // TQ_ROWS
    n_r, n_c = 2 * NA_ROWS - 1, 2 * NA_COLS - 1
    n_layers = rpb.shape[0]
    c_hots, c_oks = [], []
    for g in range(N_QGRP):
        qc = g * NA_COLS + np.arange(NA_COLS)
        kc = _band_start_cblk(g) * SUBLANES + np.arange(KBAND_COLS)
        sc = np.clip(qc - NA_COLS // 2, 0, GRID_W - NA_COLS)[:, None]
        c_oks.append((kc[None] >= sc) & (kc[None] < sc + NA_COLS))
        c_off = np.clip(kc[None] - qc[:, None] + NA_COLS - 1, 0, n_c - 1)
        c_hots.append((c_off[..., None] == np.arange(n_c)).astype(np.float32))
    r_hots, r_oks = [], []
    for tile in (0, 1, n_tiles - 1):
        qr = tile * TQ_ROWS + np.arange(TQ_ROWS)
        slot_tile = np.array([tile - 1, tile, tile + 1])
        kr = slot_tile[:, None] * TQ_ROWS + np.arange(TQ_ROWS)[None, :]
        slot_ok = ((slot_tile >= 0) & (slot_tile < n_tiles))[:, None]
        sr = np.clip(qr - kr_win // 2, 0, rows - kr_win)[:, None, None]
        r_oks.append((kr[None] >= sr) & (kr[None] < sr + kr_win) & slot_ok[None])
        r_off = np.clip(kr[None] - qr[:, None, None] + NA_ROWS - 1, 0, n_r - 1)
        r_hots.append((r_off[..., None] == np.arange(n_r)).astype(np.float32))
    c_hot = jnp.asarray(np.stack(c_hots))
    r_off = np.stack([np.argmax(h, axis=-1) for h in r_hots])
    c_ok = np.stack(c_oks).reshape(N_QGRP, 2, SUBLANES, KBAND_COLS // SUBLANES, SUBLANES)
    r_ok = np.stack(r_oks)
    ok = (r_ok[:, None, None, :, None, :, None, :, None]
          & c_ok[None, :, :, None, :, None, :, None, :])
    ok = np.broadcast_to(ok[:, :, None], (3, N_QGRP, 2) + ok.shape[2:])
    ok = jnp.asarray(ok.reshape(3, N_QGRP, 2 * QGRP, NKEY).astype(np.float32))
    rpb5 = rpb.astype(F32).reshape(n_layers, NA_HEADS // 2, 2, n_r, n_c)
    by_col = jnp.einsum("lpjrc,gmnc->lpgjmrn", rpb5, c_hot, precision=lax.Precision.HIGHEST)
    by_col = by_col.reshape(n_layers * (NA_HEADS // 2) * N_QGRP, 2 * NA_COLS, n_r * KBAND_COLS)
    nb = KBAND_COLS // SUBLANES
    expand = np.zeros((TQ_ROWS, 2, 2, TQ_ROWS, SUBLANES, 2, 2, SUBLANES), np.float32)
    for q in range(TQ_ROWS):
        for j in range(2):
            for a in range(2):
                for x in range(SUBLANES):
                    expand[q, j, a, q, x, j, a, x] = 1.0
    expand = expand.reshape(TQ_ROWS, 2 * QGRP, 2 * NA_COLS)
    select = np.zeros((3, TQ_ROWS, n_r, nb, SUBLANES, 3, nb, TQ_ROWS, SUBLANES), np.float32)
    for v in range(3):
        for q in range(TQ_ROWS):
            for s in range(3):
                for y in range(TQ_ROWS):
                    for b in range(nb):
                        for z in range(SUBLANES):
                            select[v, q, r_off[v, q, s, y], b, z, s, b, y, z] = 1.0
    select = select.reshape(3, TQ_ROWS, n_r * KBAND_COLS, NKEY)
    lane_pad = -(n_r * KBAND_COLS) % LANES
    by_col = jnp.pad(by_col, ((0, 0), (0, 0), (0, lane_pad)))
    select = np.pad(select, ((0, 0), (0, 0), (0, lane_pad), (0, 0)))
    tab = _bias_table_call(by_col, jnp.asarray(expand, BF16), jnp.asarray(select, BF16), ok)
    return tab.reshape(n_layers, NA_HEADS // 2, N_QGRP, 3, 2 * QGRP, NKEY)


def _bias_table_kernel(u_ref, expand_ref, select_ref, ok_ref, out_ref):
    u = u_ref[...]
    parts = []
    for _ in range(3):
        piece = u.astype(BF16)
        parts.append(piece)
        u = u - piece.astype(F32)
    n_k = u.shape[1]
    pieces = jnp.concatenate(parts, axis=1)
    acc = [jnp.zeros((2 * QGRP, NKEY), F32) for _ in range(3)]
    for q in range(TQ_ROWS):
        rows = jnp.dot(expand_ref[q], pieces, preferred_element_type=F32).astype(BF16)
        rows = jnp.concatenate([rows[:, k * n_k:(k + 1) * n_k] for k in range(3)], axis=0)
        for v in range(3):
            sel = jnp.dot(rows, select_ref[v, q], preferred_element_type=F32)
            acc[v] = acc[v] + (sel[0:2 * QGRP] + sel[2 * QGRP:4 * QGRP] + sel[4 * QGRP:])
    for v in range(3):
        out_ref[v] = jnp.where(ok_ref[v, 0] > 0.0, acc[v], NEG_INF)


def _bias_table_call(by_col, expand, select, ok):
    n, n_u, n_k = by_col.shape
    return pl.pallas_call(
        _bias_table_kernel,
        grid=(n,),
        in_specs=[
            pl.BlockSpec((None, n_u, n_k), lambda i: (i, 0, 0)),
            pl.BlockSpec(expand.shape, lambda i: (0, 0, 0)),
            pl.BlockSpec(select.shape, lambda i: (0, 0, 0, 0)),
            pl.BlockSpec((3, 1, 2 * QGRP, NKEY), lambda i: (0, i % N_QGRP, 0, 0)),
        ],
        out_specs=pl.BlockSpec((None, 3, 2 * QGRP, NKEY), lambda i: (i, 0, 0, 0)),
        out_shape=jax.ShapeDtypeStruct((n, 3, 2 * QGRP, NKEY), F32),
        compiler_params=pltpu.CompilerParams(
            dimension_semantics=("arbitrary",), vmem_limit_bytes=VMEM_LIMIT),
        name="bias_table",
    )(by_col, expand, select, ok)


def _mixer_kernel(tiles_per_seq, alpha,
                  x_ref, q_ref, kp_ref, kc_ref, kn_ref, vp_ref, vc_ref, vn_ref,
                  rc_ref, rp_ref, rn_ref, bias_ref,
                  wpool_ref, pscale_ref, dw_ref, dwb_ref, clng_ref, clnb_ref, wpw_ref, bpw_ref,
                  wout_ref, bout_ref, lng_ref, lnb_ref, wr_ref, br_ref, tri_ref,
                  x1_ref, topi_ref, topp_ref, rank_ref, counts_ref,
                  halo_ref, shift_ref, base_ref):
    i = pl.program_id(0)
    ib = i % tiles_per_seq
    has_prev = ib > 0
    has_next = ib < tiles_per_seq - 1

    @pl.when(i == 0)
    def _():
        base_ref[...] = jnp.zeros_like(base_ref)

    halo_ref[0:HALO, :] = jnp.where(has_prev, rp_ref[...], 0.0)
    halo_ref[HALO:HALO + TQ, :] = rc_ref[...]
    halo_ref[HALO + TQ:, :] = jnp.where(has_next, rn_ref[...], 0.0)

    def fill_shifts():
        for b in range(1, SUBLANES):
            shift_ref[b - 1] = halo_ref[pl.ds(b, TQ + 2 * HALO - SUBLANES), 0:CONV_W]

    def window(start):
        a8, b = divmod(start, SUBLANES)
        if b == 0:
            return halo_ref[pl.ds(a8 * SUBLANES, TQ), 0:CONV_W]
        return shift_ref[b - 1, pl.ds(a8 * SUBLANES, TQ), :]

    fill_shifts()

    def u_at(off):
        return window(HALO + off)

    t_seq = ib * TQ + lax.broadcasted_iota(I32, (TQ, 1), 0)
    seq_len = tiles_per_seq * TQ
    group = lax.broadcasted_iota(I32, (1, POOL_W), 1) // POOL_GROUP_W
    u0 = u_at(0)
    acc = u0
    mean = jnp.zeros((TQ, POOL_W), F32)
    done = 0
    for g, w in enumerate(POOL_WINDOWS):
        half = w // 2
        for o in range(done + 1, half + 1):
            acc = acc + u_at(-o) + (u_at(o - 1) if o > 1 else 0.0)
        done = half
        cnt = (jnp.minimum(t_seq + half, seq_len) - jnp.maximum(t_seq - half, 0)).astype(F32)
        mean = jnp.where(group == g, acc * (1.0 / cnt), mean)
    d = (mean - u0).astype(BF16)
    y_pool = jnp.dot(d, wpool_ref[...], preferred_element_type=F32) * pscale_ref[...]
    mix = jnp.dot(y_pool.astype(BF16), wout_ref[0:POOL_W, :], preferred_element_type=F32)

    a = halo_ref[:, POOL_W:POOL_W + CONV_W]
    gate = halo_ref[:, POOL_W + CONV_W:]
    halo_ref[:, 0:CONV_W] = a * jax.nn.sigmoid(gate)
    fill_shifts()
    conv = jnp.zeros((TQ, CONV_W), F32) + dwb_ref[...]
    for k in range(CONV_K):
        conv = conv + window(HALO - CONV_K // 2 + k) * dw_ref[k:k + 1, :]
    hc = _layer_norm(conv, clng_ref[...], clnb_ref[...])
    hc = hc * jax.nn.sigmoid(hc)
    y_conv = jnp.dot(hc.astype(BF16), wpw_ref[...], preferred_element_type=F32) + bpw_ref[...]
    mix = mix + jnp.dot(y_conv.astype(BF16), wout_ref[POOL_W + NA_W:, :], preferred_element_type=F32)

    low = lax.broadcasted_iota(I32, (QGRP, LANES), 1) < NA_HEAD_DIM
    scale = NA_HEAD_DIM ** -0.5
    n_pairs = NA_HEADS // 2
    bands = [pl.ds(_band_start_cblk(g) * CBLK_ROWS, KBAND) for g in range(N_QGRP)]
    col = lambda p: slice(p * LANES, (p + 1) * LANES)
    s_blk = []
    for p in range(n_pairs):
        for g in range(N_QGRP):
            qp = q_ref[pl.ds(g * QGRP, QGRP), col(p)].astype(F32) * scale
            qs = jnp.concatenate([jnp.where(low, qp, 0.0), jnp.where(low, 0.0, qp)], axis=0).astype(BF16)
            kk = jnp.concatenate([kp_ref[bands[g], col(p)], kc_ref[bands[g], col(p)], kn_ref[bands[g], col(p)]],
                                 axis=0)
            s_blk.append(lax.dot_general(qs, kk, (((1,), (1,)), ((), ())), preferred_element_type=F32))
    blk_rows = 2 * QGRP
    s = jnp.concatenate(s_blk, axis=0) + bias_ref[...].reshape(n_pairs * N_QGRP * blk_rows, NKEY)
    m = jnp.max(s, axis=-1, keepdims=True)
    e = jnp.exp(s - m)
    rl = 1.0 / jnp.sum(e, axis=-1, keepdims=True)
    e = e.astype(BF16)
    for p in range(n_pairs):
        o_grp = []
        for g in range(N_QGRP):
            vv = jnp.concatenate([vp_ref[bands[g], col(p)], vc_ref[bands[g], col(p)], vn_ref[bands[g], col(p)]],
                                 axis=0)
            rows = slice((p * N_QGRP + g) * blk_rows, (p * N_QGRP + g + 1) * blk_rows)
            o = jnp.dot(e[rows], vv, preferred_element_type=F32) * rl[rows]
            o_grp.append(jnp.where(low, o[0:QGRP], o[QGRP:]))
        o_pair = jnp.concatenate(
            [o_grp[cb // 2][(cb % 2) * CBLK_ROWS + r * SUBLANES:(cb % 2) * CBLK_ROWS + (r + 1) * SUBLANES]
             for r in range(TQ_ROWS) for cb in range(N_CBLK)], axis=0)
        r0 = POOL_W + p * LANES
        mix = mix + jnp.dot(o_pair.astype(BF16), wout_ref[r0:r0 + LANES, :], preferred_element_type=F32)

    x1 = _layer_norm(alpha * x_ref[...] + mix + bout_ref[...], lng_ref[...], lnb_ref[...])
    x1_ref[...] = x1

    lt = lax.dot_general(wr_ref[...], x1.astype(BF16), (((1,), (1,)), ((), ())),
                         preferred_element_type=F32) + br_ref[...]
    eidx = lax.broadcasted_iota(I32, (N_EXPERTS, TQ), 0)
    work = lt
    vals, idxs = [], []
    for _ in range(TOP_K):
        mk = jnp.max(work, axis=0, keepdims=True)
        ik = jnp.min(jnp.where(work == mk, eidx, N_EXPERTS), axis=0, keepdims=True)
        vals.append(mk)
        idxs.append(ik)
        work = jnp.where(eidx == ik, -jnp.inf, work)
    ex = [jnp.exp(v - vals[0]) for v in vals]
    den = ex[0] + ex[1] + ex[2] + ex[3]
    topp_ref[...] = jnp.concatenate([e_ / den for e_ in ex], axis=0)
    topi_ref[...] = jnp.concatenate(idxs, axis=0)

    run = base_ref[...]
    ranks = []
    for k in range(TOP_K):
        hot = (eidx == idxs[k]).astype(F32)
        before = jnp.dot(hot.astype(BF16), tri_ref[...], preferred_element_type=F32)
        ranks.append(jnp.sum(hot * (run + before), axis=0, keepdims=True))
        run = run + jnp.sum(hot, axis=1, keepdims=True)
    rank_ref[...] = jnp.concatenate(ranks, axis=0).astype(I32)
    base_ref[...] = run
    counts_ref[...] = jnp.broadcast_to(run, counts_ref.shape).astype(I32)


def _mixer(x, qkv, rest, bias_tab, wpool_bd, pscale, dw, dwb, clng, clnb, wpw, bpw,
           wout, bout, lng, lnb, wr_t, br, tri, *, layer, seq, alpha):
    n = x.shape[0]
    nt = n // TQ
    tps = seq // TQ
    hb = TQ // HALO

    def prev_t(i):
        return jnp.where(i % tps == 0, i, i - 1)

    def next_t(i):
        return jnp.where(i % tps == tps - 1, i, i + 1)

    def variant(i):
        ib = i % tps
        return jnp.where(ib == 0, 0, jnp.where(ib == tps - 1, 2, 1))

    def const(shape):
        return pl.BlockSpec(shape, lambda i: tuple(0 for _ in shape))

    in_specs = [
        pl.BlockSpec((TQ, D_MODEL), lambda i: (i, 0)),
        pl.BlockSpec((TQ, NA_W), lambda i: (i, 0)),
        pl.BlockSpec((TQ, NA_W), lambda i: (prev_t(i), 1)),
        pl.BlockSpec((TQ, NA_W), lambda i: (i, 1)),
        pl.BlockSpec((TQ, NA_W), lambda i: (next_t(i), 1)),
        pl.BlockSpec((TQ, NA_W), lambda i: (prev_t(i), 2)),
        pl.BlockSpec((TQ, NA_W), lambda i: (i, 2)),
        pl.BlockSpec((TQ, NA_W), lambda i: (next_t(i), 2)),
        pl.BlockSpec((TQ, POOL_W + 2 * CONV_W), lambda i: (i, 0)),
        pl.BlockSpec((HALO, POOL_W + 2 * CONV_W), lambda i: (jnp.maximum(i * hb - 1, 0), 0)),
        pl.BlockSpec((HALO, POOL_W + 2 * CONV_W), lambda i: (jnp.minimum((i + 1) * hb, nt * hb - 1), 0)),
        pl.BlockSpec((None, NA_HEADS // 2, N_QGRP, 1, 2 * QGRP, NKEY),
                     lambda i: (layer, 0, 0, variant(i), 0, 0)),
        const((POOL_W, POOL_W)), const((1, POOL_W)),
        const((CONV_K, CONV_W)), const((1, CONV_W)), const((1, CONV_W)), const((1, CONV_W)),
        const((CONV_W, CONV_W)), const((1, CONV_W)),
        const((D_MODEL, D_MODEL)), const((1, D_MODEL)), const((1, D_MODEL)), const((1, D_MODEL)),
        const((N_EXPERTS, D_MODEL)), const((N_EXPERTS, 1)), const((TQ, TQ)),
    ]
    out_specs = [
        pl.BlockSpec((TQ, D_MODEL), lambda i: (i, 0)),
        pl.BlockSpec((TOP_K, TQ), lambda i: (0, i)),
        pl.BlockSpec((TOP_K, TQ), lambda i: (0, i)),
        pl.BlockSpec((TOP_K, TQ), lambda i: (0, i)),
        pl.BlockSpec((N_EXPERTS, LANES), lambda i: (0, 0)),
    ]
    out_shape = [
        jax.ShapeDtypeStruct((n, D_MODEL), F32),
        jax.ShapeDtypeStruct((TOP_K, n), I32),
        jax.ShapeDtypeStruct((TOP_K, n), F32),
        jax.ShapeDtypeStruct((TOP_K, n), I32),
        jax.ShapeDtypeStruct((N_EXPERTS, LANES), I32),
    ]
    return pl.pallas_call(
        functools.partial(_mixer_kernel, tps, alpha),
        grid=(nt,),
        in_specs=in_specs,
        out_specs=out_specs,
        out_shape=out_shape,
        scratch_shapes=[
            pltpu.VMEM((TQ + 2 * HALO, POOL_W + 2 * CONV_W), F32),
            pltpu.VMEM((SUBLANES - 1, TQ + 2 * HALO - SUBLANES, CONV_W), F32),
            pltpu.VMEM((N_EXPERTS, 1), F32),
        ],
        compiler_params=pltpu.CompilerParams(
            dimension_semantics=("arbitrary",), vmem_limit_bytes=VMEM_LIMIT),
        name="mixer",
    )(x, qkv, qkv, qkv, qkv, qkv, qkv, qkv, rest, rest, rest, bias_tab,
      wpool_bd, pscale, dw, dwb, clng, clnb, wpw, bpw, wout, bout, lng, lnb, wr_t, br, tri)


PAD_BLOCKS = (8, 16, 32, 64, 128, 256)
assert PAD_BLOCKS[0] == SUBLANES and TM_E == 2 * PAD_BLOCKS[-1]


def _dispatch_kernel(n_tiles, pad_lo_ref, pad_hi_ref, pos_ref, x_ref, xs_hbm, zero_ref, ring, sem, pad_sem):
    i = pl.program_id(0)

    def pad_fill(act):
        for e in range(N_EXPERTS):
            lo = pad_lo_ref[e]
            hi = pad_hi_ref[e]
            head_end = jnp.minimum((lo + (SUBLANES - 1)) & -SUBLANES, hi)
            for j in range(SUBLANES - 1):
                @pl.when(lo + j < head_end)
                def _(j=j, lo=lo):
                    act(pltpu.make_async_copy(
                        zero_ref.at[pl.ds(0, 1), :], xs_hbm.at[pl.ds(lo + j, 1), :], pad_sem))
            a = head_end
            for b in PAD_BLOCKS:
                take = (a & b) != 0

                @pl.when(take)
                def _(a=a, b=b):
                    act(pltpu.make_async_copy(
                        zero_ref.at[pl.ds(0, b), :], xs_hbm.at[pl.ds(pl.multiple_of(a, SUBLANES), b), :], pad_sem))
                a = jnp.where(take, a + b, a)
        blk = PAD_BLOCKS[-1]

        def tail(c, carry):
            act(pltpu.make_async_copy(
                zero_ref, xs_hbm.at[pl.ds(pl.multiple_of(c * blk, blk), blk), :], pad_sem))
            return carry
        lax.fori_loop(pad_hi_ref[N_EXPERTS - 1] // blk, xs_hbm.shape[0] // blk, tail, 0)

    @pl.when(i == 0)
    def _():
        zero_ref[...] = jnp.zeros_like(zero_ref)
        pad_fill(lambda c: c.start())

    def wait_rows(slot):
        for k in range(TOP_K):
            pltpu.make_async_copy(ring.at[slot], xs_hbm.at[pl.ds(0, TQ), :], sem.at[slot]).wait()

    for slot in range(2):
        @pl.when(i % 2 == slot)
        def _(slot=slot):
            ring[slot] = x_ref[...]
            for k in range(TOP_K):
                for r in range(TQ):
                    pltpu.make_async_copy(
                        ring.at[slot, pl.ds(r, 1), :], xs_hbm.at[pl.ds(pos_ref[k, r], 1), :],
                        sem.at[slot]).start(priority=r % 2)

            @pl.when(i > 0)
            def _():
                wait_rows(1 - slot)

            @pl.when(i == n_tiles - 1)
            def _():
                wait_rows(slot)

    @pl.when(i == 0)
    def _():
        pad_fill(lambda c: c.wait())


def _dispatch(x1, pos, pad_lo, pad_hi, m_pad):
    n = x1.shape[0]
    grid_spec = pltpu.PrefetchScalarGridSpec(
        num_scalar_prefetch=2,
        grid=(n // TQ,),
        in_specs=[
            pl.BlockSpec((TOP_K, TQ), lambda i, *_: (0, i), memory_space=pltpu.SMEM),
            pl.BlockSpec((TQ, D_MODEL), lambda i, *_: (i, 0)),
        ],
        out_specs=pl.BlockSpec(memory_space=pl.ANY),
        scratch_shapes=[
            pltpu.VMEM((PAD_BLOCKS[-1], D_MODEL), F32),
            pltpu.VMEM((2, TQ, D_MODEL), F32),
            pltpu.SemaphoreType.DMA((2,)),
            pltpu.SemaphoreType.DMA(()),
        ],
    )
    return pl.pallas_call(
        functools.partial(_dispatch_kernel, n // TQ),
        grid_spec=grid_spec,
        out_shape=jax.ShapeDtypeStruct((m_pad, D_MODEL), F32),
        compiler_params=pltpu.CompilerParams(dimension_semantics=("arbitrary",)),
        name="dispatch",
    )(pad_lo, pad_hi, pos, x1)


def _experts_kernel(layer, te_ref, na_ref, first_ref, slot_ref, nxt_ref, rows_ref,
                    xs_ref, bg_ref, bu_ref, bd_ref, wg_hbm, wu_hbm, wd_hbm,
                    y_ref, wbuf, wg_bf, wu_bf, wd_bf, sem):
    t = pl.program_id(0)
    active = t < na_ref[0]
    e = te_ref[t]
    s = slot_ref[t]

    def fetch(expert, slot):
        return [pltpu.make_async_copy(w.at[layer, expert], wbuf.at[j, slot], sem.at[j, slot])
                for j, w in enumerate((wg_hbm, wu_hbm, wd_hbm))]

    @pl.when(t == 0)
    def _():
        for c in fetch(e, s):
            c.start()

    @pl.when(jnp.logical_and(active, first_ref[t] == 1))
    def _():
        for c in fetch(e, s):
            c.wait()

        @pl.when(nxt_ref[t] != e)
        def _():
            for c in fetch(nxt_ref[t], 1 - s):
                c.start()

        wg_bf[...] = wbuf[0, s].astype(BF16)
        wu_bf[...] = wbuf[1, s].astype(BF16)
        wd_bf[...] = wbuf[2, s].astype(BF16)

    for chunks in range(1, TM_E // TM_CHUNK + 1):
        m = chunks * TM_CHUNK

        @pl.when(jnp.logical_and(active, rows_ref[t] == chunks))
        def _(m=m):
            x = xs_ref[0:m, :].astype(BF16)
            g = jnp.minimum(jnp.dot(x, wg_bf[...], preferred_element_type=F32) + bg_ref[...], SWIGLU_LIMIT)
            u = jnp.clip(jnp.dot(x, wu_bf[...], preferred_element_type=F32) + bu_ref[...],
                         -SWIGLU_LIMIT, SWIGLU_LIMIT)
            act = (u + 1.0) * g * jax.nn.sigmoid(SWIGLU_ALPHA * g)
            y_ref[0:m, :] = jnp.dot(act.astype(BF16), wd_bf[...], preferred_element_type=F32) + bd_ref[...]
            if m < TM_E:
                y_ref[m:, :] = jnp.zeros((TM_E - m, D_MODEL), F32)

    @pl.when(jnp.logical_not(active))
    def _():
        y_ref[...] = jnp.zeros_like(y_ref)


def _experts(layer, tile_expert, n_active, tile_first, tile_slot, tile_next, tile_chunks,
             xs, wg, bg, wu, bu, wd, bd):
    m_pad = xs.shape[0]
    n_tiles = m_pad // TM_E

    def xmap(t, te, na, *_):
        return (jnp.minimum(t, na[0] - 1), 0)

    def bmap(t, te, *_):
        return (layer, te[t], 0, 0)

    grid_spec = pltpu.PrefetchScalarGridSpec(
        num_scalar_prefetch=6,
        grid=(n_tiles,),
        in_specs=[
            pl.BlockSpec((TM_E, D_MODEL), xmap),
            pl.BlockSpec((None, None, 1, D_MODEL), bmap),
            pl.BlockSpec((None, None, 1, D_MODEL), bmap),
            pl.BlockSpec((None, None, 1, D_MODEL), bmap),
            pl.BlockSpec(memory_space=pl.ANY),
            pl.BlockSpec(memory_space=pl.ANY),
            pl.BlockSpec(memory_space=pl.ANY),
        ],
        out_specs=pl.BlockSpec((TM_E, D_MODEL), lambda t, *_: (t, 0)),
        scratch_shapes=[
            pltpu.VMEM((3, 2, D_MODEL, D_MODEL), F32),
            pltpu.VMEM((D_MODEL, D_MODEL), BF16),
            pltpu.VMEM((D_MODEL, D_MODEL), BF16),
            pltpu.VMEM((D_MODEL, D_MODEL), BF16),
            pltpu.SemaphoreType.DMA((3, 2)),
        ],
    )
    return pl.pallas_call(
        functools.partial(_experts_kernel, layer),
        grid_spec=grid_spec,
        out_shape=jax.ShapeDtypeStruct((m_pad, D_MODEL), F32),
        compiler_params=pltpu.CompilerParams(
            dimension_semantics=("arbitrary",), vmem_limit_bytes=VMEM_LIMIT),
        name="experts",
    )(tile_expert, n_active, tile_first, tile_slot, tile_next, tile_chunks, xs, bg, bu, bd, wg, wu, wd)


def _combine_kernel(n_tiles, alpha, pos_ref, x1_ref, gates_ref, lng_ref, lnb_ref, ys_hbm, out_ref, buf, sem):
    j = pl.program_id(0)

    for slot in range(2):
        @pl.when(jnp.logical_and(j < n_tiles, j % 2 == slot))
        def _(slot=slot):
            for k in range(TOP_K):
                for r in range(TQ):
                    pltpu.make_async_copy(
                        ys_hbm.at[pl.ds(pos_ref[k, r], 1), :], buf.at[slot, k, pl.ds(r, 1), :],
                        sem.at[slot]).start(priority=r % 2)

    for slot in range(2):
        @pl.when(jnp.logical_and(j > 0, (j - 1) % 2 == slot))
        def _(slot=slot):
            for k in range(TOP_K):
                pltpu.make_async_copy(ys_hbm.at[pl.ds(0, TQ), :], buf.at[slot, k], sem.at[slot]).wait()
            gates = gates_ref[...]
            ffn = buf[slot, 0] * gates[:, 0:1]
            for k in range(1, TOP_K):
                ffn = ffn + buf[slot, k] * gates[:, k:k + 1]
            out_ref[...] = _layer_norm(alpha * x1_ref[...] + ffn, lng_ref[...], lnb_ref[...])


def _combine(pos, x1, gates_tm, lng, lnb, ys, *, alpha):
    n = x1.shape[0]
    n_tiles = n // TQ
    req = lambda j: jnp.minimum(j, n_tiles - 1)
    fin = lambda j: jnp.maximum(j - 1, 0)
    return pl.pallas_call(
        functools.partial(_combine_kernel, n_tiles, alpha),
        grid=(n_tiles + 1,),
        in_specs=[
            pl.BlockSpec((TOP_K, TQ), lambda j: (0, req(j)), memory_space=pltpu.SMEM),
            pl.BlockSpec((TQ, D_MODEL), lambda j: (fin(j), 0)),
            pl.BlockSpec((TQ, TOP_K), lambda j: (fin(j), 0)),
            pl.BlockSpec((1, D_MODEL), lambda j: (0, 0)),
            pl.BlockSpec((1, D_MODEL), lambda j: (0, 0)),
            pl.BlockSpec(memory_space=pl.ANY),
        ],
        out_specs=pl.BlockSpec((TQ, D_MODEL), lambda j: (fin(j), 0)),
        out_shape=jax.ShapeDtypeStruct((n, D_MODEL), F32),
        scratch_shapes=[pltpu.VMEM((2, TOP_K, TQ, D_MODEL), F32), pltpu.SemaphoreType.DMA((2,))],
        compiler_params=pltpu.CompilerParams(
            dimension_semantics=("arbitrary",), vmem_limit_bytes=VMEM_LIMIT),
        name="combine",
    )(pos, x1, gates_tm, lng, lnb, ys)


def kernel(x, w_in, b_in, w_pool, pool_scale, rpb, conv_dw, conv_dw_b, conv_ln_g, conv_ln_b,
           w_conv_pw, b_conv_pw, w_out, b_out, ln1_g, ln1_b, w_router, b_router,
           w_gate, b_gate, w_up, b_up, w_down, b_down, ln2_g, ln2_b):
    batch, seq, d = x.shape
    depth = w_in.shape[0]
    n = batch * seq
    rows = seq // GRID_W
    alpha = (2.0 * depth) ** 0.25
    off_q, off_k, off_v = POOL_W, POOL_W + NA_W, POOL_W + 2 * NA_W
    off_ca = off_v + NA_W
    m_pad = n * TOP_K + N_EXPERTS * TM_E
    n_tiles = m_pad // TM_E

    tri = (np.arange(TQ)[:, None] < np.arange(TQ)[None, :]).astype(np.float32)
    tri = jnp.asarray(tri, BF16)
    row2 = lambda v: v.reshape(1, -1)
    bias_tabs = _attn_bias_tables(rpb, rows)
    experts_iota = jnp.arange(N_EXPERTS, dtype=I32)
    tile_starts = jnp.arange(n_tiles, dtype=I32) * TM_E
    b_gate4 = b_gate.reshape(depth, N_EXPERTS, 1, d)
    b_up4 = b_up.reshape(depth, N_EXPERTS, 1, d)
    b_down4 = b_down.reshape(depth, N_EXPERTS, 1, d)

    h = x.reshape(n, d)
    for l in range(depth):
        wqkv = w_in[l][:, off_q:off_ca].astype(BF16)
        bqkv = row2(b_in[l][off_q:off_ca])
        wrest = jnp.concatenate([w_in[l][:, :off_q], w_in[l][:, off_ca:]], axis=1).astype(BF16)
        brest = row2(jnp.concatenate([b_in[l][:off_q], b_in[l][off_ca:]]))
        qkv, rest = _inproj(h, wqkv, bqkv, wrest, brest)

        wpool_bd = jax.scipy.linalg.block_diag(*[w_pool[l][g] for g in range(len(POOL_WINDOWS))]).astype(BF16)
        x1, top_i, top_p, rank, counts = _mixer(
            h, qkv, rest, bias_tabs, wpool_bd, row2(pool_scale[l]),
            conv_dw[l].reshape(CONV_K, CONV_W), row2(conv_dw_b[l]), row2(conv_ln_g[l]), row2(conv_ln_b[l]),
            w_conv_pw[l].astype(BF16), row2(b_conv_pw[l]),
            w_out[l].astype(BF16), row2(b_out[l]), row2(ln1_g[l]), row2(ln1_b[l]),
            w_router[l].T.astype(BF16), b_router[l].reshape(N_EXPERTS, 1), tri,
            layer=l, seq=seq, alpha=alpha)

        cnt = counts[:, 0]
        cpad = ((cnt + TM_E - 1) // TM_E) * TM_E
        ends = jnp.cumsum(cpad)
        off = ends - cpad
        hot = top_i[None] == experts_iota[:, None, None]
        pos = rank + jnp.sum(jnp.where(hot, off[:, None, None], 0), axis=0)
        n_active = (ends[-1] // TM_E).astype(I32)
        last_start = jnp.minimum(tile_starts, ends[-1] - TM_E)
        tile_expert = jnp.sum((ends[None, :] <= last_start[:, None]).astype(I32), axis=1)
        owns = cnt > 0
        e_slot = (jnp.cumsum(owns.astype(I32)) - 1) % 2
        later = jnp.logical_and(experts_iota[None, :] > experts_iota[:, None], owns[None, :])
        e_next = jnp.min(jnp.where(later, experts_iota[None, :], N_EXPERTS), axis=1)
        e_next = jnp.where(e_next == N_EXPERTS, experts_iota, e_next)
        tile_hot = tile_expert[:, None] == experts_iota[None, :]
        pick = lambda v: jnp.sum(jnp.where(tile_hot, v[None, :], 0), axis=1).astype(I32)
        tile_first = (pick(off) == tile_starts).astype(I32)
        tile_slot = pick(e_slot)
        tile_next = pick(e_next)

        xs = _dispatch(x1, pos, (off + cnt).astype(I32), ends.astype(I32), m_pad)
        real_rows = jnp.clip(pick(off + cnt) - tile_starts, 0, TM_E)
        tile_chunks = (real_rows + TM_CHUNK - 1) // TM_CHUNK
        ys = _experts(l, tile_expert, n_active.reshape(1), tile_first, tile_slot, tile_next, tile_chunks, xs,
                      w_gate, b_gate4, w_up, b_up4, w_down, b_down4)
        h = _combine(pos, x1, top_p.T, row2(ln2_g[l]), row2(ln2_b[l]), ys, alpha=alpha)
    return h.reshape(batch, seq, d)
```

```python
import functools

import jax
import jax.numpy as jnp
import numpy as np
from jax import lax
from jax.experimental import pallas as pl
from jax.experimental.pallas import tpu as pltpu

F32 = jnp.float32
BF16 = jnp.bfloat16
I32 = jnp.int32

D_MODEL = 1024
GRID_W = 64
POOL_W = 256
POOL_WINDOWS = (2, 4, 8, 16)
POOL_GROUP_W = 64
NA_W = 512
NA_HEADS = 8
NA_HEAD_DIM = 64
NA_ROWS = 8
NA_COLS = 16
CONV_W = 256
CONV_K = 31
N_EXPERTS = 32
TOP_K = 4
SWIGLU_ALPHA = 1.702
SWIGLU_LIMIT = 7.0
LN_EPS = 1e-5
NEG_INF = -1e30

LANES = 128
TM_IN = 512
TQ = 256
TQ_ROWS = TQ // GRID_W
HALO = 16
SUBLANES = 8
N_CBLK = GRID_W // SUBLANES
CBLK_ROWS = TQ_ROWS * SUBLANES
N_QGRP = GRID_W // NA_COLS
QGRP = TQ // N_QGRP
KBAND_COLS = 2 * NA_COLS
KBAND = TQ_ROWS * KBAND_COLS
NKEY = 3 * KBAND
TD = 512
TM_E = 1024
TM_CHUNK = 256
VMEM_LIMIT = 56 * 1024 * 1024


def _layer_norm(v, g, b):
    mu = jnp.mean(v, axis=-1, keepdims=True)
    c = v - mu
    var = jnp.mean(c * c, axis=-1, keepdims=True)
    return c * lax.rsqrt(var + LN_EPS) * g + b


def _band_start_cblk(g):
    return min(max(2 * g - 1, 0), N_CBLK - KBAND_COLS // SUBLANES)


def _stored_row_groups():
    return [r * N_CBLK + cb for cb in range(N_CBLK) for r in range(TQ_ROWS)]


def _inproj_kernel(x_ref, wqkv_ref, bqkv_ref, wrest_ref, brest_ref, qkv_ref, rest_ref):
    x = x_ref[...].astype(BF16)
    rest_ref[...] = jnp.dot(x, wrest_ref[...], preferred_element_type=F32) + brest_ref[...]
    groups = [x_ref[pl.ds(tile * TQ + gn * SUBLANES, SUBLANES), :]
              for tile in range(TM_IN // TQ) for gn in _stored_row_groups()]
    xp = jnp.concatenate(groups, axis=0).astype(BF16)
    qkv = jnp.dot(xp, wqkv_ref[...], preferred_element_type=F32) + bqkv_ref[...]
    qkv_ref[...] = qkv.astype(BF16)


def _inproj(x, wqkv, bqkv, wrest, brest):
    n = x.shape[0]
    return pl.pallas_call(
        _inproj_kernel,
        grid=(n // TM_IN,),
        in_specs=[
            pl.BlockSpec((TM_IN, D_MODEL), lambda i: (i, 0)),
            pl.BlockSpec((D_MODEL, 3 * NA_W), lambda i: (0, 0)),
            pl.BlockSpec((1, 3 * NA_W), lambda i: (0, 0)),
            pl.BlockSpec((D_MODEL, POOL_W + 2 * CONV_W), lambda i: (0, 0)),
            pl.BlockSpec((1, POOL_W + 2 * CONV_W), lambda i: (0, 0)),
        ],
        out_specs=[
            pl.BlockSpec((TM_IN, 3 * NA_W), lambda i: (i, 0)),
            pl.BlockSpec((TM_IN, POOL_W + 2 * CONV_W), lambda i: (i, 0)),
        ],
        out_shape=[
            jax.ShapeDtypeStruct((n, 3 * NA_W), BF16),
            jax.ShapeDtypeStruct((n, POOL_W + 2 * CONV_W), F32),
        ],
        compiler_params=pltpu.CompilerParams(
            dimension_semantics=("arbitrary",), vmem_limit_bytes=VMEM_LIMIT),
        name="inproj",
    )(x, wqkv, bqkv, wrest, brest)


def _attn_bias_tables(rpb, rows):
    kr_win = min(NA_ROWS, rows)
    n_tiles = rows // TQ_ROWS
    n_r, n_c = 2 * NA_ROWS - 1, 2 * NA_COLS - 1
    n_layers = rpb.shape[0]
    c_hots, c_oks = [], []
    for g in range(N_QGRP):
        qc = g * NA_COLS + np.arange(NA_COLS)
        kc = _band_start_cblk(g) * SUBLANES + np.arange(KBAND_COLS)
        sc = np.clip(qc - NA_COLS // 2, 0, GRID_W - NA_COLS)[:, None]
        c_oks.append((kc[None] >= sc) & (kc[None] < sc + NA_COLS))
        c_off = np.clip(kc[None] - qc[:, None] + NA_COLS - 1, 0, n_c - 1)
        c_hots.append((c_off[..., None] == np.arange(n_c)).astype(np.float32))
    r_hots, r_oks = [], []
    for tile in (0, 1, n_tiles - 1):
        qr = tile * TQ_ROWS + np.arange(TQ_ROWS)
        slot_tile = np.array([tile - 1, tile, tile + 1])
        kr = slot_tile[:, None] * TQ_ROWS + np.arange(TQ_ROWS)[None, :]
        slot_ok = ((slot_tile >= 0) & (slot_tile < n_tiles))[:, None]
        sr = np.clip(qr - kr_win // 2, 0, rows - kr_win)[:, None, None]
        r_oks.append((kr[None] >= sr) & (kr[None] < sr + kr_win) & slot_ok[None])
        r_off = np.clip(kr[None] - qr[:, None, None] + NA_ROWS - 1, 0, n_r - 1)
        r_hots.append((r_off[..., None] == np.arange(n_r)).astype(np.float32))
    c_hot = jnp.asarray(np.stack(c_hots))
    r_off = np.stack([np.argmax(h, axis=-1) for h in r_hots])
    c_ok = np.stack(c_oks).reshape(N_QGRP, 2, SUBLANES, KBAND_COLS // SUBLANES, SUBLANES)
    r_ok = np.stack(r_oks)
    ok = (r_ok[:, None, None, :, None, :, None, :, None]
          & c_ok[None, :, :, None, :, None, :, None, :])
    ok = np.broadcast_to(ok[:, :, None], (3, N_QGRP, 2) + ok.shape[2:])
    ok = jnp.asarray(ok.reshape(3, N_QGRP, 2 * QGRP, NKEY).astype(np.float32))
    rpb5 = rpb.astype(F32).reshape(n_layers, NA_HEADS // 2, 2, n_r, n_c)
    by_col = jnp.einsum("lpjrc,gmnc->lpgjmrn", rpb5, c_hot, precision=lax.Precision.HIGHEST)
    by_col = by_col.reshape(n_layers * (NA_HEADS // 2) * N_QGRP, 2 * NA_COLS, n_r * KBAND_COLS)
    nb = KBAND_COLS // SUBLANES
    expand = np.zeros((TQ_ROWS, 2, 2, TQ_ROWS, SUBLANES, 2, 2, SUBLANES), np.float32)
    for q in range(TQ_ROWS):
        for j in range(2):
            for a in range(2):
                for x in range(SUBLANES):
                    expand[q, j, a, q, x, j, a, x] = 1.0
    expand = expand.reshape(TQ_ROWS, 2 * QGRP, 2 * NA_COLS)
    select = np.zeros((3, TQ_ROWS, n_r, nb, SUBLANES, 3, nb, TQ_ROWS, SUBLANES), np.float32)
    for v in range(3):
        for q in range(TQ_ROWS):
            for s in range(3):
                for y in range(TQ_ROWS):
                    for b in range(nb):
                        for z in range(SUBLANES):
                            select[v, q, r_off[v, q, s, y], b, z, s, b, y, z] = 1.0
    select = select.reshape(3, TQ_ROWS, n_r * KBAND_COLS, NKEY)
    lane_pad = -(n_r * KBAND_COLS) % LANES
    by_col = jnp.pad(by_col, ((0, 0), (0, 0), (0, lane_pad)))
    select = np.pad(select, ((0, 0), (0, 0), (0, lane_pad), (0, 0)))
    tab = _bias_table_call(by_col, jnp.asarray(expand, BF16), jnp.asarray(select, BF16), ok)
    return tab.reshape(n_layers, NA_HEADS // 2, N_QGRP, 3, 2 * QGRP, NKEY)


def _bias_table_kernel(u_ref, expand_ref, select_ref, ok_ref, out_ref):
    u = u_ref[...]
    parts = []
    for _ in range(3):
        piece = u.astype(BF16)
        parts.append(piece)
        u = u - piece.astype(F32)
    n_k = u.shape[1]
    pieces = jnp.concatenate(parts, axis=1)
    acc = [jnp.zeros((2 * QGRP, NKEY), F32) for _ in range(3)]
    for q in range(TQ_ROWS):
        rows = jnp.dot(expand_ref[q], pieces, preferred_element_type=F32).astype(BF16)
        rows = jnp.concatenate([rows[:, k * n_k:(k + 1) * n_k] for k in range(3)], axis=0)
        for v in range(3):
            sel = jnp.dot(rows, select_ref[v, q], preferred_element_type=F32)
            acc[v] = acc[v] + (sel[0:2 * QGRP] + sel[2 * QGRP:4 * QGRP] + sel[4 * QGRP:])
    for v in range(3):
        out_ref[v] = jnp.where(ok_ref[v, 0] > 0.0, acc[v], NEG_INF)


def _bias_table_call(by_col, expand, select, ok):
    n, n_u, n_k = by_col.shape
    return pl.pallas_call(
        _bias_table_kernel,
        grid=(n,),
        in_specs=[
            pl.BlockSpec((None, n_u, n_k), lambda i: (i, 0, 0)),
            pl.BlockSpec(expand.shape, lambda i: (0, 0, 0)),
            pl.BlockSpec(select.shape, lambda i: (0, 0, 0, 0)),
            pl.BlockSpec((3, 1, 2 * QGRP, NKEY), lambda i: (0, i % N_QGRP, 0, 0)),
        ],
        out_specs=pl.BlockSpec((None, 3, 2 * QGRP, NKEY), lambda i: (i, 0, 0, 0)),
        out_shape=jax.ShapeDtypeStruct((n, 3, 2 * QGRP, NKEY), F32),
        compiler_params=pltpu.CompilerParams(
            dimension_semantics=("arbitrary",), vmem_limit_bytes=VMEM_LIMIT),
        name="bias_table",
    )(by_col, expand, select, ok)


def _mixer_kernel(tiles_per_seq, alpha,
                  x_ref, q_ref, kp_ref, kc_ref, kn_ref, vp_ref, vc_ref, vn_ref,
                  rc_ref, rp_ref, rn_ref, bias_ref,
                  wpool_ref, pscale_ref, dw_ref, dwb_ref, clng_ref, clnb_ref, wpw_ref, bpw_ref,
                  wout_ref, bout_ref, lng_ref, lnb_ref, wr_ref, br_ref, tri_ref,
                  x1_ref, topi_ref, topp_ref, rank_ref, counts_ref,
                  halo_ref, shift_ref, base_ref):
    i = pl.program_id(0)
    ib = i % tiles_per_seq
    has_prev = ib > 0
    has_next = ib < tiles_per_seq - 1

    @pl.when(i == 0)
    def _():
        base_ref[...] = jnp.zeros_like(base_ref)

    halo_ref[0:HALO, :] = jnp.where(has_prev, rp_ref[...], 0.0)
    halo_ref[HALO:HALO + TQ, :] = rc_ref[...]
    halo_ref[HALO + TQ:, :] = jnp.where(has_next, rn_ref[...], 0.0)

    def fill_shifts():
        for b in range(1, SUBLANES):
            shift_ref[b - 1] = halo_ref[pl.ds(b, TQ + 2 * HALO - SUBLANES), 0:CONV_W]

    def window(start):
        a8, b = divmod(start, SUBLANES)
        if b == 0:
            return halo_ref[pl.ds(a8 * SUBLANES, TQ), 0:CONV_W]
        return shift_ref[b - 1, pl.ds(a8 * SUBLANES, TQ), :]

    fill_shifts()

    def u_at(off):
        return window(HALO + off)

    t_seq = ib * TQ + lax.broadcasted_iota(I32, (TQ, 1), 0)
    seq_len = tiles_per_seq * TQ
    group = lax.broadcasted_iota(I32, (1, POOL_W), 1) // POOL_GROUP_W
    u0 = u_at(0)
    acc = u0
    mean = jnp.zeros((TQ, POOL_W), F32)
    done = 0
    for g, w in enumerate(POOL_WINDOWS):
        half = w // 2
        for o in range(done + 1, half + 1):
            acc = acc + u_at(-o) + (u_at(o - 1) if o > 1 else 0.0)
        done = half
        cnt = (jnp.minimum(t_seq + half, seq_len) - jnp.maximum(t_seq - half, 0)).astype(F32)
        mean = jnp.where(group == g, acc * (1.0 / cnt), mean)
    d = (mean - u0).astype(BF16)
    y_pool = jnp.dot(d, wpool_ref[...], preferred_element_type=F32) * pscale_ref[...]
    mix = jnp.dot(y_pool.astype(BF16), wout_ref[0:POOL_W, :], preferred_element_type=F32)

    a = halo_ref[:, POOL_W:POOL_W + CONV_W]
    gate = halo_ref[:, POOL_W + CONV_W:]
    halo_ref[:, 0:CONV_W] = a * jax.nn.sigmoid(gate)
    fill_shifts()
    conv = jnp.zeros((TQ, CONV_W), F32) + dwb_ref[...]
    for k in range(CONV_K):
        conv = conv + window(HALO - CONV_K // 2 + k) * dw_ref[k:k + 1, :]
    hc = _layer_norm(conv, clng_ref[...], clnb_ref[...])
    hc = hc * jax.nn.sigmoid(hc)
    y_conv = jnp.dot(hc.astype(BF16), wpw_ref[...], preferred_element_type=F32) + bpw_ref[...]
    mix = mix + jnp.dot(y_conv.astype(BF16), wout_ref[POOL_W + NA_W:, :], preferred_element_type=F32)

    low = lax.broadcasted_iota(I32, (QGRP, LANES), 1) < NA_HEAD_DIM
    scale = NA_HEAD_DIM ** -0.5
    n_pairs = NA_HEADS // 2
    bands = [pl.ds(_band_start_cblk(g) * CBLK_ROWS, KBAND) for g in range(N_QGRP)]
    col = lambda p: slice(p * LANES, (p + 1) * LANES)
    s_blk = []
    for p in range(n_pairs):
        for g in range(N_QGRP):
            qp = q_ref[pl.ds(g * QGRP, QGRP), col(p)].astype(F32) * scale
            qs = jnp.concatenate([jnp.where(low, qp, 0.0), jnp.where(low, 0.0, qp)], axis=0).astype(BF16)
            kk = jnp.concatenate([kp_ref[bands[g], col(p)], kc_ref[bands[g], col(p)], kn_ref[bands[g], col(p)]],
                                 axis=0)
            s_blk.append(lax.dot_general(qs, kk, (((1,), (1,)), ((), ())), preferred_element_type=F32))
    blk_rows = 2 * QGRP
    s = jnp.concatenate(s_blk, axis=0) + bias_ref[...].reshape(n_pairs * N_QGRP * blk_rows, NKEY)
    m = jnp.max(s, axis=-1, keepdims=True)
    e = jnp.exp(s - m)
    rl = 1.0 / jnp.sum(e, axis=-1, keepdims=True)
    e = e.astype(BF16)
    for p in range(n_pairs):
        o_grp = []
        for g in range(N_QGRP):
            vv = jnp.concatenate([vp_ref[bands[g], col(p)], vc_ref[bands[g], col(p)], vn_ref[bands[g], col(p)]],
                                 axis=0)
            rows = slice((p * N_QGRP + g) * blk_rows, (p * N_QGRP + g + 1) * blk_rows)
            o = jnp.dot(e[rows], vv, preferred_element_type=F32) * rl[rows]
            o_grp.append(jnp.where(low, o[0:QGRP], o[QGRP:]))
        o_pair = jnp.concatenate(
            [o_grp[cb // 2][(cb % 2) * CBLK_ROWS + r * SUBLANES:(cb % 2) * CBLK_ROWS + (r + 1) * SUBLANES]
             for r in range(TQ_ROWS) for cb in range(N_CBLK)], axis=0)
        r0 = POOL_W + p * LANES
        mix = mix + jnp.dot(o_pair.astype(BF16), wout_ref[r0:r0 + LANES, :], preferred_element_type=F32)

    x1 = _layer_norm(alpha * x_ref[...] + mix + bout_ref[...], lng_ref[...], lnb_ref[...])
    x1_ref[...] = x1

    lt = lax.dot_general(wr_ref[...], x1.astype(BF16), (((1,), (1,)), ((), ())),
                         preferred_element_type=F32) + br_ref[...]
    eidx = lax.broadcasted_iota(I32, (N_EXPERTS, TQ), 0)
    work = lt
    vals, idxs = [], []
    for _ in range(TOP_K):
        mk = jnp.max(work, axis=0, keepdims=True)
        ik = jnp.min(jnp.where(work == mk, eidx, N_EXPERTS), axis=0, keepdims=True)
        vals.append(mk)
        idxs.append(ik)
        work = jnp.where(eidx == ik, -jnp.inf, work)
    ex = [jnp.exp(v - vals[0]) for v in vals]
    den = ex[0] + ex[1] + ex[2] + ex[3]
    topp_ref[...] = jnp.concatenate([e_ / den for e_ in ex], axis=0)
    topi_ref[...] = jnp.concatenate(idxs, axis=0)

    run = base_ref[...]
    ranks = []
    for k in range(TOP_K):
        hot = (eidx == idxs[k]).astype(F32)
        before = jnp.dot(hot.astype(BF16), tri_ref[...], preferred_element_type=F32)
        ranks.append(jnp.sum(hot * (run + before), axis=0, keepdims=True))
        run = run + jnp.sum(hot, axis=1, keepdims=True)
    rank_ref[...] = jnp.concatenate(ranks, axis=0).astype(I32)
    base_ref[...] = run
    counts_ref[...] = jnp.broadcast_to(run, counts_ref.shape).astype(I32)


def _mixer(x, qkv, rest, bias_tab, wpool_bd, pscale, dw, dwb, clng, clnb, wpw, bpw,
           wout, bout, lng, lnb, wr_t, br, tri, *, layer, seq, alpha):
    n = x.shape[0]
    nt = n // TQ
    tps = seq // TQ
    hb = TQ // HALO

    def prev_t(i):
        return jnp.where(i % tps == 0, i, i - 1)

    def next_t(i):
        return jnp.where(i % tps == tps - 1, i, i + 1)

    def variant(i):
        ib = i % tps
        return jnp.where(ib == 0, 0, jnp.where(ib == tps - 1, 2, 1))

    def const(shape):
        return pl.BlockSpec(shape, lambda i: tuple(0 for _ in shape))

    in_specs = [
        pl.BlockSpec((TQ, D_MODEL), lambda i: (i, 0)),
        pl.BlockSpec((TQ, NA_W), lambda i: (i, 0)),
        pl.BlockSpec((TQ, NA_W), lambda i: (prev_t(i), 1)),
        pl.BlockSpec((TQ, NA_W), lambda i: (i, 1)),
        pl.BlockSpec((TQ, NA_W), lambda i: (next_t(i), 1)),
        pl.BlockSpec((TQ, NA_W), lambda i: (prev_t(i), 2)),
        pl.BlockSpec((TQ, NA_W), lambda i: (i, 2)),
        pl.BlockSpec((TQ, NA_W), lambda i: (next_t(i), 2)),
        pl.BlockSpec((TQ, POOL_W + 2 * CONV_W), lambda i: (i, 0)),
        pl.BlockSpec((HALO, POOL_W + 2 * CONV_W), lambda i: (jnp.maximum(i * hb - 1, 0), 0)),
        pl.BlockSpec((HALO, POOL_W + 2 * CONV_W), lambda i: (jnp.minimum((i + 1) * hb, nt * hb - 1), 0)),
        pl.BlockSpec((None, NA_HEADS // 2, N_QGRP, 1, 2 * QGRP, NKEY),
                     lambda i: (layer, 0, 0, variant(i), 0, 0)),
        const((POOL_W, POOL_W)), const((1, POOL_W)),
        const((CONV_K, CONV_W)), const((1, CONV_W)), const((1, CONV_W)), const((1, CONV_W)),
        const((CONV_W, CONV_W)), const((1, CONV_W)),
        const((D_MODEL, D_MODEL)), const((1, D_MODEL)), const((1, D_MODEL)), const((1, D_MODEL)),
        const((N_EXPERTS, D_MODEL)), const((N_EXPERTS, 1)), const((TQ, TQ)),
    ]
    out_specs = [
        pl.BlockSpec((TQ, D_MODEL), lambda i: (i, 0)),
        pl.BlockSpec((TOP_K, TQ), lambda i: (0, i)),
        pl.BlockSpec((TOP_K, TQ), lambda i: (0, i)),
        pl.BlockSpec((TOP_K, TQ), lambda i: (0, i)),
        pl.BlockSpec((N_EXPERTS, LANES), lambda i: (0, 0)),
    ]
    out_shape = [
        jax.ShapeDtypeStruct((n, D_MODEL), F32),
        jax.ShapeDtypeStruct((TOP_K, n), I32),
        jax.ShapeDtypeStruct((TOP_K, n), F32),
        jax.ShapeDtypeStruct((TOP_K, n), I32),
        jax.ShapeDtypeStruct((N_EXPERTS, LANES), I32),
    ]
    return pl.pallas_call(
        functools.partial(_mixer_kernel, tps, alpha),
        grid=(nt,),
        in_specs=in_specs,
        out_specs=out_specs,
        out_shape=out_shape,
        scratch_shapes=[
            pltpu.VMEM((TQ + 2 * HALO, POOL_W + 2 * CONV_W), F32),
            pltpu.VMEM((SUBLANES - 1, TQ + 2 * HALO - SUBLANES, CONV_W), F32),
            pltpu.VMEM((N_EXPERTS, 1), F32),
        ],
        compiler_params=pltpu.CompilerParams(
            dimension_semantics=("arbitrary",), vmem_limit_bytes=VMEM_LIMIT),
        name="mixer",
    )(x, qkv, qkv, qkv, qkv, qkv, qkv, qkv, rest, rest, rest, bias_tab,
      wpool_bd, pscale, dw, dwb, clng, clnb, wpw, bpw, wout, bout, lng, lnb, wr_t, br, tri)


PAD_BLOCKS = (8, 16, 32, 64, 128, 256, 512)
assert PAD_BLOCKS[0] == SUBLANES and TM_E == 2 * PAD_BLOCKS[-1]


def _dispatch_kernel(n_tiles, pad_lo_ref, pad_hi_ref, pos_ref, x_ref, xs_hbm, zero_ref, ring, sem, pad_sem):
    i = pl.program_id(0)

    def pad_fill(act):
        for e in range(N_EXPERTS):
            lo = pad_lo_ref[e]
            hi = pad_hi_ref[e]
            head_end = jnp.minimum((lo + (SUBLANES - 1)) & -SUBLANES, hi)
            for j in range(SUBLANES - 1):
                @pl.when(lo + j < head_end)
                def _(j=j, lo=lo):
                    act(pltpu.make_async_copy(
                        zero_ref.at[pl.ds(0, 1), :], xs_hbm.at[pl.ds(lo + j, 1), :], pad_sem))
            a = head_end
            for b in PAD_BLOCKS:
                take = (a & b) != 0

                @pl.when(take)
                def _(a=a, b=b):
                    act(pltpu.make_async_copy(
                        zero_ref.at[pl.ds(0, b), :], xs_hbm.at[pl.ds(pl.multiple_of(a, SUBLANES), b), :], pad_sem))
                a = jnp.where(take, a + b, a)
        blk = PAD_BLOCKS[-1]

        def tail(c, carry):
            act(pltpu.make_async_copy(
                zero_ref, xs_hbm.at[pl.ds(pl.multiple_of(c * blk, blk), blk), :], pad_sem))
            return carry
        lax.fori_loop(pad_hi_ref[N_EXPERTS - 1] // blk, xs_hbm.shape[0] // blk, tail, 0)

    @pl.when(i == 0)
    def _():
        zero_ref[...] = jnp.zeros_like(zero_ref)
        pad_fill(lambda c: c.start())

    def wait_rows(slot):
        for k in range(TOP_K):
            pltpu.make_async_copy(ring.at[slot], xs_hbm.at[pl.ds(0, TD), :], sem.at[slot]).wait()

    for slot in range(2):
        @pl.when(i % 2 == slot)
        def _(slot=slot):
            ring[slot] = x_ref[...]
            for k in range(TOP_K):
                for r in range(TD):
                    pltpu.make_async_copy(
                        ring.at[slot, pl.ds(r, 1), :], xs_hbm.at[pl.ds(pos_ref[k, r], 1), :],
                        sem.at[slot]).start(priority=r % 2)

            @pl.when(i > 0)
            def _():
                wait_rows(1 - slot)

            @pl.when(i == n_tiles - 1)
            def _():
                wait_rows(slot)

    @pl.when(i == 0)
    def _():
        pad_fill(lambda c: c.wait())


def _dispatch(x1, pos, pad_lo, pad_hi, m_pad):
    n = x1.shape[0]
    grid_spec = pltpu.PrefetchScalarGridSpec(
        num_scalar_prefetch=2,
        grid=(n // TD,),
        in_specs=[
            pl.BlockSpec((TOP_K, TD), lambda i, *_: (0, i), memory_space=pltpu.SMEM),
            pl.BlockSpec((TD, D_MODEL), lambda i, *_: (i, 0)),
        ],
        out_specs=pl.BlockSpec(memory_space=pl.ANY),
        scratch_shapes=[
            pltpu.VMEM((PAD_BLOCKS[-1], D_MODEL), F32),
            pltpu.VMEM((2, TD, D_MODEL), F32),
            pltpu.SemaphoreType.DMA((2,)),
            pltpu.SemaphoreType.DMA(()),
        ],
    )
    return pl.pallas_call(
        functools.partial(_dispatch_kernel, n // TD),
        grid_spec=grid_spec,
        out_shape=jax.ShapeDtypeStruct((m_pad, D_MODEL), F32),
        compiler_params=pltpu.CompilerParams(dimension_semantics=("arbitrary",)),
        name="dispatch",
    )(pad_lo, pad_hi, pos, x1)


def _experts_kernel(layer, te_ref, na_ref, first_ref, slot_ref, nxt_ref, rows_ref,
                    xs_ref, bg_ref, bu_ref, bd_ref, wg_hbm, wu_hbm, wd_hbm,
                    y_ref, wbuf, wg_bf, wu_bf, wd_bf, sem):
    t = pl.program_id(0)
    active = t < na_ref[0]
    e = te_ref[t]
    s = slot_ref[t]

    def fetch(expert, slot):
        return [pltpu.make_async_copy(w.at[layer, expert], wbuf.at[j, slot], sem.at[j, slot])
                for j, w in enumerate((wg_hbm, wu_hbm, wd_hbm))]

    @pl.when(t == 0)
    def _():
        for c in fetch(e, s):
            c.start()

    @pl.when(jnp.logical_and(active, first_ref[t] == 1))
    def _():
        for c in fetch(e, s):
            c.wait()

        @pl.when(nxt_ref[t] != e)
        def _():
            for c in fetch(nxt_ref[t], 1 - s):
                c.start()

        wg_bf[...] = wbuf[0, s].astype(BF16)
        wu_bf[...] = wbuf[1, s].astype(BF16)
        wd_bf[...] = wbuf[2, s].astype(BF16)

    for chunks in range(1, TM_E // TM_CHUNK + 1):
        m = chunks * TM_CHUNK

        @pl.when(jnp.logical_and(active, rows_ref[t] == chunks))
        def _(m=m):
            x = xs_ref[0:m, :].astype(BF16)
            g = jnp.minimum(jnp.dot(x, wg_bf[...], preferred_element_type=F32) + bg_ref[...], SWIGLU_LIMIT)
            u = jnp.clip(jnp.dot(x, wu_bf[...], preferred_element_type=F32) + bu_ref[...],
                         -SWIGLU_LIMIT, SWIGLU_LIMIT)
            act = (u + 1.0) * g * jax.nn.sigmoid(SWIGLU_ALPHA * g)
            y_ref[0:m, :] = jnp.dot(act.astype(BF16), wd_bf[...], preferred_element_type=F32) + bd_ref[...]
            if m < TM_E:
                y_ref[m:, :] = jnp.zeros((TM_E - m, D_MODEL), F32)

    @pl.when(jnp.logical_not(active))
    def _():
        y_ref[...] = jnp.zeros_like(y_ref)


def _experts(layer, tile_expert, n_active, tile_first, tile_slot, tile_next, tile_chunks,
             xs, wg, bg, wu, bu, wd, bd):
    m_pad = xs.shape[0]
    n_tiles = m_pad // TM_E

    def xmap(t, te, na, *_):
        return (jnp.minimum(t, na[0] - 1), 0)

    def bmap(t, te, *_):
        return (layer, te[t], 0, 0)

    grid_spec = pltpu.PrefetchScalarGridSpec(
        num_scalar_prefetch=6,
        grid=(n_tiles,),
        in_specs=[
            pl.BlockSpec((TM_E, D_MODEL), xmap),
            pl.BlockSpec((None, None, 1, D_MODEL), bmap),
            pl.BlockSpec((None, None, 1, D_MODEL), bmap),
            pl.BlockSpec((None, None, 1, D_MODEL), bmap),
            pl.BlockSpec(memory_space=pl.ANY),
            pl.BlockSpec(memory_space=pl.ANY),
            pl.BlockSpec(memory_space=pl.ANY),
        ],
        out_specs=pl.BlockSpec((TM_E, D_MODEL), lambda t, *_: (t, 0)),
        scratch_shapes=[
            pltpu.VMEM((3, 2, D_MODEL, D_MODEL), F32),
            pltpu.VMEM((D_MODEL, D_MODEL), BF16),
            pltpu.VMEM((D_MODEL, D_MODEL), BF16),
            pltpu.VMEM((D_MODEL, D_MODEL), BF16),
            pltpu.SemaphoreType.DMA((3, 2)),
        ],
    )
    return pl.pallas_call(
        functools.partial(_experts_kernel, layer),
        grid_spec=grid_spec,
        out_shape=jax.ShapeDtypeStruct((m_pad, D_MODEL), F32),
        compiler_params=pltpu.CompilerParams(
            dimension_semantics=("arbitrary",), vmem_limit_bytes=VMEM_LIMIT),
        name="experts",
    )(tile_expert, n_active, tile_first, tile_slot, tile_next, tile_chunks, xs, bg, bu, bd, wg, wu, wd)


def _combine_kernel(n_tiles, alpha, pos_ref, x1_ref, gates_ref, lng_ref, lnb_ref, ys_hbm, out_ref, buf, sem):
    j = pl.program_id(0)

    for slot in range(2):
        @pl.when(jnp.logical_and(j < n_tiles, j % 2 == slot))
        def _(slot=slot):
            for k in range(TOP_K):
                for r in range(TD):
                    pltpu.make_async_copy(
                        ys_hbm.at[pl.ds(pos_ref[k, r], 1), :], buf.at[slot, k, pl.ds(r, 1), :],
                        sem.at[slot]).start(priority=r % 2)

    for slot in range(2):
        @pl.when(jnp.logical_and(j > 0, (j - 1) % 2 == slot))
        def _(slot=slot):
            for k in range(TOP_K):
                pltpu.make_async_copy(ys_hbm.at[pl.ds(0, TD), :], buf.at[slot, k], sem.at[slot]).wait()
            gates = gates_ref[...]
            ffn = buf[slot, 0] * gates[:, 0:1]
            for k in range(1, TOP_K):
                ffn = ffn + buf[slot, k] * gates[:, k:k + 1]
            out_ref[...] = _layer_norm(alpha * x1_ref[...] + ffn, lng_ref[...], lnb_ref[...])


def _combine(pos, x1, gates_tm, lng, lnb, ys, *, alpha):
    n = x1.shape[0]
    n_tiles = n // TD
    req = lambda j: jnp.minimum(j, n_tiles - 1)
    fin = lambda j: jnp.maximum(j - 1, 0)
    return pl.pallas_call(
        functools.partial(_combine_kernel, n_tiles, alpha),
        grid=(n_tiles + 1,),
        in_specs=[
            pl.BlockSpec((TOP_K, TD), lambda j: (0, req(j)), memory_space=pltpu.SMEM),
            pl.BlockSpec((TD, D_MODEL), lambda j: (fin(j), 0)),
            pl.BlockSpec((TD, TOP_K), lambda j: (fin(j), 0)),
            pl.BlockSpec((1, D_MODEL), lambda j: (0, 0)),
            pl.BlockSpec((1, D_MODEL), lambda j: (0, 0)),
            pl.BlockSpec(memory_space=pl.ANY),
        ],
        out_specs=pl.BlockSpec((TD, D_MODEL), lambda j: (fin(j), 0)),
        out_shape=jax.ShapeDtypeStruct((n, D_MODEL), F32),
        scratch_shapes=[pltpu.VMEM((2, TOP_K, TD, D_MODEL), F32), pltpu.SemaphoreType.DMA((2,))],
        compiler_params=pltpu.CompilerParams(
            dimension_semantics=("arbitrary",), vmem_limit_bytes=VMEM_LIMIT),
        name="combine",
    )(pos, x1, gates_tm, lng, lnb, ys)


def kernel(x, w_in, b_in, w_pool, pool_scale, rpb, conv_dw, conv_dw_b, conv_ln_g, conv_ln_b,
           w_conv_pw, b_conv_pw, w_out, b_out, ln1_g, ln1_b, w_router, b_router,
           w_gate, b_gate, w_up, b_up, w_down, b_down, ln2_g, ln2_b):
    batch, seq, d = x.shape
    depth = w_in.shape[0]
    n = batch * seq
    rows = seq // GRID_W
    alpha = (2.0 * depth) ** 0.25
    off_q, off_k, off_v = POOL_W, POOL_W + NA_W, POOL_W + 2 * NA_W
    off_ca = off_v + NA_W
    m_pad = n * TOP_K + N_EXPERTS * TM_E
    n_tiles = m_pad // TM_E

    tri = (np.arange(TQ)[:, None] < np.arange(TQ)[None, :]).astype(np.float32)
    tri = jnp.asarray(tri, BF16)
    row2 = lambda v: v.reshape(1, -1)
    bias_tabs = _attn_bias_tables(rpb, rows)
    experts_iota = jnp.arange(N_EXPERTS, dtype=I32)
    tile_starts = jnp.arange(n_tiles, dtype=I32) * TM_E
    b_gate4 = b_gate.reshape(depth, N_EXPERTS, 1, d)
    b_up4 = b_up.reshape(depth, N_EXPERTS, 1, d)
    b_down4 = b_down.reshape(depth, N_EXPERTS, 1, d)

    h = x.reshape(n, d)
    for l in range(depth):
        wqkv = w_in[l][:, off_q:off_ca].astype(BF16)
        bqkv = row2(b_in[l][off_q:off_ca])
        wrest = jnp.concatenate([w_in[l][:, :off_q], w_in[l][:, off_ca:]], axis=1).astype(BF16)
        brest = row2(jnp.concatenate([b_in[l][:off_q], b_in[l][off_ca:]]))
        qkv, rest = _inproj(h, wqkv, bqkv, wrest, brest)

        wpool_bd = jax.scipy.linalg.block_diag(*[w_pool[l][g] for g in range(len(POOL_WINDOWS))]).astype(BF16)
        x1, top_i, top_p, rank, counts = _mixer(
            h, qkv, rest, bias_tabs, wpool_bd, row2(pool_scale[l]),
            conv_dw[l].reshape(CONV_K, CONV_W), row2(conv_dw_b[l]), row2(conv_ln_g[l]), row2(conv_ln_b[l]),
            w_conv_pw[l].astype(BF16), row2(b_conv_pw[l]),
            w_out[l].astype(BF16), row2(b_out[l]), row2(ln1_g[l]), row2(ln1_b[l]),
            w_router[l].T.astype(BF16), b_router[l].reshape(N_EXPERTS, 1), tri,
            layer=l, seq=seq, alpha=alpha)

        cnt = counts[:, 0]
        cpad = ((cnt + TM_E - 1) // TM_E) * TM_E
        ends = jnp.cumsum(cpad)
        off = ends - cpad
        hot = top_i[None] == experts_iota[:, None, None]
        pos = rank + jnp.sum(jnp.where(hot, off[:, None, None], 0), axis=0)
        n_active = (ends[-1] // TM_E).astype(I32)
        last_start = jnp.minimum(tile_starts, ends[-1] - TM_E)
        tile_expert = jnp.sum((ends[None, :] <= last_start[:, None]).astype(I32), axis=1)
        owns = cnt > 0
        e_slot = (jnp.cumsum(owns.astype(I32)) - 1) % 2
        later = jnp.logical_and(experts_iota[None, :] > experts_iota[:, None], owns[None, :])
        e_next = jnp.min(jnp.where(later, experts_iota[None, :], N_EXPERTS), axis=1)
        e_next = jnp.where(e_next == N_EXPERTS, experts_iota, e_next)
        tile_hot = tile_expert[:, None] == experts_iota[None, :]
        pick = lambda v: jnp.sum(jnp.where(tile_hot, v[None, :], 0), axis=1).astype(I32)
        tile_first = (pick(off) == tile_starts).astype(I32)
        tile_slot = pick(e_slot)
        tile_next = pick(e_next)

        xs = _dispatch(x1, pos, (off + cnt).astype(I32), ends.astype(I32), m_pad)
        real_rows = jnp.clip(pick(off + cnt) - tile_starts, 0, TM_E)
        tile_chunks = (real_rows + TM_CHUNK - 1) // TM_CHUNK
        ys = _experts(l, tile_expert, n_active.reshape(1), tile_first, tile_slot, tile_next, tile_chunks, xs,
                      w_gate, b_gate4, w_up, b_up4, w_down, b_down4)
        h = _combine(pos, x1, top_p.T, row2(ln2_g[l]), row2(ln2_b[l]), ys, alpha=alpha)
    return h.reshape(batch, seq, d)
```

```python
import functools

import jax
import jax.numpy as jnp
import numpy as np
from jax import lax
from jax.experimental import pallas as pl
from jax.experimental.pallas import tpu as pltpu

F32 = jnp.float32
BF16 = jnp.bfloat16
I32 = jnp.int32

D_MODEL = 1024
GRID_W = 64
POOL_W = 256
POOL_WINDOWS = (2, 4, 8, 16)
POOL_GROUP_W = 64
NA_W = 512
NA_HEADS = 8
NA_HEAD_DIM = 64
NA_ROWS = 8
NA_COLS = 16
CONV_W = 256
CONV_K = 31
N_EXPERTS = 32
TOP_K = 4
SWIGLU_ALPHA = 1.702
SWIGLU_LIMIT = 7.0
LN_EPS = 1e-5
NEG_INF = -1e30

LANES = 128
TM_IN = 512
TQ = 256
TQ_ROWS = TQ // GRID_W
HALO = 16
SUBLANES = 8
N_CBLK = GRID_W // SUBLANES
CBLK_ROWS = TQ_ROWS * SUBLANES
N_QGRP = GRID_W // NA_COLS
QGRP = TQ // N_QGRP
KBAND_COLS = 2 * NA_COLS
KBAND = TQ_ROWS * KBAND_COLS
NKEY = 3 * KBAND
PAIR_BATCH = 2
TD = 256
TM_E = 512
TM_CHUNK = 128
V7X_VMEM_BYTES = 64 * 1024 * 1024
VMEM_LIMIT = V7X_VMEM_BYTES * 7 // 8


def _layer_norm(v, g, b):
    mu = jnp.mean(v, axis=-1, keepdims=True)
    c = v - mu
    var = jnp.mean(c * c, axis=-1, keepdims=True)
    return c * lax.rsqrt(var + LN_EPS) * g + b


def _band_start_cblk(g):
    return min(max(2 * g - 1, 0), N_CBLK - KBAND_COLS // SUBLANES)


def _stored_row_groups():
    return [r * N_CBLK + cb for cb in range(N_CBLK) for r in range(TQ_ROWS)]


def _inproj_kernel(x_ref, wqkv_ref, bqkv_ref, wrest_ref, brest_ref, qkv_ref, rest_ref):
    x = x_ref[...].astype(BF16)
    rest_ref[...] = jnp.dot(x, wrest_ref[...], preferred_element_type=F32) + brest_ref[...]
    groups = [x_ref[pl.ds(tile * TQ + gn * SUBLANES, SUBLANES), :]
              for tile in range(TM_IN // TQ) for gn in _stored_row_groups()]
    xp = jnp.concatenate(groups, axis=0).astype(BF16)
    qkv = jnp.dot(xp, wqkv_ref[...], preferred_element_type=F32) + bqkv_ref[...]
    qkv_ref[...] = qkv.astype(BF16)


def _inproj(x, wqkv, bqkv, wrest, brest):
    n = x.shape[0]
    return pl.pallas_call(
        _inproj_kernel,
        grid=(n // TM_IN,),
        in_specs=[
            pl.BlockSpec((TM_IN, D_MODEL), lambda i: (i, 0)),
            pl.BlockSpec((D_MODEL, 3 * NA_W), lambda i: (0, 0)),
            pl.BlockSpec((1, 3 * NA_W), lambda i: (0, 0)),
            pl.BlockSpec((D_MODEL, POOL_W + 2 * CONV_W), lambda i: (0, 0)),
            pl.BlockSpec((1, POOL_W + 2 * CONV_W), lambda i: (0, 0)),
        ],
        out_specs=[
            pl.BlockSpec((TM_IN, 3 * NA_W), lambda i: (i, 0)),
            pl.BlockSpec((TM_IN, POOL_W + 2 * CONV_W), lambda i: (i, 0)),
        ],
        out_shape=[
            jax.ShapeDtypeStruct((n, 3 * NA_W), BF16),
            jax.ShapeDtypeStruct((n, POOL_W + 2 * CONV_W), F32),
        ],
        compiler_params=pltpu.CompilerParams(
            dimension_semantics=("arbitrary",), vmem_limit_bytes=VMEM_LIMIT),
        name="inproj",
    )(x, wqkv, bqkv, wrest, brest)


def _attn_bias_tables(rpb, rows):
    kr_win = min(NA_ROWS, rows)
    n_tiles = rows // TQ_ROWS
    n_r, n_c = 2 * NA_ROWS - 1, 2 * NA_COLS - 1
    n_layers = rpb.shape[0]
    c_hots, c_oks = [], []
    for g in range(N_QGRP):
        qc = g * NA_COLS + np.arange(NA_COLS)
        kc = _band_start_cblk(g) * SUBLANES + np.arange(KBAND_COLS)
        sc = np.clip(qc - NA_COLS // 2, 0, GRID_W - NA_COLS)[:, None]
        c_oks.append((kc[None] >= sc) & (kc[None] < sc + NA_COLS))
        c_off = np.clip(kc[None] - qc[:, None] + NA_COLS - 1, 0, n_c - 1)
        c_hots.append((c_off[..., None] == np.arange(n_c)).astype(np.float32))
    r_hots, r_oks = [], []
    for tile in (0, 1, n_tiles - 1):
        qr = tile * TQ_ROWS + np.arange(TQ_ROWS)
        slot_tile = np.array([tile - 1, tile, tile + 1])
        kr = slot_tile[:, None] * TQ_ROWS + np.arange(TQ_ROWS)[None, :]
        slot_ok = ((slot_tile >= 0) & (slot_tile < n_tiles))[:, None]
        sr = np.clip(qr - kr_win // 2, 0, rows - kr_win)[:, None, None]
        r_oks.append((kr[None] >= sr) & (kr[None] < sr + kr_win) & slot_ok[None])
        r_off = np.clip(kr[None] - qr[:, None, None] + NA_ROWS - 1, 0, n_r - 1)
        r_hots.append((r_off[..., None] == np.arange(n_r)).astype(np.float32))
    c_hot = jnp.asarray(np.stack(c_hots))
    r_off = np.stack([np.argmax(h, axis=-1) for h in r_hots])
    c_ok = np.stack(c_oks).reshape(N_QGRP, 2, SUBLANES, KBAND_COLS // SUBLANES, SUBLANES)
    r_ok = np.stack(r_oks)
    ok = (r_ok[:, None, None, :, None, :, None, :, None]
          & c_ok[None, :, :, None, :, None, :, None, :])
    ok = np.broadcast_to(ok[:, :, None], (3, N_QGRP, 2) + ok.shape[2:])
    ok = jnp.asarray(ok.reshape(3, N_QGRP, 2 * QGRP, NKEY).astype(np.float32))
    rpb5 = rpb.astype(F32).reshape(n_layers, NA_HEADS // 2, 2, n_r, n_c)
    by_col = jnp.einsum("lpjrc,gmnc->lpgjmrn", rpb5, c_hot, precision=lax.Precision.HIGHEST)
    by_col = by_col.reshape(n_layers * (NA_HEADS // 2) * N_QGRP, 2 * NA_COLS, n_r * KBAND_COLS)
    nb = KBAND_COLS // SUBLANES
    expand = np.zeros((TQ_ROWS, 2, 2, TQ_ROWS, SUBLANES, 2, 2, SUBLANES), np.float32)
    for q in range(TQ_ROWS):
        for j in range(2):
            for a in range(2):
                for x in range(SUBLANES):
                    expand[q, j, a, q, x, j, a, x] = 1.0
    expand = expand.reshape(TQ_ROWS, 2 * QGRP, 2 * NA_COLS)
    select = np.zeros((3, TQ_ROWS, n_r, nb, SUBLANES, 3, nb, TQ_ROWS, SUBLANES), np.float32)
    for v in range(3):
        for q in range(TQ_ROWS):
            for s in range(3):
                for y in range(TQ_ROWS):
                    for b in range(nb):
                        for z in range(SUBLANES):
                            select[v, q, r_off[v, q, s, y], b, z, s, b, y, z] = 1.0
    select = select.reshape(3, TQ_ROWS, n_r * KBAND_COLS, NKEY)
    lane_pad = -(n_r * KBAND_COLS) % LANES
    by_col = jnp.pad(by_col, ((0, 0), (0, 0), (0, lane_pad)))
    select = np.pad(select, ((0, 0), (0, 0), (0, lane_pad), (0, 0)))
    tab = _bias_table_call(by_col, jnp.asarray(expand, BF16), jnp.asarray(select, BF16), ok)
    return tab.reshape(n_layers, NA_HEADS // 2, N_QGRP, 3, 2 * QGRP, NKEY)


def _bias_table_kernel(u_ref, expand_ref, select_ref, ok_ref, out_ref):
    u = u_ref[...]
    parts = []
    for _ in range(3):
        piece = u.astype(BF16)
        parts.append(piece)
        u = u - piece.astype(F32)
    n_k = u.shape[1]
    pieces = jnp.concatenate(parts, axis=1)
    acc = [jnp.zeros((2 * QGRP, NKEY), F32) for _ in range(3)]
    for q in range(TQ_ROWS):
        rows = jnp.dot(expand_ref[q], pieces, preferred_element_type=F32).astype(BF16)
        rows = jnp.concatenate([rows[:, k * n_k:(k + 1) * n_k] for k in range(3)], axis=0)
        for v in range(3):
            sel = jnp.dot(rows, select_ref[v, q], preferred_element_type=F32)
            acc[v] = acc[v] + (sel[0:2 * QGRP] + sel[2 * QGRP:4 * QGRP] + sel[4 * QGRP:])
    for v in range(3):
        out_ref[v] = jnp.where(ok_ref[v, 0] > 0.0, acc[v], NEG_INF)


def _bias_table_call(by_col, expand, select, ok):
    n, n_u, n_k = by_col.shape
    return pl.pallas_call(
        _bias_table_kernel,
        grid=(n,),
        in_specs=[
            pl.BlockSpec((None, n_u, n_k), lambda i: (i, 0, 0)),
            pl.BlockSpec(expand.shape, lambda i: (0, 0, 0)),
            pl.BlockSpec(select.shape, lambda i: (0, 0, 0, 0)),
            pl.BlockSpec((3, 1, 2 * QGRP, NKEY), lambda i: (0, i % N_QGRP, 0, 0)),
        ],
        out_specs=pl.BlockSpec((None, 3, 2 * QGRP, NKEY), lambda i: (i, 0, 0, 0)),
        out_shape=jax.ShapeDtypeStruct((n, 3, 2 * QGRP, NKEY), F32),
        compiler_params=pltpu.CompilerParams(
            dimension_semantics=("arbitrary",), vmem_limit_bytes=VMEM_LIMIT),
        name="bias_table",
    )(by_col, expand, select, ok)


def _mixer_kernel(tiles_per_seq, alpha,
                  x_ref, q_ref, kp_ref, kc_ref, kn_ref, vp_ref, vc_ref, vn_ref,
                  rc_ref, rp_ref, rn_ref, bias_ref,
                  wpool_ref, pscale_ref, dw_ref, dwb_ref, clng_ref, clnb_ref, wpw_ref, bpw_ref,
                  wout_ref, bout_ref, lng_ref, lnb_ref, wr_ref, br_ref, tri_ref,
                  x1_ref, topi_ref, topp_ref, rank_ref, counts_ref,
                  halo_ref, shift_ref, base_ref):
    i = pl.program_id(0)
    ib = i % tiles_per_seq
    has_prev = ib > 0
    has_next = ib < tiles_per_seq - 1

    @pl.when(i == 0)
    def _():
        base_ref[...] = jnp.zeros_like(base_ref)

    halo_ref[0:HALO, :] = jnp.where(has_prev, rp_ref[...], 0.0)
    halo_ref[HALO:HALO + TQ, :] = rc_ref[...]
    halo_ref[HALO + TQ:, :] = jnp.where(has_next, rn_ref[...], 0.0)

    def fill_shifts():
        for b in range(1, SUBLANES):
            shift_ref[b - 1] = halo_ref[pl.ds(b, TQ + 2 * HALO - SUBLANES), 0:CONV_W]

    def window(start):
        a8, b = divmod(start, SUBLANES)
        if b == 0:
            return halo_ref[pl.ds(a8 * SUBLANES, TQ), 0:CONV_W]
        return shift_ref[b - 1, pl.ds(a8 * SUBLANES, TQ), :]

    def u_at(off):
        return halo_ref[pl.ds(HALO + off, TQ), 0:POOL_W]

    t_seq = ib * TQ + lax.broadcasted_iota(I32, (TQ, 1), 0)
    seq_len = tiles_per_seq * TQ
    group = lax.broadcasted_iota(I32, (1, POOL_W), 1) // POOL_GROUP_W
    u0 = u_at(0)
    acc = u0
    mean = jnp.zeros((TQ, POOL_W), F32)
    done = 0
    for g, w in enumerate(POOL_WINDOWS):
        half = w // 2
        for o in range(done + 1, half + 1):
            acc = acc + u_at(-o) + (u_at(o - 1) if o > 1 else 0.0)
        done = half
        cnt = (jnp.minimum(t_seq + half, seq_len) - jnp.maximum(t_seq - half, 0)).astype(F32)
        mean = jnp.where(group == g, acc * (1.0 / cnt), mean)
    d = (mean - u0).astype(BF16)
    y_pool = jnp.dot(d, wpool_ref[...], preferred_element_type=F32) * pscale_ref[...]
    mix = jnp.dot(y_pool.astype(BF16), wout_ref[0:POOL_W, :], preferred_element_type=F32)

    a = halo_ref[:, POOL_W:POOL_W + CONV_W]
    gate = halo_ref[:, POOL_W + CONV_W:]
    halo_ref[:, 0:CONV_W] = a * jax.nn.sigmoid(gate)
    fill_shifts()
    conv = jnp.zeros((TQ, CONV_W), F32) + dwb_ref[...]
    for k in range(CONV_K):
        conv = conv + window(HALO - CONV_K // 2 + k) * dw_ref[k:k + 1, :]
    hc = _layer_norm(conv, clng_ref[...], clnb_ref[...])
    hc = hc * jax.nn.sigmoid(hc)
    y_conv = jnp.dot(hc.astype(BF16), wpw_ref[...], preferred_element_type=F32) + bpw_ref[...]
    mix = mix + jnp.dot(y_conv.astype(BF16), wout_ref[POOL_W + NA_W:, :], preferred_element_type=F32)

    low = lax.broadcasted_iota(I32, (QGRP, LANES), 1) < NA_HEAD_DIM
    scale = NA_HEAD_DIM ** -0.5
    n_pairs = NA_HEADS // 2
    bands = [pl.ds(_band_start_cblk(g) * CBLK_ROWS, KBAND) for g in range(N_QGRP)]
    col = lambda p: slice(p * LANES, (p + 1) * LANES)
    blk_rows = 2 * QGRP
    for p0 in range(0, n_pairs, PAIR_BATCH):
        pairs = range(p0, p0 + PAIR_BATCH)
        s_blk = []
        for p in pairs:
            for g in range(N_QGRP):
                qp = q_ref[pl.ds(g * QGRP, QGRP), col(p)].astype(F32) * scale
                qs = jnp.concatenate([jnp.where(low, qp, 0.0), jnp.where(low, 0.0, qp)], axis=0).astype(BF16)
                kk = jnp.concatenate(
                    [kp_ref[bands[g], col(p)], kc_ref[bands[g], col(p)], kn_ref[bands[g], col(p)]], axis=0)
                s_blk.append(lax.dot_general(qs, kk, (((1,), (1,)), ((), ())), preferred_element_type=F32))
        bias = bias_ref[p0:p0 + PAIR_BATCH].reshape(PAIR_BATCH * N_QGRP * blk_rows, NKEY)
        s = jnp.concatenate(s_blk, axis=0) + bias
        m = jnp.max(s, axis=-1, keepdims=True)
        e = jnp.exp(s - m)
        rl = 1.0 / jnp.sum(e, axis=-1, keepdims=True)
        e = e.astype(BF16)
        for p in pairs:
            o_grp = []
            for g in range(N_QGRP):
                vv = jnp.concatenate(
                    [vp_ref[bands[g], col(p)], vc_ref[bands[g], col(p)], vn_ref[bands[g], col(p)]], axis=0)
                blk = (p - p0) * N_QGRP + g
                rows = slice(blk * blk_rows, (blk + 1) * blk_rows)
                o = jnp.dot(e[rows], vv, preferred_element_type=F32) * rl[rows]
                o_grp.append(jnp.where(low, o[0:QGRP], o[QGRP:]))
            o_pair = jnp.concatenate(
                [o_grp[cb // 2][(cb % 2) * CBLK_ROWS + r * SUBLANES:(cb % 2) * CBLK_ROWS + (r + 1) * SUBLANES]
                 for r in range(TQ_ROWS) for cb in range(N_CBLK)], axis=0)
            r0 = POOL_W + p * LANES
            mix = mix + jnp.dot(o_pair.astype(BF16), wout_ref[r0:r0 + LANES, :], preferred_element_type=F32)

    x1 = _layer_norm(alpha * x_ref[...] + mix + bout_ref[...], lng_ref[...], lnb_ref[...])
    x1_ref[...] = x1

    lt = lax.dot_general(wr_ref[...], x1.astype(BF16), (((1,), (1,)), ((), ())),
                         preferred_element_type=F32) + br_ref[...]
    eidx = lax.broadcasted_iota(I32, (N_EXPERTS, TQ), 0)
    work = lt
    vals, idxs = [], []
    for _ in range(TOP_K):
        mk = jnp.max(work, axis=0, keepdims=True)
        ik = jnp.min(jnp.where(work == mk, eidx, N_EXPERTS), axis=0, keepdims=True)
        vals.append(mk)
        idxs.append(ik)
        work = jnp.where(eidx == ik, -jnp.inf, work)
    ex = [jnp.exp(v - vals[0]) for v in vals]
    den = ex[0] + ex[1] + ex[2] + ex[3]
    topp_ref[...] = jnp.concatenate([e_ / den for e_ in ex], axis=0)
    topi_ref[...] = jnp.concatenate(idxs, axis=0)

    run = base_ref[...]
    ranks = []
    for k in range(TOP_K):
        hot = (eidx == idxs[k]).astype(F32)
        before = jnp.dot(hot.astype(BF16), tri_ref[...], preferred_element_type=F32)
        ranks.append(jnp.sum(hot * (run + before), axis=0, keepdims=True))
        run = run + jnp.sum(hot, axis=1, keepdims=True)
    rank_ref[...] = jnp.concatenate(ranks, axis=0).astype(I32)
    base_ref[...] = run
    counts_ref[...] = jnp.broadcast_to(run, counts_ref.shape).astype(I32)


def _mixer(x, qkv, rest, bias_tab, wpool_bd, pscale, dw, dwb, clng, clnb, wpw, bpw,
           wout, bout, lng, lnb, wr_t, br, tri, *, layer, seq, alpha):
    n = x.shape[0]
    nt = n // TQ
    tps = seq // TQ
    hb = TQ // HALO

    def prev_t(i):
        return jnp.where(i % tps == 0, i, i - 1)

    def next_t(i):
        return jnp.where(i % tps == tps - 1, i, i + 1)

    def variant(i):
        ib = i % tps
        return jnp.where(ib == 0, 0, jnp.where(ib == tps - 1, 2, 1))

    def const(shape):
        return pl.BlockSpec(shape, lambda i: tuple(0 for _ in shape))

    in_specs = [
        pl.BlockSpec((TQ, D_MODEL), lambda i: (i, 0)),
        pl.BlockSpec((TQ, NA_W), lambda i: (i, 0)),
        pl.BlockSpec((TQ, NA_W), lambda i: (prev_t(i), 1)),
        pl.BlockSpec((TQ, NA_W), lambda i: (i, 1)),
        pl.BlockSpec((TQ, NA_W), lambda i: (next_t(i), 1)),
        pl.BlockSpec((TQ, NA_W), lambda i: (prev_t(i), 2)),
        pl.BlockSpec((TQ, NA_W), lambda i: (i, 2)),
        pl.BlockSpec((TQ, NA_W), lambda i: (next_t(i), 2)),
        pl.BlockSpec((TQ, POOL_W + 2 * CONV_W), lambda i: (i, 0)),
        pl.BlockSpec((HALO, POOL_W + 2 * CONV_W), lambda i: (jnp.maximum(i * hb - 1, 0), 0)),
        pl.BlockSpec((HALO, POOL_W + 2 * CONV_W), lambda i: (jnp.minimum((i + 1) * hb, nt * hb - 1), 0)),
        pl.BlockSpec((None, NA_HEADS // 2, N_QGRP, 1, 2 * QGRP, NKEY),
                     lambda i: (layer, 0, 0, variant(i), 0, 0)),
        const((POOL_W, POOL_W)), const((1, POOL_W)),
        const((CONV_K, CONV_W)), const((1, CONV_W)), const((1, CONV_W)), const((1, CONV_W)),
        const((CONV_W, CONV_W)), const((1, CONV_W)),
        const((D_MODEL, D_MODEL)), const((1, D_MODEL)), const((1, D_MODEL)), const((1, D_MODEL)),
        const((N_EXPERTS, D_MODEL)), const((N_EXPERTS, 1)), const((TQ, TQ)),
    ]
    out_specs = [
        pl.BlockSpec((TQ, D_MODEL), lambda i: (i, 0)),
        pl.BlockSpec((TOP_K, TQ), lambda i: (0, i)),
        pl.BlockSpec((TOP_K, TQ), lambda i: (0, i)),
        pl.BlockSpec((TOP_K, TQ), lambda i: (0, i)),
        pl.BlockSpec((N_EXPERTS, LANES), lambda i: (0, 0)),
    ]
    out_shape = [
        jax.ShapeDtypeStruct((n, D_MODEL), F32),
        jax.ShapeDtypeStruct((TOP_K, n), I32),
        jax.ShapeDtypeStruct((TOP_K, n), F32),
        jax.ShapeDtypeStruct((TOP_K, n), I32),
        jax.ShapeDtypeStruct((N_EXPERTS, LANES), I32),
    ]
    return pl.pallas_call(
        functools.partial(_mixer_kernel, tps, alpha),
        grid=(nt,),
        in_specs=in_specs,
        out_specs=out_specs,
        out_shape=out_shape,
        scratch_shapes=[
            pltpu.VMEM((TQ + 2 * HALO, POOL_W + 2 * CONV_W), F32),
            pltpu.VMEM((SUBLANES - 1, TQ + 2 * HALO - SUBLANES, CONV_W), F32),
            pltpu.VMEM((N_EXPERTS, 1), F32),
        ],
        compiler_params=pltpu.CompilerParams(
            dimension_semantics=("arbitrary",), vmem_limit_bytes=VMEM_LIMIT),
        name="mixer",
    )(x, qkv, qkv, qkv, qkv, qkv, qkv, qkv, rest, rest, rest, bias_tab,
      wpool_bd, pscale, dw, dwb, clng, clnb, wpw, bpw, wout, bout, lng, lnb, wr_t, br, tri)


PAD_BLOCKS = (8, 16, 32, 64, 128, 256)
assert PAD_BLOCKS[0] == SUBLANES and TM_E == 2 * PAD_BLOCKS[-1]


def _dispatch_kernel(n_tiles, pad_lo_ref, pad_hi_ref, pos_ref, x_ref, xs_hbm, zero_ref, ring, sem, pad_sem):
    i = pl.program_id(0)

    def pad_fill(act):
        for e in range(N_EXPERTS):
            lo = pad_lo_ref[e]
            hi = pad_hi_ref[e]
            head_end = jnp.minimum((lo + (SUBLANES - 1)) & -SUBLANES, hi)
            for j in range(SUBLANES - 1):
                @pl.when(lo + j < head_end)
                def _(j=j, lo=lo):
                    act(pltpu.make_async_copy(
                        zero_ref.at[pl.ds(0, 1), :], xs_hbm.at[pl.ds(lo + j, 1), :], pad_sem))
            a = head_end
            for b in PAD_BLOCKS:
                take = (a & b) != 0

                @pl.when(take)
                def _(a=a, b=b):
                    act(pltpu.make_async_copy(
                        zero_ref.at[pl.ds(0, b), :], xs_hbm.at[pl.ds(pl.multiple_of(a, SUBLANES), b), :], pad_sem))
                a = jnp.where(take, a + b, a)
        blk = PAD_BLOCKS[-1]

        def tail(c, carry):
            act(pltpu.make_async_copy(
                zero_ref, xs_hbm.at[pl.ds(pl.multiple_of(c * blk, blk), blk), :], pad_sem))
            return carry
        lax.fori_loop(pad_hi_ref[N_EXPERTS - 1] // blk, xs_hbm.shape[0] // blk, tail, 0)

    @pl.when(i == 0)
    def _():
        zero_ref[...] = jnp.zeros_like(zero_ref)
        pad_fill(lambda c: c.start())

    def wait_rows(slot):
        for k in range(TOP_K):
            pltpu.make_async_copy(ring.at[slot], xs_hbm.at[pl.ds(0, TD), :], sem.at[slot]).wait()

    for slot in range(2):
        @pl.when(i % 2 == slot)
        def _(slot=slot):
            ring[slot] = x_ref[...]
            for k in range(TOP_K):
                for r in range(TD):
                    pltpu.make_async_copy(
                        ring.at[slot, pl.ds(r, 1), :], xs_hbm.at[pl.ds(pos_ref[k, r], 1), :],
                        sem.at[slot]).start(priority=r % 2)

            @pl.when(i > 0)
            def _():
                wait_rows(1 - slot)

            @pl.when(i == n_tiles - 1)
            def _():
                wait_rows(slot)

    @pl.when(i == 0)
    def _():
        pad_fill(lambda c: c.wait())


def _dispatch(x1, pos, pad_lo, pad_hi, m_pad):
    n = x1.shape[0]
    grid_spec = pltpu.PrefetchScalarGridSpec(
        num_scalar_prefetch=2,
        grid=(n // TD,),
        in_specs=[
            pl.BlockSpec((TOP_K, TD), lambda i, *_: (0, i), memory_space=pltpu.SMEM),
            pl.BlockSpec((TD, D_MODEL), lambda i, *_: (i, 0)),
        ],
        out_specs=pl.BlockSpec(memory_space=pl.ANY),
        scratch_shapes=[
            pltpu.VMEM((PAD_BLOCKS[-1], D_MODEL), F32),
            pltpu.VMEM((2, TD, D_MODEL), F32),
            pltpu.SemaphoreType.DMA((2,)),
            pltpu.SemaphoreType.DMA(()),
        ],
    )
    return pl.pallas_call(
        functools.partial(_dispatch_kernel, n // TD),
        grid_spec=grid_spec,
        out_shape=jax.ShapeDtypeStruct((m_pad, D_MODEL), F32),
        compiler_params=pltpu.CompilerParams(dimension_semantics=("arbitrary",)),
        name="dispatch",
    )(pad_lo, pad_hi, pos, x1)


def _experts_kernel(layer, te_ref, na_ref, first_ref, slot_ref, nxt_ref, rows_ref,
                    xs_ref, bg_ref, bu_ref, bd_ref, wg_hbm, wu_hbm, wd_hbm,
                    y_ref, wbuf, wg_bf, wu_bf, wd_bf, sem):
    t = pl.program_id(0)
    active = t < na_ref[0]
    e = te_ref[t]
    s = slot_ref[t]

    def fetch(expert, slot):
        return [pltpu.make_async_copy(w.at[layer, expert], wbuf.at[j, slot], sem.at[j, slot])
                for j, w in enumerate((wg_hbm, wu_hbm, wd_hbm))]

    @pl.when(t == 0)
    def _():
        for c in fetch(e, s):
            c.start()

    @pl.when(jnp.logical_and(active, first_ref[t] == 1))
    def _():
        for c in fetch(e, s):
            c.wait()

        @pl.when(nxt_ref[t] != e)
        def _():
            for c in fetch(nxt_ref[t], 1 - s):
                c.start()

        wg_bf[...] = wbuf[0, s].astype(BF16)
        wu_bf[...] = wbuf[1, s].astype(BF16)
        wd_bf[...] = wbuf[2, s].astype(BF16)

    for chunks in range(1, TM_E // TM_CHUNK + 1):
        m = chunks * TM_CHUNK

        @pl.when(jnp.logical_and(active, rows_ref[t] == chunks))
        def _(m=m):
            x = xs_ref[0:m, :].astype(BF16)
            g = jnp.minimum(jnp.dot(x, wg_bf[...], preferred_element_type=F32) + bg_ref[...], SWIGLU_LIMIT)
            u = jnp.clip(jnp.dot(x, wu_bf[...], preferred_element_type=F32) + bu_ref[...],
                         -SWIGLU_LIMIT, SWIGLU_LIMIT)
            act = (u + 1.0) * g * jax.nn.sigmoid(SWIGLU_ALPHA * g)
            y_ref[0:m, :] = jnp.dot(act.astype(BF16), wd_bf[...], preferred_element_type=F32) + bd_ref[...]
            if m < TM_E:
                y_ref[m:, :] = jnp.zeros((TM_E - m, D_MODEL), F32)

    @pl.when(jnp.logical_not(active))
    def _():
        y_ref[...] = jnp.zeros_like(y_ref)


def _experts(layer, tile_expert, n_active, tile_first, tile_slot, tile_next, tile_chunks,
             xs, wg, bg, wu, bu, wd, bd):
    m_pad = xs.shape[0]
    n_tiles = m_pad // TM_E

    def xmap(t, te, na, *_):
        return (jnp.minimum(t, na[0] - 1), 0)

    def bmap(t, te, *_):
        return (layer, te[t], 0, 0)

    grid_spec = pltpu.PrefetchScalarGridSpec(
        num_scalar_prefetch=6,
        grid=(n_tiles,),
        in_specs=[
            pl.BlockSpec((TM_E, D_MODEL), xmap),
            pl.BlockSpec((None, None, 1, D_MODEL), bmap),
            pl.BlockSpec((None, None, 1, D_MODEL), bmap),
            pl.BlockSpec((None, None, 1, D_MODEL), bmap),
            pl.BlockSpec(memory_space=pl.ANY),
            pl.BlockSpec(memory_space=pl.ANY),
            pl.BlockSpec(memory_space=pl.ANY),
        ],
        out_specs=pl.BlockSpec((TM_E, D_MODEL), lambda t, *_: (t, 0)),
        scratch_shapes=[
            pltpu.VMEM((3, 2, D_MODEL, D_MODEL), F32),
            pltpu.VMEM((D_MODEL, D_MODEL), BF16),
            pltpu.VMEM((D_MODEL, D_MODEL), BF16),
            pltpu.VMEM((D_MODEL, D_MODEL), BF16),
            pltpu.SemaphoreType.DMA((3, 2)),
        ],
    )
    return pl.pallas_call(
        functools.partial(_experts_kernel, layer),
        grid_spec=grid_spec,
        out_shape=jax.ShapeDtypeStruct((m_pad, D_MODEL), F32),
        compiler_params=pltpu.CompilerParams(
            dimension_semantics=("arbitrary",), vmem_limit_bytes=VMEM_LIMIT),
        name="experts",
    )(tile_expert, n_active, tile_first, tile_slot, tile_next, tile_chunks, xs, bg, bu, bd, wg, wu, wd)


def _combine_kernel(n_tiles, alpha, pos_ref, x1_ref, gates_ref, lng_ref, lnb_ref, ys_hbm, out_ref, buf, sem):
    j = pl.program_id(0)

    for slot in range(2):
        @pl.when(jnp.logical_and(j < n_tiles, j % 2 == slot))
        def _(slot=slot):
            for k in range(TOP_K):
                for r in range(TD):
                    pltpu.make_async_copy(
                        ys_hbm.at[pl.ds(pos_ref[k, r], 1), :], buf.at[slot, k, pl.ds(r, 1), :],
                        sem.at[slot]).start(priority=r % 2)

    for slot in range(2):
        @pl.when(jnp.logical_and(j > 0, (j - 1) % 2 == slot))
        def _(slot=slot):
            for k in range(TOP_K):
                pltpu.make_async_copy(ys_hbm.at[pl.ds(0, TD), :], buf.at[slot, k], sem.at[slot]).wait()
            gates = gates_ref[...]
            ffn = buf[slot, 0] * gates[:, 0:1]
            for k in range(1, TOP_K):
                ffn = ffn + buf[slot, k] * gates[:, k:k + 1]
            out_ref[...] = _layer_norm(alpha * x1_ref[...] + ffn, lng_ref[...], lnb_ref[...])


def _combine(pos, x1, gates_tm, lng, lnb, ys, *, alpha):
    n = x1.shape[0]
    n_tiles = n // TD
    req = lambda j: jnp.minimum(j, n_tiles - 1)
    fin = lambda j: jnp.maximum(j - 1, 0)
    return pl.pallas_call(
        functools.partial(_combine_kernel, n_tiles, alpha),
        grid=(n_tiles + 1,),
        in_specs=[
            pl.BlockSpec((TOP_K, TD), lambda j: (0, req(j)), memory_space=pltpu.SMEM),
            pl.BlockSpec((TD, D_MODEL), lambda j: (fin(j), 0)),
            pl.BlockSpec((TD, TOP_K), lambda j: (fin(j), 0)),
            pl.BlockSpec((1, D_MODEL), lambda j: (0, 0)),
            pl.BlockSpec((1, D_MODEL), lambda j: (0, 0)),
            pl.BlockSpec(memory_space=pl.ANY),
        ],
        out_specs=pl.BlockSpec((TD, D_MODEL), lambda j: (fin(j), 0)),
        out_shape=jax.ShapeDtypeStruct((n, D_MODEL), F32),
        scratch_shapes=[pltpu.VMEM((2, TOP_K, TD, D_MODEL), F32), pltpu.SemaphoreType.DMA((2,))],
        compiler_params=pltpu.CompilerParams(
            dimension_semantics=("arbitrary",), vmem_limit_bytes=VMEM_LIMIT),
        name="combine",
    )(pos, x1, gates_tm, lng, lnb, ys)


def kernel(x, w_in, b_in, w_pool, pool_scale, rpb, conv_dw, conv_dw_b, conv_ln_g, conv_ln_b,
           w_conv_pw, b_conv_pw, w_out, b_out, ln1_g, ln1_b, w_router, b_router,
           w_gate, b_gate, w_up, b_up, w_down, b_down, ln2_g, ln2_b):
    batch, seq, d = x.shape
    depth = w_in.shape[0]
    n = batch * seq
    rows = seq // GRID_W
    alpha = (2.0 * depth) ** 0.25
    off_q, off_k, off_v = POOL_W, POOL_W + NA_W, POOL_W + 2 * NA_W
    off_ca = off_v + NA_W
    m_pad = n * TOP_K + N_EXPERTS * TM_E
    n_tiles = m_pad // TM_E

    tri = (np.arange(TQ)[:, None] < np.arange(TQ)[None, :]).astype(np.float32)
    tri = jnp.asarray(tri, BF16)
    row2 = lambda v: v.reshape(1, -1)
    bias_tabs = _attn_bias_tables(rpb, rows)
    experts_iota = jnp.arange(N_EXPERTS, dtype=I32)
    tile_starts = jnp.arange(n_tiles, dtype=I32) * TM_E
    b_gate4 = b_gate.reshape(depth, N_EXPERTS, 1, d)
    b_up4 = b_up.reshape(depth, N_EXPERTS, 1, d)
    b_down4 = b_down.reshape(depth, N_EXPERTS, 1, d)

    h = x.reshape(n, d)
    for l in range(depth):
        wqkv = w_in[l][:, off_q:off_ca].astype(BF16)
        bqkv = row2(b_in[l][off_q:off_ca])
        wrest = jnp.concatenate([w_in[l][:, :off_q], w_in[l][:, off_ca:]], axis=1).astype(BF16)
        brest = row2(jnp.concatenate([b_in[l][:off_q], b_in[l][off_ca:]]))
        qkv, rest = _inproj(h, wqkv, bqkv, wrest, brest)

        wpool_bd = jax.scipy.linalg.block_diag(*[w_pool[l][g] for g in range(len(POOL_WINDOWS))]).astype(BF16)
        x1, top_i, top_p, rank, counts = _mixer(
            h, qkv, rest, bias_tabs, wpool_bd, row2(pool_scale[l]),
            conv_dw[l].reshape(CONV_K, CONV_W), row2(conv_dw_b[l]), row2(conv_ln_g[l]), row2(conv_ln_b[l]),
            w_conv_pw[l].astype(BF16), row2(b_conv_pw[l]),
            w_out[l].astype(BF16), row2(b_out[l]), row2(ln1_g[l]), row2(ln1_b[l]),
            w_router[l].T.astype(BF16), b_router[l].reshape(N_EXPERTS, 1), tri,
            layer=l, seq=seq, alpha=alpha)

        cnt = counts[:, 0]
        cpad = ((cnt + TM_E - 1) // TM_E) * TM_E
        ends = jnp.cumsum(cpad)
        off = ends - cpad
        hot = top_i[None] == experts_iota[:, None, None]
        pos = rank + jnp.sum(jnp.where(hot, off[:, None, None], 0), axis=0)
        n_active = (ends[-1] // TM_E).astype(I32)
        last_start = jnp.minimum(tile_starts, ends[-1] - TM_E)
        tile_expert = jnp.sum((ends[None, :] <= last_start[:, None]).astype(I32), axis=1)
        owns = cnt > 0
        e_slot = (jnp.cumsum(owns.astype(I32)) - 1) % 2
        later = jnp.logical_and(experts_iota[None, :] > experts_iota[:, None], owns[None, :])
        e_next = jnp.min(jnp.where(later, experts_iota[None, :], N_EXPERTS), axis=1)
        e_next = jnp.where(e_next == N_EXPERTS, experts_iota, e_next)
        tile_hot = tile_expert[:, None] == experts_iota[None, :]
        pick = lambda v: jnp.sum(jnp.where(tile_hot, v[None, :], 0), axis=1).astype(I32)
        tile_first = (pick(off) == tile_starts).astype(I32)
        tile_slot = pick(e_slot)
        tile_next = pick(e_next)

        xs = _dispatch(x1, pos, (off + cnt).astype(I32), ends.astype(I32), m_pad)
        real_rows = jnp.clip(pick(off + cnt) - tile_starts, 0, TM_E)
        tile_chunks = (real_rows + TM_CHUNK - 1) // TM_CHUNK
        ys = _experts(l, tile_expert, n_active.reshape(1), tile_first, tile_slot, tile_next, tile_chunks, xs,
                      w_gate, b_gate4, w_up, b_up4, w_down, b_down4)
        h = _combine(pos, x1, top_p.T, row2(ln2_g[l]), row2(ln2_b[l]), ys, alpha=alpha)
    return h.reshape(batch, seq, d)
```

```python
import functools

import jax
import jax.numpy as jnp
import numpy as np
from jax import lax
from jax.experimental import pallas as pl
from jax.experimental.pallas import tpu as pltpu

F32 = jnp.float32
BF16 = jnp.bfloat16
I32 = jnp.int32

D_MODEL = 1024
GRID_W = 64
POOL_W = 256
POOL_WINDOWS = (2, 4, 8, 16)
POOL_GROUP_W = 64
NA_W = 512
NA_HEADS = 8
NA_HEAD_DIM = 64
NA_ROWS = 8
NA_COLS = 16
CONV_W = 256
CONV_K = 31
N_EXPERTS = 32
TOP_K = 4
SWIGLU_ALPHA = 1.702
SWIGLU_LIMIT = 7.0
LN_EPS = 1e-5
NEG_INF = -1e30

LANES = 128
TM_IN = 512
TQ = 256
TQ_ROWS = TQ // GRID_W
HALO = 16
SUBLANES = 8
N_CBLK = GRID_W // SUBLANES
CBLK_ROWS = TQ_ROWS * SUBLANES
N_QGRP = GRID_W // NA_COLS
QGRP = TQ // N_QGRP
KBAND_COLS = 2 * NA_COLS
KBAND = TQ_ROWS * KBAND_COLS
NKEY = 3 * KBAND
PAIR_BATCH = 2
TD = 256
TM_E = 512
TM_CHUNK = 128
V7X_VMEM_BYTES = 64 * 1024 * 1024
VMEM_LIMIT = V7X_VMEM_BYTES * 7 // 8


def _layer_norm(v, g, b):
    mu = jnp.mean(v, axis=-1, keepdims=True)
    c = v - mu
    var = jnp.mean(c * c, axis=-1, keepdims=True)
    return c * lax.rsqrt(var + LN_EPS) * g + b


def _band_start_cblk(g):
    return min(max(2 * g - 1, 0), N_CBLK - KBAND_COLS // SUBLANES)


def _stored_row_groups():
    return [r * N_CBLK + cb for cb in range(N_CBLK) for r in range(TQ_ROWS)]


def _inproj_kernel(x_ref, wqkv_ref, bqkv_ref, wrest_ref, brest_ref, qkv_ref, rest_ref):
    x = x_ref[...].astype(BF16)
    rest_ref[...] = jnp.dot(x, wrest_ref[...], preferred_element_type=F32) + brest_ref[...]
    groups = [x_ref[pl.ds(tile * TQ + gn * SUBLANES, SUBLANES), :]
              for tile in range(TM_IN // TQ) for gn in _stored_row_groups()]
    xp = jnp.concatenate(groups, axis=0).astype(BF16)
    qkv = jnp.dot(xp, wqkv_ref[...], preferred_element_type=F32) + bqkv_ref[...]
    qkv_ref[...] = qkv.astype(BF16)


def _inproj(x, wqkv, bqkv, wrest, brest):
    n = x.shape[0]
    return pl.pallas_call(
        _inproj_kernel,
        grid=(n // TM_IN,),
        in_specs=[
            pl.BlockSpec((TM_IN, D_MODEL), lambda i: (i, 0)),
            pl.BlockSpec((D_MODEL, 3 * NA_W), lambda i: (0, 0)),
            pl.BlockSpec((1, 3 * NA_W), lambda i: (0, 0)),
            pl.BlockSpec((D_MODEL, POOL_W + 2 * CONV_W), lambda i: (0, 0)),
            pl.BlockSpec((1, POOL_W + 2 * CONV_W), lambda i: (0, 0)),
        ],
        out_specs=[
            pl.BlockSpec((TM_IN, 3 * NA_W), lambda i: (i, 0)),
            pl.BlockSpec((TM_IN, POOL_W + 2 * CONV_W), lambda i: (i, 0)),
        ],
        out_shape=[
            jax.ShapeDtypeStruct((n, 3 * NA_W), BF16),
            jax.ShapeDtypeStruct((n, POOL_W + 2 * CONV_W), F32),
        ],
        compiler_params=pltpu.CompilerParams(
            dimension_semantics=("arbitrary",), vmem_limit_bytes=VMEM_LIMIT),
        name="inproj",
    )(x, wqkv, bqkv, wrest, brest)


def _attn_bias_tables(rpb, rows):
    kr_win = min(NA_ROWS, rows)
    n_tiles = rows // TQ_ROWS
    n_r, n_c = 2 * NA_ROWS - 1, 2 * NA_COLS - 1
    n_layers = rpb.shape[0]
    c_hots, c_oks = [], []
    for g in range(N_QGRP):
        qc = g * NA_COLS + np.arange(NA_COLS)
        kc = _band_start_cblk(g) * SUBLANES + np.arange(KBAND_COLS)
        sc = np.clip(qc - NA_COLS // 2, 0, GRID_W - NA_COLS)[:, None]
        c_oks.append((kc[None] >= sc) & (kc[None] < sc + NA_COLS))
        c_off = np.clip(kc[None] - qc[:, None] + NA_COLS - 1, 0, n_c - 1)
        c_hots.append((c_off[..., None] == np.arange(n_c)).astype(np.float32))
    r_hots, r_oks = [], []
    for tile in (0, 1, n_tiles - 1):
        qr = tile * TQ_ROWS + np.arange(TQ_ROWS)
        slot_tile = np.array([tile - 1, tile, tile + 1])
        kr = slot_tile[:, None] * TQ_ROWS + np.arange(TQ_ROWS)[None, :]
        slot_ok = ((slot_tile >= 0) & (slot_tile < n_tiles))[:, None]
        sr = np.clip(qr - kr_win // 2, 0, rows - kr_win)[:, None, None]
        r_oks.append((kr[None] >= sr) & (kr[None] < sr + kr_win) & slot_ok[None])
        r_off = np.clip(kr[None] - qr[:, None, None] + NA_ROWS - 1, 0, n_r - 1)
        r_hots.append((r_off[..., None] == np.arange(n_r)).astype(np.float32))
    c_hot = jnp.asarray(np.stack(c_hots))
    r_off = np.stack([np.argmax(h, axis=-1) for h in r_hots])
    c_ok = np.stack(c_oks).reshape(N_QGRP, 2, SUBLANES, KBAND_COLS // SUBLANES, SUBLANES)
    r_ok = np.stack(r_oks)
    ok = (r_ok[:, None, None, :, None, :, None, :, None]
          & c_ok[None, :, :, None, :, None, :, None, :])
    ok = np.broadcast_to(ok[:, :, None], (3, N_QGRP, 2) + ok.shape[2:])
    ok = jnp.asarray(ok.reshape(3, N_QGRP, 2 * QGRP, NKEY).astype(np.float32))
    rpb5 = rpb.astype(F32).reshape(n_layers, NA_HEADS // 2, 2, n_r, n_c)
    by_col = jnp.einsum("lpjrc,gmnc->lpgjmrn", rpb5, c_hot, precision=lax.Precision.HIGHEST)
    by_col = by_col.reshape(n_layers * (NA_HEADS // 2) * N_QGRP, 2 * NA_COLS, n_r * KBAND_COLS)
    nb = KBAND_COLS // SUBLANES
    expand = np.zeros((TQ_ROWS, 2, 2, TQ_ROWS, SUBLANES, 2, 2, SUBLANES), np.float32)
    for q in range(TQ_ROWS):
        for j in range(2):
            for a in range(2):
                for x in range(SUBLANES):
                    expand[q, j, a, q, x, j, a, x] = 1.0
    expand = expand.reshape(TQ_ROWS, 2 * QGRP, 2 * NA_COLS)
    select = np.zeros((3, TQ_ROWS, n_r, nb, SUBLANES, 3, nb, TQ_ROWS, SUBLANES), np.float32)
    for v in range(3):
        for q in range(TQ_ROWS):
            for s in range(3):
                for y in range(TQ_ROWS):
                    for b in range(nb):
                        for z in range(SUBLANES):
                            select[v, q, r_off[v, q, s, y], b, z, s, b, y, z] = 1.0
    select = select.reshape(3, TQ_ROWS, n_r * KBAND_COLS, NKEY)
    lane_pad = -(n_r * KBAND_COLS) % LANES
    by_col = jnp.pad(by_col, ((0, 0), (0, 0), (0, lane_pad)))
    select = np.pad(select, ((0, 0), (0, 0), (0, lane_pad), (0, 0)))
    tab = _bias_table_call(by_col, jnp.asarray(expand, BF16), jnp.asarray(select, BF16), ok)
    return tab.reshape(n_layers, NA_HEADS // 2, N_QGRP, 3, 2 * QGRP, NKEY)


def _bias_table_kernel(u_ref, expand_ref, select_ref, ok_ref, out_ref):
    u = u_ref[...]
    parts = []
    for _ in range(3):
        piece = u.astype(BF16)
        parts.append(piece)
        u = u - piece.astype(F32)
    n_k = u.shape[1]
    pieces = jnp.concatenate(parts, axis=1)
    acc = [jnp.zeros((2 * QGRP, NKEY), F32) for _ in range(3)]
    for q in range(TQ_ROWS):
        rows = jnp.dot(expand_ref[q], pieces, preferred_element_type=F32).astype(BF16)
        rows = jnp.concatenate([rows[:, k * n_k:(k + 1) * n_k] for k in range(3)], axis=0)
        for v in range(3):
            sel = jnp.dot(rows, select_ref[v, q], preferred_element_type=F32)
            acc[v] = acc[v] + (sel[0:2 * QGRP] + sel[2 * QGRP:4 * QGRP] + sel[4 * QGRP:])
    for v in range(3):
        out_ref[v] = jnp.where(ok_ref[v, 0] > 0.0, acc[v], NEG_INF)


def _bias_table_call(by_col, expand, select, ok):
    n, n_u, n_k = by_col.shape
    return pl.pallas_call(
        _bias_table_kernel,
        grid=(n,),
        in_specs=[
            pl.BlockSpec((None, n_u, n_k), lambda i: (i, 0, 0)),
            pl.BlockSpec(expand.shape, lambda i: (0, 0, 0)),
            pl.BlockSpec(select.shape, lambda i: (0, 0, 0, 0)),
            pl.BlockSpec((3, 1, 2 * QGRP, NKEY), lambda i: (0, i % N_QGRP, 0, 0)),
        ],
        out_specs=pl.BlockSpec((None, 3, 2 * QGRP, NKEY), lambda i: (i, 0, 0, 0)),
        out_shape=jax.ShapeDtypeStruct((n, 3, 2 * QGRP, NKEY), F32),
        compiler_params=pltpu.CompilerParams(
            dimension_semantics=("arbitrary",), vmem_limit_bytes=VMEM_LIMIT),
        name="bias_table",
    )(by_col, expand, select, ok)


def _mixer_kernel(tiles_per_seq, alpha,
                  x_ref, q_ref, kp_ref, kc_ref, kn_ref, vp_ref, vc_ref, vn_ref,
                  rc_ref, rp_ref, rn_ref, bias_ref,
                  wpool_ref, pscale_ref, dw_ref, dwb_ref, clng_ref, clnb_ref, wpw_ref, bpw_ref,
                  wout_ref, bout_ref, lng_ref, lnb_ref, wr_ref, br_ref, tri_ref,
                  x1_ref, topi_ref, topp_ref, rank_ref, counts_ref,
                  halo_ref, shift_ref, base_ref):
    i = pl.program_id(0)
    ib = i % tiles_per_seq
    has_prev = ib > 0
    has_next = ib < tiles_per_seq - 1

    @pl.when(i == 0)
    def _():
        base_ref[...] = jnp.zeros_like(base_ref)

    halo_ref[0:HALO, :] = jnp.where(has_prev, rp_ref[...], 0.0)
    halo_ref[HALO:HALO + TQ, :] = rc_ref[...]
    halo_ref[HALO + TQ:, :] = jnp.where(has_next, rn_ref[...], 0.0)

    def fill_shifts():
        for b in range(1, SUBLANES):
            shift_ref[b - 1] = halo_ref[pl.ds(b, TQ + 2 * HALO - SUBLANES), 0:CONV_W]

    def window(start):
        a8, b = divmod(start, SUBLANES)
        if b == 0:
            return halo_ref[pl.ds(a8 * SUBLANES, TQ), 0:CONV_W]
        return shift_ref[b - 1, pl.ds(a8 * SUBLANES, TQ), :]

    def pool_and_conv():
        def u_at(off):
            return halo_ref[pl.ds(HALO + off, TQ), 0:POOL_W]

        t_seq = ib * TQ + lax.broadcasted_iota(I32, (TQ, 1), 0)
        seq_len = tiles_per_seq * TQ
        group = lax.broadcasted_iota(I32, (1, POOL_W), 1) // POOL_GROUP_W
        u0 = u_at(0)
        acc = u0
        mean = jnp.zeros((TQ, POOL_W), F32)
        done = 0
        for g, w in enumerate(POOL_WINDOWS):
            half = w // 2
            for o in range(done + 1, half + 1):
                acc = acc + u_at(-o) + (u_at(o - 1) if o > 1 else 0.0)
            done = half
            cnt = (jnp.minimum(t_seq + half, seq_len) - jnp.maximum(t_seq - half, 0)).astype(F32)
            mean = jnp.where(group == g, acc * (1.0 / cnt), mean)
        d = (mean - u0).astype(BF16)
        y_pool = jnp.dot(d, wpool_ref[...], preferred_element_type=F32) * pscale_ref[...]
        part = jnp.dot(y_pool.astype(BF16), wout_ref[0:POOL_W, :], preferred_element_type=F32)

        a = halo_ref[:, POOL_W:POOL_W + CONV_W]
        gate = halo_ref[:, POOL_W + CONV_W:]
        halo_ref[:, 0:CONV_W] = a * jax.nn.sigmoid(gate)
        fill_shifts()
        conv = jnp.zeros((TQ, CONV_W), F32) + dwb_ref[...]
        for k in range(CONV_K):
            conv = conv + window(HALO - CONV_K // 2 + k) * dw_ref[k:k + 1, :]
        hc = _layer_norm(conv, clng_ref[...], clnb_ref[...])
        hc = hc * jax.nn.sigmoid(hc)
        y_conv = jnp.dot(hc.astype(BF16), wpw_ref[...], preferred_element_type=F32) + bpw_ref[...]
        return part + jnp.dot(y_conv.astype(BF16), wout_ref[POOL_W + NA_W:, :], preferred_element_type=F32)

    mix = jnp.zeros((TQ, D_MODEL), F32)

    low = lax.broadcasted_iota(I32, (QGRP, LANES), 1) < NA_HEAD_DIM
    scale = NA_HEAD_DIM ** -0.5
    n_pairs = NA_HEADS // 2
    bands = [pl.ds(_band_start_cblk(g) * CBLK_ROWS, KBAND) for g in range(N_QGRP)]
    col = lambda p: slice(p * LANES, (p + 1) * LANES)
    blk_rows = 2 * QGRP
    for p0 in range(0, n_pairs, PAIR_BATCH):
        pairs = range(p0, p0 + PAIR_BATCH)
        s_blk = []
        for p in pairs:
            for g in range(N_QGRP):
                qp = q_ref[pl.ds(g * QGRP, QGRP), col(p)].astype(F32) * scale
                qs = jnp.concatenate([jnp.where(low, qp, 0.0), jnp.where(low, 0.0, qp)], axis=0).astype(BF16)
                kk = jnp.concatenate(
                    [kp_ref[bands[g], col(p)], kc_ref[bands[g], col(p)], kn_ref[bands[g], col(p)]], axis=0)
                s_blk.append(lax.dot_general(qs, kk, (((1,), (1,)), ((), ())), preferred_element_type=F32))
        bias = bias_ref[p0:p0 + PAIR_BATCH].reshape(PAIR_BATCH * N_QGRP * blk_rows, NKEY)
        s = jnp.concatenate(s_blk, axis=0) + bias
        m = jnp.max(s, axis=-1, keepdims=True)
        e = jnp.exp(s - m)
        rl = 1.0 / jnp.sum(e, axis=-1, keepdims=True)
        e = e.astype(BF16)
        for p in pairs:
            o_grp = []
            for g in range(N_QGRP):
                vv = jnp.concatenate(
                    [vp_ref[bands[g], col(p)], vc_ref[bands[g], col(p)], vn_ref[bands[g], col(p)]], axis=0)
                blk = (p - p0) * N_QGRP + g
                rows = slice(blk * blk_rows, (blk + 1) * blk_rows)
                o = jnp.dot(e[rows], vv, preferred_element_type=F32) * rl[rows]
                o_grp.append(jnp.where(low, o[0:QGRP], o[QGRP:]))
            o_pair = jnp.concatenate(
                [o_grp[cb // 2][(cb % 2) * CBLK_ROWS + r * SUBLANES:(cb % 2) * CBLK_ROWS + (r + 1) * SUBLANES]
                 for r in range(TQ_ROWS) for cb in range(N_CBLK)], axis=0)
            r0 = POOL_W + p * LANES
            mix = mix + jnp.dot(o_pair.astype(BF16), wout_ref[r0:r0 + LANES, :], preferred_element_type=F32)
    mix = mix + pool_and_conv()

    x1 = _layer_norm(alpha * x_ref[...] + mix + bout_ref[...], lng_ref[...], lnb_ref[...])
    x1_ref[...] = x1

    lt = lax.dot_general(wr_ref[...], x1.astype(BF16), (((1,), (1,)), ((), ())),
                         preferred_element_type=F32) + br_ref[...]
    eidx = lax.broadcasted_iota(I32, (N_EXPERTS, TQ), 0)
    work = lt
    vals, idxs = [], []
    for _ in range(TOP_K):
        mk = jnp.max(work, axis=0, keepdims=True)
        ik = jnp.min(jnp.where(work == mk, eidx, N_EXPERTS), axis=0, keepdims=True)
        vals.append(mk)
        idxs.append(ik)
        work = jnp.where(eidx == ik, -jnp.inf, work)
    ex = [jnp.exp(v - vals[0]) for v in vals]
    den = ex[0] + ex[1] + ex[2] + ex[3]
    topp_ref[...] = jnp.concatenate([e_ / den for e_ in ex], axis=0)
    topi_ref[...] = jnp.concatenate(idxs, axis=0)

    run = base_ref[...]
    ranks = []
    for k in range(TOP_K):
        hot = (eidx == idxs[k]).astype(F32)
        before = jnp.dot(hot.astype(BF16), tri_ref[...], preferred_element_type=F32)
        ranks.append(jnp.sum(hot * (run + before), axis=0, keepdims=True))
        run = run + jnp.sum(hot, axis=1, keepdims=True)
    rank_ref[...] = jnp.concatenate(ranks, axis=0).astype(I32)
    base_ref[...] = run
    counts_ref[...] = jnp.broadcast_to(run, counts_ref.shape).astype(I32)


def _mixer(x, qkv, rest, bias_tab, wpool_bd, pscale, dw, dwb, clng, clnb, wpw, bpw,
           wout, bout, lng, lnb, wr_t, br, tri, *, layer, seq, alpha):
    n = x.shape[0]
    nt = n // TQ
    tps = seq // TQ
    hb = TQ // HALO

    def prev_t(i):
        return jnp.where(i % tps == 0, i, i - 1)

    def next_t(i):
        return jnp.where(i % tps == tps - 1, i, i + 1)

    def variant(i):
        ib = i % tps
        return jnp.where(ib == 0, 0, jnp.where(ib == tps - 1, 2, 1))

    def const(shape):
        return pl.BlockSpec(shape, lambda i: tuple(0 for _ in shape))

    in_specs = [
        pl.BlockSpec((TQ, D_MODEL), lambda i: (i, 0)),
        pl.BlockSpec((TQ, NA_W), lambda i: (i, 0)),
        pl.BlockSpec((TQ, NA_W), lambda i: (prev_t(i), 1)),
        pl.BlockSpec((TQ, NA_W), lambda i: (i, 1)),
        pl.BlockSpec((TQ, NA_W), lambda i: (next_t(i), 1)),
        pl.BlockSpec((TQ, NA_W), lambda i: (prev_t(i), 2)),
        pl.BlockSpec((TQ, NA_W), lambda i: (i, 2)),
        pl.BlockSpec((TQ, NA_W), lambda i: (next_t(i), 2)),
        pl.BlockSpec((TQ, POOL_W + 2 * CONV_W), lambda i: (i, 0)),
        pl.BlockSpec((HALO, POOL_W + 2 * CONV_W), lambda i: (jnp.maximum(i * hb - 1, 0), 0)),
        pl.BlockSpec((HALO, POOL_W + 2 * CONV_W), lambda i: (jnp.minimum((i + 1) * hb, nt * hb - 1), 0)),
        pl.BlockSpec((None, NA_HEADS // 2, N_QGRP, 1, 2 * QGRP, NKEY),
                     lambda i: (layer, 0, 0, variant(i), 0, 0)),
        const((POOL_W, POOL_W)), const((1, POOL_W)),
        const((CONV_K, CONV_W)), const((1, CONV_W)), const((1, CONV_W)), const((1, CONV_W)),
        const((CONV_W, CONV_W)), const((1, CONV_W)),
        const((D_MODEL, D_MODEL)), const((1, D_MODEL)), const((1, D_MODEL)), const((1, D_MODEL)),
        const((N_EXPERTS, D_MODEL)), const((N_EXPERTS, 1)), const((TQ, TQ)),
    ]
    out_specs = [
        pl.BlockSpec((TQ, D_MODEL), lambda i: (i, 0)),
        pl.BlockSpec((TOP_K, TQ), lambda i: (0, i)),
        pl.BlockSpec((TOP_K, TQ), lambda i: (0, i)),
        pl.BlockSpec((TOP_K, TQ), lambda i: (0, i)),
        pl.BlockSpec((N_EXPERTS, LANES), lambda i: (0, 0)),
    ]
    out_shape = [
        jax.ShapeDtypeStruct((n, D_MODEL), F32),
        jax.ShapeDtypeStruct((TOP_K, n), I32),
        jax.ShapeDtypeStruct((TOP_K, n), F32),
        jax.ShapeDtypeStruct((TOP_K, n), I32),
        jax.ShapeDtypeStruct((N_EXPERTS, LANES), I32),
    ]
    return pl.pallas_call(
        functools.partial(_mixer_kernel, tps, alpha),
        grid=(nt,),
        in_specs=in_specs,
        out_specs=out_specs,
        out_shape=out_shape,
        scratch_shapes=[
            pltpu.VMEM((TQ + 2 * HALO, POOL_W + 2 * CONV_W), F32),
            pltpu.VMEM((SUBLANES - 1, TQ + 2 * HALO - SUBLANES, CONV_W), F32),
            pltpu.VMEM((N_EXPERTS, 1), F32),
        ],
        compiler_params=pltpu.CompilerParams(
            dimension_semantics=("arbitrary",), vmem_limit_bytes=VMEM_LIMIT),
        name="mixer",
    )(x, qkv, qkv, qkv, qkv, qkv, qkv, qkv, rest, rest, rest, bias_tab,
      wpool_bd, pscale, dw, dwb, clng, clnb, wpw, bpw, wout, bout, lng, lnb, wr_t, br, tri)


PAD_BLOCKS = (8, 16, 32, 64, 128, 256)
assert PAD_BLOCKS[0] == SUBLANES and TM_E == 2 * PAD_BLOCKS[-1]


def _dispatch_kernel(n_tiles, pad_lo_ref, pad_hi_ref, pos_ref, x_ref, xs_hbm, zero_ref, ring, sem, pad_sem):
    i = pl.program_id(0)

    def pad_fill(act):
        for e in range(N_EXPERTS):
            lo = pad_lo_ref[e]
            hi = pad_hi_ref[e]
            head_end = jnp.minimum((lo + (SUBLANES - 1)) & -SUBLANES, hi)
            for j in range(SUBLANES - 1):
                @pl.when(lo + j < head_end)
                def _(j=j, lo=lo):
                    act(pltpu.make_async_copy(
                        zero_ref.at[pl.ds(0, 1), :], xs_hbm.at[pl.ds(lo + j, 1), :], pad_sem))
            a = head_end
            for b in PAD_BLOCKS:
                take = (a & b) != 0

                @pl.when(take)
                def _(a=a, b=b):
                    act(pltpu.make_async_copy(
                        zero_ref.at[pl.ds(0, b), :], xs_hbm.at[pl.ds(pl.multiple_of(a, SUBLANES), b), :], pad_sem))
                a = jnp.where(take, a + b, a)
        blk = PAD_BLOCKS[-1]

        def tail(c, carry):
            act(pltpu.make_async_copy(
                zero_ref, xs_hbm.at[pl.ds(pl.multiple_of(c * blk, blk), blk), :], pad_sem))
            return carry
        lax.fori_loop(pad_hi_ref[N_EXPERTS - 1] // blk, xs_hbm.shape[0] // blk, tail, 0)

    @pl.when(i == 0)
    def _():
        zero_ref[...] = jnp.zeros_like(zero_ref)
        pad_fill(lambda c: c.start())

    def wait_rows(slot):
        for k in range(TOP_K):
            pltpu.make_async_copy(ring.at[slot], xs_hbm.at[pl.ds(0, TD), :], sem.at[slot]).wait()

    for slot in range(2):
        @pl.when(i % 2 == slot)
        def _(slot=slot):
            ring[slot] = x_ref[...]
            for k in range(TOP_K):
                for r in range(TD):
                    pltpu.make_async_copy(
                        ring.at[slot, pl.ds(r, 1), :], xs_hbm.at[pl.ds(pos_ref[k, r], 1), :],
                        sem.at[slot]).start(priority=r % 2)

            @pl.when(i > 0)
            def _():
                wait_rows(1 - slot)

            @pl.when(i == n_tiles - 1)
            def _():
                wait_rows(slot)

    @pl.when(i == 0)
    def _():
        pad_fill(lambda c: c.wait())


def _dispatch(x1, pos, pad_lo, pad_hi, m_pad):
    n = x1.shape[0]
    grid_spec = pltpu.PrefetchScalarGridSpec(
        num_scalar_prefetch=2,
        grid=(n // TD,),
        in_specs=[
            pl.BlockSpec((TOP_K, TD), lambda i, *_: (0, i), memory_space=pltpu.SMEM),
            pl.BlockSpec((TD, D_MODEL), lambda i, *_: (i, 0)),
        ],
        out_specs=pl.BlockSpec(memory_space=pl.ANY),
        scratch_shapes=[
            pltpu.VMEM((PAD_BLOCKS[-1], D_MODEL), F32),
            pltpu.VMEM((2, TD, D_MODEL), F32),
            pltpu.SemaphoreType.DMA((2,)),
            pltpu.SemaphoreType.DMA(()),
        ],
    )
    return pl.pallas_call(
        functools.partial(_dispatch_kernel, n // TD),
        grid_spec=grid_spec,
        out_shape=jax.ShapeDtypeStruct((m_pad, D_MODEL), F32),
        compiler_params=pltpu.CompilerParams(dimension_semantics=("arbitrary",)),
        name="dispatch",
    )(pad_lo, pad_hi, pos, x1)


def _experts_kernel(layer, te_ref, na_ref, first_ref, slot_ref, nxt_ref, rows_ref,
                    xs_ref, bg_ref, bu_ref, bd_ref, wg_hbm, wu_hbm, wd_hbm,
                    y_ref, wbuf, wg_bf, wu_bf, wd_bf, sem):
    t = pl.program_id(0)
    active = t < na_ref[0]
    e = te_ref[t]
    s = slot_ref[t]

    def fetch(expert, slot):
        return [pltpu.make_async_copy(w.at[layer, expert], wbuf.at[j, slot], sem.at[j, slot])
                for j, w in enumerate((wg_hbm, wu_hbm, wd_hbm))]

    @pl.when(t == 0)
    def _():
        for c in fetch(e, s):
            c.start()

    @pl.when(jnp.logical_and(active, first_ref[t] == 1))
    def _():
        for c in fetch(e, s):
            c.wait()

        @pl.when(nxt_ref[t] != e)
        def _():
            for c in fetch(nxt_ref[t], 1 - s):
                c.start()

        wg_bf[...] = wbuf[0, s].astype(BF16)
        wu_bf[...] = wbuf[1, s].astype(BF16)
        wd_bf[...] = wbuf[2, s].astype(BF16)

    for chunks in range(1, TM_E // TM_CHUNK + 1):
        m = chunks * TM_CHUNK

        @pl.when(jnp.logical_and(active, rows_ref[t] == chunks))
        def _(m=m):
            x = xs_ref[0:m, :].astype(BF16)
            g = jnp.minimum(jnp.dot(x, wg_bf[...], preferred_element_type=F32) + bg_ref[...], SWIGLU_LIMIT)
            u = jnp.clip(jnp.dot(x, wu_bf[...], preferred_element_type=F32) + bu_ref[...],
                         -SWIGLU_LIMIT, SWIGLU_LIMIT)
            act = (u + 1.0) * g * jax.nn.sigmoid(SWIGLU_ALPHA * g)
            y_ref[0:m, :] = jnp.dot(act.astype(BF16), wd_bf[...], preferred_element_type=F32) + bd_ref[...]
            if m < TM_E:
                y_ref[m:, :] = jnp.zeros((TM_E - m, D_MODEL), F32)

    @pl.when(jnp.logical_not(active))
    def _():
        y_ref[...] = jnp.zeros_like(y_ref)


def _experts(layer, tile_expert, n_active, tile_first, tile_slot, tile_next, tile_chunks,
             xs, wg, bg, wu, bu, wd, bd):
    m_pad = xs.shape[0]
    n_tiles = m_pad // TM_E

    def xmap(t, te, na, *_):
        return (jnp.minimum(t, na[0] - 1), 0)

    def bmap(t, te, *_):
        return (layer, te[t], 0, 0)

    grid_spec = pltpu.PrefetchScalarGridSpec(
        num_scalar_prefetch=6,
        grid=(n_tiles,),
        in_specs=[
            pl.BlockSpec((TM_E, D_MODEL), xmap),
            pl.BlockSpec((None, None, 1, D_MODEL), bmap),
            pl.BlockSpec((None, None, 1, D_MODEL), bmap),
            pl.BlockSpec((None, None, 1, D_MODEL), bmap),
            pl.BlockSpec(memory_space=pl.ANY),
            pl.BlockSpec(memory_space=pl.ANY),
            pl.BlockSpec(memory_space=pl.ANY),
        ],
        out_specs=pl.BlockSpec((TM_E, D_MODEL), lambda t, *_: (t, 0)),
        scratch_shapes=[
            pltpu.VMEM((3, 2, D_MODEL, D_MODEL), F32),
            pltpu.VMEM((D_MODEL, D_MODEL), BF16),
            pltpu.VMEM((D_MODEL, D_MODEL), BF16),
            pltpu.VMEM((D_MODEL, D_MODEL), BF16),
            pltpu.SemaphoreType.DMA((3, 2)),
        ],
    )
    return pl.pallas_call(
        functools.partial(_experts_kernel, layer),
        grid_spec=grid_spec,
        out_shape=jax.ShapeDtypeStruct((m_pad, D_MODEL), F32),
        compiler_params=pltpu.CompilerParams(
            dimension_semantics=("arbitrary",), vmem_limit_bytes=VMEM_LIMIT),
        name="experts",
    )(tile_expert, n_active, tile_first, tile_slot, tile_next, tile_chunks, xs, bg, bu, bd, wg, wu, wd)


def _combine_kernel(n_tiles, alpha, pos_ref, x1_ref, gates_ref, lng_ref, lnb_ref, ys_hbm, out_ref, buf, sem):
    j = pl.program_id(0)

    for slot in range(2):
        @pl.when(jnp.logical_and(j < n_tiles, j % 2 == slot))
        def _(slot=slot):
            for k in range(TOP_K):
                for r in range(TD):
                    pltpu.make_async_copy(
                        ys_hbm.at[pl.ds(pos_ref[k, r], 1), :], buf.at[slot, k, pl.ds(r, 1), :],
                        sem.at[slot]).start(priority=r % 2)

    for slot in range(2):
        @pl.when(jnp.logical_and(j > 0, (j - 1) % 2 == slot))
        def _(slot=slot):
            for k in range(TOP_K):
                pltpu.make_async_copy(ys_hbm.at[pl.ds(0, TD), :], buf.at[slot, k], sem.at[slot]).wait()
            gates = gates_ref[...]
            ffn = buf[slot, 0] * gates[:, 0:1]
            for k in range(1, TOP_K):
                ffn = ffn + buf[slot, k] * gates[:, k:k + 1]
            out_ref[...] = _layer_norm(alpha * x1_ref[...] + ffn, lng_ref[...], lnb_ref[...])


def _combine(pos, x1, gates_tm, lng, lnb, ys, *, alpha):
    n = x1.shape[0]
    n_tiles = n // TD
    req = lambda j: jnp.minimum(j, n_tiles - 1)
    fin = lambda j: jnp.maximum(j - 1, 0)
    return pl.pallas_call(
        functools.partial(_combine_kernel, n_tiles, alpha),
        grid=(n_tiles + 1,),
        in_specs=[
            pl.BlockSpec((TOP_K, TD), lambda j: (0, req(j)), memory_space=pltpu.SMEM),
            pl.BlockSpec((TD, D_MODEL), lambda j: (fin(j), 0)),
            pl.BlockSpec((TD, TOP_K), lambda j: (fin(j), 0)),
            pl.BlockSpec((1, D_MODEL), lambda j: (0, 0)),
            pl.BlockSpec((1, D_MODEL), lambda j: (0, 0)),
            pl.BlockSpec(memory_space=pl.ANY),
        ],
        out_specs=pl.BlockSpec((TD, D_MODEL), lambda j: (fin(j), 0)),
        out_shape=jax.ShapeDtypeStruct((n, D_MODEL), F32),
        scratch_shapes=[pltpu.VMEM((2, TOP_K, TD, D_MODEL), F32), pltpu.SemaphoreType.DMA((2,))],
        compiler_params=pltpu.CompilerParams(
            dimension_semantics=("arbitrary",), vmem_limit_bytes=VMEM_LIMIT),
        name="combine",
    )(pos, x1, gates_tm, lng, lnb, ys)


def kernel(x, w_in, b_in, w_pool, pool_scale, rpb, conv_dw, conv_dw_b, conv_ln_g, conv_ln_b,
           w_conv_pw, b_conv_pw, w_out, b_out, ln1_g, ln1_b, w_router, b_router,
           w_gate, b_gate, w_up, b_up, w_down, b_down, ln2_g, ln2_b):
    batch, seq, d = x.shape
    depth = w_in.shape[0]
    n = batch * seq
    rows = seq // GRID_W
    alpha = (2.0 * depth) ** 0.25
    off_q, off_k, off_v = POOL_W, POOL_W + NA_W, POOL_W + 2 * NA_W
    off_ca = off_v + NA_W
    m_pad = n * TOP_K + N_EXPERTS * TM_E
    n_tiles = m_pad // TM_E

    tri = (np.arange(TQ)[:, None] < np.arange(TQ)[None, :]).astype(np.float32)
    tri = jnp.asarray(tri, BF16)
    row2 = lambda v: v.reshape(1, -1)
    bias_tabs = _attn_bias_tables(rpb, rows)
    experts_iota = jnp.arange(N_EXPERTS, dtype=I32)
    tile_starts = jnp.arange(n_tiles, dtype=I32) * TM_E
    b_gate4 = b_gate.reshape(depth, N_EXPERTS, 1, d)
    b_up4 = b_up.reshape(depth, N_EXPERTS, 1, d)
    b_down4 = b_down.reshape(depth, N_EXPERTS, 1, d)

    h = x.reshape(n, d)
    for l in range(depth):
        wqkv = w_in[l][:, off_q:off_ca].astype(BF16)
        bqkv = row2(b_in[l][off_q:off_ca])
        wrest = jnp.concatenate([w_in[l][:, :off_q], w_in[l][:, off_ca:]], axis=1).astype(BF16)
        brest = row2(jnp.concatenate([b_in[l][:off_q], b_in[l][off_ca:]]))
        qkv, rest = _inproj(h, wqkv, bqkv, wrest, brest)

        wpool_bd = jax.scipy.linalg.block_diag(*[w_pool[l][g] for g in range(len(POOL_WINDOWS))]).astype(BF16)
        x1, top_i, top_p, rank, counts = _mixer(
            h, qkv, rest, bias_tabs, wpool_bd, row2(pool_scale[l]),
            conv_dw[l].reshape(CONV_K, CONV_W), row2(conv_dw_b[l]), row2(conv_ln_g[l]), row2(conv_ln_b[l]),
            w_conv_pw[l].astype(BF16), row2(b_conv_pw[l]),
            w_out[l].astype(BF16), row2(b_out[l]), row2(ln1_g[l]), row2(ln1_b[l]),
            w_router[l].T.astype(BF16), b_router[l].reshape(N_EXPERTS, 1), tri,
            layer=l, seq=seq, alpha=alpha)

        cnt = counts[:, 0]
        cpad = ((cnt + TM_E - 1) // TM_E) * TM_E
        ends = jnp.cumsum(cpad)
        off = ends - cpad
        hot = top_i[None] == experts_iota[:, None, None]
        pos = rank + jnp.sum(jnp.where(hot, off[:, None, None], 0), axis=0)
        n_active = (ends[-1] // TM_E).astype(I32)
        last_start = jnp.minimum(tile_starts, ends[-1] - TM_E)
        tile_expert = jnp.sum((ends[None, :] <= last_start[:, None]).astype(I32), axis=1)
        owns = cnt > 0
        e_slot = (jnp.cumsum(owns.astype(I32)) - 1) % 2
        later = jnp.logical_and(experts_iota[None, :] > experts_iota[:, None], owns[None, :])
        e_next = jnp.min(jnp.where(later, experts_iota[None, :], N_EXPERTS), axis=1)
        e_next = jnp.where(e_next == N_EXPERTS, experts_iota, e_next)
        tile_hot = tile_expert[:, None] == experts_iota[None, :]
        pick = lambda v: jnp.sum(jnp.where(tile_hot, v[None, :], 0), axis=1).astype(I32)
        tile_first = (pick(off) == tile_starts).astype(I32)
        tile_slot = pick(e_slot)
        tile_next = pick(e_next)

        xs = _dispatch(x1, pos, (off + cnt).astype(I32), ends.astype(I32), m_pad)
        real_rows = jnp.clip(pick(off + cnt) - tile_starts, 0, TM_E)
        tile_chunks = (real_rows + TM_CHUNK - 1) // TM_CHUNK
        ys = _experts(l, tile_expert, n_active.reshape(1), tile_first, tile_slot, tile_next, tile_chunks, xs,
                      w_gate, b_gate4, w_up, b_up4, w_down, b_down4)
        h = _combine(pos, x1, top_p.T, row2(ln2_g[l]), row2(ln2_b[l]), ys, alpha=alpha)
    return h.reshape(batch, seq, d)
```

```python
import functools

import jax
import jax.numpy as jnp
import numpy as np
from jax import lax
from jax.experimental import pallas as pl
from jax.experimental.pallas import tpu as pltpu

F32 = jnp.float32
BF16 = jnp.bfloat16
I32 = jnp.int32

D_MODEL = 1024
GRID_W = 64
POOL_W = 256
POOL_WINDOWS = (2, 4, 8, 16)
POOL_GROUP_W = 64
NA_W = 512
NA_HEADS = 8
NA_HEAD_DIM = 64
NA_ROWS = 8
NA_COLS = 16
CONV_W = 256
CONV_K = 31
N_EXPERTS = 32
TOP_K = 4
SWIGLU_ALPHA = 1.702
SWIGLU_LIMIT = 7.0
LN_EPS = 1e-5
NEG_INF = -1e30

LANES = 128
TM_IN = 512
TQ = 256
TQ_ROWS = TQ // GRID_W
HALO = 16
SUBLANES = 8
N_CBLK = GRID_W // SUBLANES
CBLK_ROWS = TQ_ROWS * SUBLANES
N_QGRP = GRID_W // NA_COLS
QGRP = TQ // N_QGRP
KBAND_COLS = 2 * NA_COLS
KBAND = TQ_ROWS * KBAND_COLS
NKEY = 3 * KBAND
PAIR_BATCH = 2
TD = 256
COMBINE_CHUNKS = 4
TM_E = 512
TM_CHUNK = 128
V7X_VMEM_BYTES = 64 * 1024 * 1024
VMEM_LIMIT = V7X_VMEM_BYTES * 7 // 8


def _layer_norm(v, g, b):
    mu = jnp.mean(v, axis=-1, keepdims=True)
    c = v - mu
    var = jnp.mean(c * c, axis=-1, keepdims=True)
    return c * lax.rsqrt(var + LN_EPS) * g + b


def _band_start_cblk(g):
    return min(max(2 * g - 1, 0), N_CBLK - KBAND_COLS // SUBLANES)


def _stored_row_groups():
    return [r * N_CBLK + cb for cb in range(N_CBLK) for r in range(TQ_ROWS)]


def _inproj_kernel(x_ref, wqkv_ref, bqkv_ref, wrest_ref, brest_ref, qkv_ref, rest_ref):
    x = x_ref[...].astype(BF16)
    rest_ref[...] = jnp.dot(x, wrest_ref[...], preferred_element_type=F32) + brest_ref[...]
    groups = [x_ref[pl.ds(tile * TQ + gn * SUBLANES, SUBLANES), :]
              for tile in range(TM_IN // TQ) for gn in _stored_row_groups()]
    xp = jnp.concatenate(groups, axis=0).astype(BF16)
    qkv = jnp.dot(xp, wqkv_ref[...], preferred_element_type=F32) + bqkv_ref[...]
    qkv_ref[...] = qkv.astype(BF16)


def _inproj(x, wqkv, bqkv, wrest, brest):
    n = x.shape[0]
    return pl.pallas_call(
        _inproj_kernel,
        grid=(n // TM_IN,),
        in_specs=[
            pl.BlockSpec((TM_IN, D_MODEL), lambda i: (i, 0)),
            pl.BlockSpec((D_MODEL, 3 * NA_W), lambda i: (0, 0)),
            pl.BlockSpec((1, 3 * NA_W), lambda i: (0, 0)),
            pl.BlockSpec((D_MODEL, POOL_W + 2 * CONV_W), lambda i: (0, 0)),
            pl.BlockSpec((1, POOL_W + 2 * CONV_W), lambda i: (0, 0)),
        ],
        out_specs=[
            pl.BlockSpec((TM_IN, 3 * NA_W), lambda i: (i, 0)),
            pl.BlockSpec((TM_IN, POOL_W + 2 * CONV_W), lambda i: (i, 0)),
        ],
        out_shape=[
            jax.ShapeDtypeStruct((n, 3 * NA_W), BF16),
            jax.ShapeDtypeStruct((n, POOL_W + 2 * CONV_W), F32),
        ],
        compiler_params=pltpu.CompilerParams(
            dimension_semantics=("arbitrary",), vmem_limit_bytes=VMEM_LIMIT),
        name="inproj",
    )(x, wqkv, bqkv, wrest, brest)


def _attn_bias_tables(rpb, rows):
    kr_win = min(NA_ROWS, rows)
    n_tiles = rows // TQ_ROWS
    n_r, n_c = 2 * NA_ROWS - 1, 2 * NA_COLS - 1
    n_layers = rpb.shape[0]
    c_hots, c_oks = [], []
    for g in range(N_QGRP):
        qc = g * NA_COLS + np.arange(NA_COLS)
        kc = _band_start_cblk(g) * SUBLANES + np.arange(KBAND_COLS)
        sc = np.clip(qc - NA_COLS // 2, 0, GRID_W - NA_COLS)[:, None]
        c_oks.append((kc[None] >= sc) & (kc[None] < sc + NA_COLS))
        c_off = np.clip(kc[None] - qc[:, None] + NA_COLS - 1, 0, n_c - 1)
        c_hots.append((c_off[..., None] == np.arange(n_c)).astype(np.float32))
    r_hots, r_oks = [], []
    for tile in (0, 1, n_tiles - 1):
        qr = tile * TQ_ROWS + np.arange(TQ_ROWS)
        slot_tile = np.array([tile - 1, tile, tile + 1])
        kr = slot_tile[:, None] * TQ_ROWS + np.arange(TQ_ROWS)[None, :]
        slot_ok = ((slot_tile >= 0) & (slot_tile < n_tiles))[:, None]
        sr = np.clip(qr - kr_win // 2, 0, rows - kr_win)[:, None, None]
        r_oks.append((kr[None] >= sr) & (kr[None] < sr + kr_win) & slot_ok[None])
        r_off = np.clip(kr[None] - qr[:, None, None] + NA_ROWS - 1, 0, n_r - 1)
        r_hots.append((r_off[..., None] == np.arange(n_r)).astype(np.float32))
    c_hot = jnp.asarray(np.stack(c_hots))
    r_off = np.stack([np.argmax(h, axis=-1) for h in r_hots])
    c_ok = np.stack(c_oks).reshape(N_QGRP, 2, SUBLANES, KBAND_COLS // SUBLANES, SUBLANES)
    r_ok = np.stack(r_oks)
    ok = (r_ok[:, None, None, :, None, :, None, :, None]
          & c_ok[None, :, :, None, :, None, :, None, :])
    ok = np.broadcast_to(ok[:, :, None], (3, N_QGRP, 2) + ok.shape[2:])
    ok = jnp.asarray(ok.reshape(3, N_QGRP, 2 * QGRP, NKEY).astype(np.float32))
    rpb5 = rpb.astype(F32).reshape(n_layers, NA_HEADS // 2, 2, n_r, n_c)
    by_col = jnp.einsum("lpjrc,gmnc->lpgjmrn", rpb5, c_hot, precision=lax.Precision.HIGHEST)
    by_col = by_col.reshape(n_layers * (NA_HEADS // 2) * N_QGRP, 2 * NA_COLS, n_r * KBAND_COLS)
    nb = KBAND_COLS // SUBLANES
    expand = np.zeros((TQ_ROWS, 2, 2, TQ_ROWS, SUBLANES, 2, 2, SUBLANES), np.float32)
    for q in range(TQ_ROWS):
        for j in range(2):
            for a in range(2):
                for x in range(SUBLANES):
                    expand[q, j, a, q, x, j, a, x] = 1.0
    expand = expand.reshape(TQ_ROWS, 2 * QGRP, 2 * NA_COLS)
    select = np.zeros((3, TQ_ROWS, n_r, nb, SUBLANES, 3, nb, TQ_ROWS, SUBLANES), np.float32)
    for v in range(3):
        for q in range(TQ_ROWS):
            for s in range(3):
                for y in range(TQ_ROWS):
                    for b in range(nb):
                        for z in range(SUBLANES):
                            select[v, q, r_off[v, q, s, y], b, z, s, b, y, z] = 1.0
    select = select.reshape(3, TQ_ROWS, n_r * KBAND_COLS, NKEY)
    lane_pad = -(n_r * KBAND_COLS) % LANES
    by_col = jnp.pad(by_col, ((0, 0), (0, 0), (0, lane_pad)))
    select = np.pad(select, ((0, 0), (0, 0), (0, lane_pad), (0, 0)))
    tab = _bias_table_call(by_col, jnp.asarray(expand, BF16), jnp.asarray(select, BF16), ok)
    return tab.reshape(n_layers, NA_HEADS // 2, N_QGRP, 3, 2 * QGRP, NKEY)


def _bias_table_kernel(u_ref, expand_ref, select_ref, ok_ref, out_ref):
    u = u_ref[...]
    parts = []
    for _ in range(3):
        piece = u.astype(BF16)
        parts.append(piece)
        u = u - piece.astype(F32)
    n_k = u.shape[1]
    pieces = jnp.concatenate(parts, axis=1)
    acc = [jnp.zeros((2 * QGRP, NKEY), F32) for _ in range(3)]
    for q in range(TQ_ROWS):
        rows = jnp.dot(expand_ref[q], pieces, preferred_element_type=F32).astype(BF16)
        rows = jnp.concatenate([rows[:, k * n_k:(k + 1) * n_k] for k in range(3)], axis=0)
        for v in range(3):
            sel = jnp.dot(rows, select_ref[v, q], preferred_element_type=F32)
            acc[v] = acc[v] + (sel[0:2 * QGRP] + sel[2 * QGRP:4 * QGRP] + sel[4 * QGRP:])
    for v in range(3):
        out_ref[v] = jnp.where(ok_ref[v, 0] > 0.0, acc[v], NEG_INF)


def _bias_table_call(by_col, expand, select, ok):
    n, n_u, n_k = by_col.shape
    return pl.pallas_call(
        _bias_table_kernel,
        grid=(n,),
        in_specs=[
            pl.BlockSpec((None, n_u, n_k), lambda i: (i, 0, 0)),
            pl.BlockSpec(expand.shape, lambda i: (0, 0, 0)),
            pl.BlockSpec(select.shape, lambda i: (0, 0, 0, 0)),
            pl.BlockSpec((3, 1, 2 * QGRP, NKEY), lambda i: (0, i % N_QGRP, 0, 0)),
        ],
        out_specs=pl.BlockSpec((None, 3, 2 * QGRP, NKEY), lambda i: (i, 0, 0, 0)),
        out_shape=jax.ShapeDtypeStruct((n, 3, 2 * QGRP, NKEY), F32),
        compiler_params=pltpu.CompilerParams(
            dimension_semantics=("arbitrary",), vmem_limit_bytes=VMEM_LIMIT),
        name="bias_table",
    )(by_col, expand, select, ok)


def _mixer_kernel(tiles_per_seq, alpha,
                  x_ref, q_ref, kp_ref, kc_ref, kn_ref, vp_ref, vc_ref, vn_ref,
                  rc_ref, rp_ref, rn_ref, bias_ref,
                  wpool_ref, pscale_ref, dw_ref, dwb_ref, clng_ref, clnb_ref, wpw_ref, bpw_ref,
                  wout_ref, bout_ref, lng_ref, lnb_ref, wr_ref, br_ref, tri_ref,
                  x1_ref, topi_ref, topp_ref, rank_ref, counts_ref,
                  halo_ref, shift_ref, base_ref):
    i = pl.program_id(0)
    ib = i % tiles_per_seq
    has_prev = ib > 0
    has_next = ib < tiles_per_seq - 1

    @pl.when(i == 0)
    def _():
        base_ref[...] = jnp.zeros_like(base_ref)

    halo_ref[0:HALO, :] = jnp.where(has_prev, rp_ref[...], 0.0)
    halo_ref[HALO:HALO + TQ, :] = rc_ref[...]
    halo_ref[HALO + TQ:, :] = jnp.where(has_next, rn_ref[...], 0.0)

    def fill_shifts():
        for b in range(1, SUBLANES):
            shift_ref[b - 1] = halo_ref[pl.ds(b, TQ + 2 * HALO - SUBLANES), 0:CONV_W]

    def window(start):
        a8, b = divmod(start, SUBLANES)
        if b == 0:
            return halo_ref[pl.ds(a8 * SUBLANES, TQ), 0:CONV_W]
        return shift_ref[b - 1, pl.ds(a8 * SUBLANES, TQ), :]

    def pool_and_conv():
        def u_at(off):
            return halo_ref[pl.ds(HALO + off, TQ), 0:POOL_W]

        t_seq = ib * TQ + lax.broadcasted_iota(I32, (TQ, 1), 0)
        seq_len = tiles_per_seq * TQ
        group = lax.broadcasted_iota(I32, (1, POOL_W), 1) // POOL_GROUP_W
        u0 = u_at(0)
        acc = u0
        mean = jnp.zeros((TQ, POOL_W), F32)
        done = 0
        for g, w in enumerate(POOL_WINDOWS):
            half = w // 2
            for o in range(done + 1, half + 1):
                acc = acc + u_at(-o) + (u_at(o - 1) if o > 1 else 0.0)
            done = half
            cnt = (jnp.minimum(t_seq + half, seq_len) - jnp.maximum(t_seq - half, 0)).astype(F32)
            mean = jnp.where(group == g, acc * (1.0 / cnt), mean)
        d = (mean - u0).astype(BF16)
        y_pool = jnp.dot(d, wpool_ref[...], preferred_element_type=F32) * pscale_ref[...]
        part = jnp.dot(y_pool.astype(BF16), wout_ref[0:POOL_W, :], preferred_element_type=F32)

        a = halo_ref[:, POOL_W:POOL_W + CONV_W]
        gate = halo_ref[:, POOL_W + CONV_W:]
        halo_ref[:, 0:CONV_W] = a * jax.nn.sigmoid(gate)
        fill_shifts()
        conv = jnp.zeros((TQ, CONV_W), F32) + dwb_ref[...]
        for k in range(CONV_K):
            conv = conv + window(HALO - CONV_K // 2 + k) * dw_ref[k:k + 1, :]
        hc = _layer_norm(conv, clng_ref[...], clnb_ref[...])
        hc = hc * jax.nn.sigmoid(hc)
        y_conv = jnp.dot(hc.astype(BF16), wpw_ref[...], preferred_element_type=F32) + bpw_ref[...]
        return part + jnp.dot(y_conv.astype(BF16), wout_ref[POOL_W + NA_W:, :], preferred_element_type=F32)

    mix = jnp.zeros((TQ, D_MODEL), F32)

    low = lax.broadcasted_iota(I32, (QGRP, LANES), 1) < NA_HEAD_DIM
    scale = NA_HEAD_DIM ** -0.5
    n_pairs = NA_HEADS // 2
    bands = [pl.ds(_band_start_cblk(g) * CBLK_ROWS, KBAND) for g in range(N_QGRP)]
    col = lambda p: slice(p * LANES, (p + 1) * LANES)
    blk_rows = 2 * QGRP
    for p0 in range(0, n_pairs, PAIR_BATCH):
        pairs = range(p0, p0 + PAIR_BATCH)
        s_blk = []
        for p in pairs:
            for g in range(N_QGRP):
                qp = q_ref[pl.ds(g * QGRP, QGRP), col(p)].astype(F32) * scale
                qs = jnp.concatenate([jnp.where(low, qp, 0.0), jnp.where(low, 0.0, qp)], axis=0).astype(BF16)
                kk = jnp.concatenate(
                    [kp_ref[bands[g], col(p)], kc_ref[bands[g], col(p)], kn_ref[bands[g], col(p)]], axis=0)
                s_blk.append(lax.dot_general(qs, kk, (((1,), (1,)), ((), ())), preferred_element_type=F32))
        bias = bias_ref[p0:p0 + PAIR_BATCH].reshape(PAIR_BATCH * N_QGRP * blk_rows, NKEY)
        s = jnp.concatenate(s_blk, axis=0) + bias
        m = jnp.max(s, axis=-1, keepdims=True)
        e = jnp.exp(s - m)
        rl = 1.0 / jnp.sum(e, axis=-1, keepdims=True)
        e = e.astype(BF16)
        for p in pairs:
            o_grp = []
            for g in range(N_QGRP):
                vv = jnp.concatenate(
                    [vp_ref[bands[g], col(p)], vc_ref[bands[g], col(p)], vn_ref[bands[g], col(p)]], axis=0)
                blk = (p - p0) * N_QGRP + g
                rows = slice(blk * blk_rows, (blk + 1) * blk_rows)
                o = jnp.dot(e[rows], vv, preferred_element_type=F32) * rl[rows]
                o_grp.append(jnp.where(low, o[0:QGRP], o[QGRP:]))
            o_pair = jnp.concatenate(
                [o_grp[cb // 2][(cb % 2) * CBLK_ROWS + r * SUBLANES:(cb % 2) * CBLK_ROWS + (r + 1) * SUBLANES]
                 for r in range(TQ_ROWS) for cb in range(N_CBLK)], axis=0)
            r0 = POOL_W + p * LANES
            mix = mix + jnp.dot(o_pair.astype(BF16), wout_ref[r0:r0 + LANES, :], preferred_element_type=F32)
    mix = mix + pool_and_conv()

    x1 = _layer_norm(alpha * x_ref[...] + mix + bout_ref[...], lng_ref[...], lnb_ref[...])
    x1_ref[...] = x1

    lt = lax.dot_general(wr_ref[...], x1.astype(BF16), (((1,), (1,)), ((), ())),
                         preferred_element_type=F32) + br_ref[...]
    eidx = lax.broadcasted_iota(I32, (N_EXPERTS, TQ), 0)
    work = lt
    vals, idxs = [], []
    for _ in range(TOP_K):
        mk = jnp.max(work, axis=0, keepdims=True)
        ik = jnp.min(jnp.where(work == mk, eidx, N_EXPERTS), axis=0, keepdims=True)
        vals.append(mk)
        idxs.append(ik)
        work = jnp.where(eidx == ik, -jnp.inf, work)
    ex = [jnp.exp(v - vals[0]) for v in vals]
    den = ex[0] + ex[1] + ex[2] + ex[3]
    topp_ref[...] = jnp.concatenate([e_ / den for e_ in ex], axis=0)
    topi_ref[...] = jnp.concatenate(idxs, axis=0)

    run = base_ref[...]
    ranks = []
    for k in range(TOP_K):
        hot = (eidx == idxs[k]).astype(F32)
        before = jnp.dot(hot.astype(BF16), tri_ref[...], preferred_element_type=F32)
        ranks.append(jnp.sum(hot * (run + before), axis=0, keepdims=True))
        run = run + jnp.sum(hot, axis=1, keepdims=True)
    rank_ref[...] = jnp.concatenate(ranks, axis=0).astype(I32)
    base_ref[...] = run
    counts_ref[...] = jnp.broadcast_to(run, counts_ref.shape).astype(I32)


def _mixer(x, qkv, rest, bias_tab, wpool_bd, pscale, dw, dwb, clng, clnb, wpw, bpw,
           wout, bout, lng, lnb, wr_t, br, tri, *, layer, seq, alpha):
    n = x.shape[0]
    nt = n // TQ
    tps = seq // TQ
    hb = TQ // HALO

    def prev_t(i):
        return jnp.where(i % tps == 0, i, i - 1)

    def next_t(i):
        return jnp.where(i % tps == tps - 1, i, i + 1)

    def variant(i):
        ib = i % tps
        return jnp.where(ib == 0, 0, jnp.where(ib == tps - 1, 2, 1))

    def const(shape):
        return pl.BlockSpec(shape, lambda i: tuple(0 for _ in shape))

    in_specs = [
        pl.BlockSpec((TQ, D_MODEL), lambda i: (i, 0)),
        pl.BlockSpec((TQ, NA_W), lambda i: (i, 0)),
        pl.BlockSpec((TQ, NA_W), lambda i: (prev_t(i), 1)),
        pl.BlockSpec((TQ, NA_W), lambda i: (i, 1)),
        pl.BlockSpec((TQ, NA_W), lambda i: (next_t(i), 1)),
        pl.BlockSpec((TQ, NA_W), lambda i: (prev_t(i), 2)),
        pl.BlockSpec((TQ, NA_W), lambda i: (i, 2)),
        pl.BlockSpec((TQ, NA_W), lambda i: (next_t(i), 2)),
        pl.BlockSpec((TQ, POOL_W + 2 * CONV_W), lambda i: (i, 0)),
        pl.BlockSpec((HALO, POOL_W + 2 * CONV_W), lambda i: (jnp.maximum(i * hb - 1, 0), 0)),
        pl.BlockSpec((HALO, POOL_W + 2 * CONV_W), lambda i: (jnp.minimum((i + 1) * hb, nt * hb - 1), 0)),
        pl.BlockSpec((None, NA_HEADS // 2, N_QGRP, 1, 2 * QGRP, NKEY),
                     lambda i: (layer, 0, 0, variant(i), 0, 0)),
        const((POOL_W, POOL_W)), const((1, POOL_W)),
        const((CONV_K, CONV_W)), const((1, CONV_W)), const((1, CONV_W)), const((1, CONV_W)),
        const((CONV_W, CONV_W)), const((1, CONV_W)),
        const((D_MODEL, D_MODEL)), const((1, D_MODEL)), const((1, D_MODEL)), const((1, D_MODEL)),
        const((N_EXPERTS, D_MODEL)), const((N_EXPERTS, 1)), const((TQ, TQ)),
    ]
    out_specs = [
        pl.BlockSpec((TQ, D_MODEL), lambda i: (i, 0)),
        pl.BlockSpec((TOP_K, TQ), lambda i: (0, i)),
        pl.BlockSpec((TOP_K, TQ), lambda i: (0, i)),
        pl.BlockSpec((TOP_K, TQ), lambda i: (0, i)),
        pl.BlockSpec((N_EXPERTS, LANES), lambda i: (0, 0)),
    ]
    out_shape = [
        jax.ShapeDtypeStruct((n, D_MODEL), F32),
        jax.ShapeDtypeStruct((TOP_K, n), I32),
        jax.ShapeDtypeStruct((TOP_K, n), F32),
        jax.ShapeDtypeStruct((TOP_K, n), I32),
        jax.ShapeDtypeStruct((N_EXPERTS, LANES), I32),
    ]
    return pl.pallas_call(
        functools.partial(_mixer_kernel, tps, alpha),
        grid=(nt,),
        in_specs=in_specs,
        out_specs=out_specs,
        out_shape=out_shape,
        scratch_shapes=[
            pltpu.VMEM((TQ + 2 * HALO, POOL_W + 2 * CONV_W), F32),
            pltpu.VMEM((SUBLANES - 1, TQ + 2 * HALO - SUBLANES, CONV_W), F32),
            pltpu.VMEM((N_EXPERTS, 1), F32),
        ],
        compiler_params=pltpu.CompilerParams(
            dimension_semantics=("arbitrary",), vmem_limit_bytes=VMEM_LIMIT),
        name="mixer",
    )(x, qkv, qkv, qkv, qkv, qkv, qkv, qkv, rest, rest, rest, bias_tab,
      wpool_bd, pscale, dw, dwb, clng, clnb, wpw, bpw, wout, bout, lng, lnb, wr_t, br, tri)


PAD_BLOCKS = (8, 16, 32, 64, 128, 256)
assert PAD_BLOCKS[0] == SUBLANES and TM_E == 2 * PAD_BLOCKS[-1]


def _dispatch_kernel(n_tiles, pad_lo_ref, pad_hi_ref, pos_ref, x_ref, xs_hbm, zero_ref, ring, sem, pad_sem):
    i = pl.program_id(0)

    def pad_fill(act):
        for e in range(N_EXPERTS):
            lo = pad_lo_ref[e]
            hi = pad_hi_ref[e]
            head_end = jnp.minimum((lo + (SUBLANES - 1)) & -SUBLANES, hi)
            for j in range(SUBLANES - 1):
                @pl.when(lo + j < head_end)
                def _(j=j, lo=lo):
                    act(pltpu.make_async_copy(
                        zero_ref.at[pl.ds(0, 1), :], xs_hbm.at[pl.ds(lo + j, 1), :], pad_sem))
            a = head_end
            for b in PAD_BLOCKS:
                take = (a & b) != 0

                @pl.when(take)
                def _(a=a, b=b):
                    act(pltpu.make_async_copy(
                        zero_ref.at[pl.ds(0, b), :], xs_hbm.at[pl.ds(pl.multiple_of(a, SUBLANES), b), :], pad_sem))
                a = jnp.where(take, a + b, a)
        blk = PAD_BLOCKS[-1]

        def tail(c, carry):
            act(pltpu.make_async_copy(
                zero_ref, xs_hbm.at[pl.ds(pl.multiple_of(c * blk, blk), blk), :], pad_sem))
            return carry
        lax.fori_loop(pad_hi_ref[N_EXPERTS - 1] // blk, xs_hbm.shape[0] // blk, tail, 0)

    @pl.when(i == 0)
    def _():
        zero_ref[...] = jnp.zeros_like(zero_ref)
        pad_fill(lambda c: c.start())

    def wait_rows(slot):
        for k in range(TOP_K):
            pltpu.make_async_copy(ring.at[slot], xs_hbm.at[pl.ds(0, TD), :], sem.at[slot]).wait()

    for slot in range(2):
        @pl.when(i % 2 == slot)
        def _(slot=slot):
            ring[slot] = x_ref[...]
            for k in range(TOP_K):
                for r in range(TD):
                    pltpu.make_async_copy(
                        ring.at[slot, pl.ds(r, 1), :], xs_hbm.at[pl.ds(pos_ref[k, r], 1), :],
                        sem.at[slot]).start(priority=r % 2)

            @pl.when(i > 0)
            def _():
                wait_rows(1 - slot)

            @pl.when(i == n_tiles - 1)
            def _():
                wait_rows(slot)

    @pl.when(i == 0)
    def _():
        pad_fill(lambda c: c.wait())


def _dispatch(x1, pos, pad_lo, pad_hi, m_pad):
    n = x1.shape[0]
    grid_spec = pltpu.PrefetchScalarGridSpec(
        num_scalar_prefetch=2,
        grid=(n // TD,),
        in_specs=[
            pl.BlockSpec((TOP_K, TD), lambda i, *_: (0, i), memory_space=pltpu.SMEM),
            pl.BlockSpec((TD, D_MODEL), lambda i, *_: (i, 0)),
        ],
        out_specs=pl.BlockSpec(memory_space=pl.ANY),
        scratch_shapes=[
            pltpu.VMEM((PAD_BLOCKS[-1], D_MODEL), F32),
            pltpu.VMEM((2, TD, D_MODEL), F32),
            pltpu.SemaphoreType.DMA((2,)),
            pltpu.SemaphoreType.DMA(()),
        ],
    )
    return pl.pallas_call(
        functools.partial(_dispatch_kernel, n // TD),
        grid_spec=grid_spec,
        out_shape=jax.ShapeDtypeStruct((m_pad, D_MODEL), F32),
        compiler_params=pltpu.CompilerParams(dimension_semantics=("arbitrary",)),
        name="dispatch",
    )(pad_lo, pad_hi, pos, x1)


def _experts_kernel(layer, te_ref, na_ref, first_ref, slot_ref, nxt_ref, rows_ref,
                    xs_ref, bg_ref, bu_ref, bd_ref, wg_hbm, wu_hbm, wd_hbm,
                    y_ref, wbuf, wg_bf, wu_bf, wd_bf, sem):
    t = pl.program_id(0)
    active = t < na_ref[0]
    e = te_ref[t]
    s = slot_ref[t]

    def fetch(expert, slot):
        return [pltpu.make_async_copy(w.at[layer, expert], wbuf.at[j, slot], sem.at[j, slot])
                for j, w in enumerate((wg_hbm, wu_hbm, wd_hbm))]

    @pl.when(t == 0)
    def _():
        for c in fetch(e, s):
            c.start()

    @pl.when(jnp.logical_and(active, first_ref[t] == 1))
    def _():
        for c in fetch(e, s):
            c.wait()

        @pl.when(nxt_ref[t] != e)
        def _():
            for c in fetch(nxt_ref[t], 1 - s):
                c.start()

        wg_bf[...] = wbuf[0, s].astype(BF16)
        wu_bf[...] = wbuf[1, s].astype(BF16)
        wd_bf[...] = wbuf[2, s].astype(BF16)

    for chunks in range(1, TM_E // TM_CHUNK + 1):
        m = chunks * TM_CHUNK

        @pl.when(jnp.logical_and(active, rows_ref[t] == chunks))
        def _(m=m):
            x = xs_ref[0:m, :].astype(BF16)
            g = jnp.minimum(jnp.dot(x, wg_bf[...], preferred_element_type=F32) + bg_ref[...], SWIGLU_LIMIT)
            u = jnp.clip(jnp.dot(x, wu_bf[...], preferred_element_type=F32) + bu_ref[...],
                         -SWIGLU_LIMIT, SWIGLU_LIMIT)
            act = (u + 1.0) * g * jax.nn.sigmoid(SWIGLU_ALPHA * g)
            y_ref[0:m, :] = jnp.dot(act.astype(BF16), wd_bf[...], preferred_element_type=F32) + bd_ref[...]
            if m < TM_E:
                y_ref[m:, :] = jnp.zeros((TM_E - m, D_MODEL), F32)

    @pl.when(jnp.logical_not(active))
    def _():
        y_ref[...] = jnp.zeros_like(y_ref)


def _experts(layer, tile_expert, n_active, tile_first, tile_slot, tile_next, tile_chunks,
             xs, wg, bg, wu, bu, wd, bd):
    m_pad = xs.shape[0]
    n_tiles = m_pad // TM_E

    def xmap(t, te, na, *_):
        return (jnp.minimum(t, na[0] - 1), 0)

    def bmap(t, te, *_):
        return (layer, te[t], 0, 0)

    grid_spec = pltpu.PrefetchScalarGridSpec(
        num_scalar_prefetch=6,
        grid=(n_tiles,),
        in_specs=[
            pl.BlockSpec((TM_E, D_MODEL), xmap),
            pl.BlockSpec((None, None, 1, D_MODEL), bmap),
            pl.BlockSpec((None, None, 1, D_MODEL), bmap),
            pl.BlockSpec((None, None, 1, D_MODEL), bmap),
            pl.BlockSpec(memory_space=pl.ANY),
            pl.BlockSpec(memory_space=pl.ANY),
            pl.BlockSpec(memory_space=pl.ANY),
        ],
        out_specs=pl.BlockSpec((TM_E, D_MODEL), lambda t, *_: (t, 0)),
        scratch_shapes=[
            pltpu.VMEM((3, 2, D_MODEL, D_MODEL), F32),
            pltpu.VMEM((D_MODEL, D_MODEL), BF16),
            pltpu.VMEM((D_MODEL, D_MODEL), BF16),
            pltpu.VMEM((D_MODEL, D_MODEL), BF16),
            pltpu.SemaphoreType.DMA((3, 2)),
        ],
    )
    return pl.pallas_call(
        functools.partial(_experts_kernel, layer),
        grid_spec=grid_spec,
        out_shape=jax.ShapeDtypeStruct((m_pad, D_MODEL), F32),
        compiler_params=pltpu.CompilerParams(
            dimension_semantics=("arbitrary",), vmem_limit_bytes=VMEM_LIMIT),
        name="experts",
    )(tile_expert, n_active, tile_first, tile_slot, tile_next, tile_chunks, xs, bg, bu, bd, wg, wu, wd)


def _combine_kernel(n_tiles, alpha, pos_ref, x1_ref, gates_ref, lng_ref, lnb_ref, ys_hbm, out_ref, buf, sem):
    j = pl.program_id(0)
    n_chunks = COMBINE_CHUNKS
    rows_c = TD // n_chunks

    def request(slot, c):
        for k in range(TOP_K):
            for r in range(c * rows_c, (c + 1) * rows_c):
                pltpu.make_async_copy(
                    ys_hbm.at[pl.ds(pos_ref[k, r], 1), :], buf.at[slot, k, pl.ds(r, 1), :],
                    sem.at[slot]).start(priority=r % 2)

    def arrived(slot):
        for k in range(TOP_K):
            pltpu.make_async_copy(ys_hbm.at[pl.ds(0, TD), :], buf.at[slot, k], sem.at[slot]).wait()

    def finish(slot, c):
        rows = pl.ds(c * rows_c, rows_c)
        gates = gates_ref[rows, :]
        ffn = buf[slot, 0, rows, :] * gates[:, 0:1]
        for k in range(1, TOP_K):
            ffn = ffn + buf[slot, k, rows, :] * gates[:, k:k + 1]
        out_ref[rows, :] = _layer_norm(alpha * x1_ref[rows, :] + ffn, lng_ref[...], lnb_ref[...])

    @pl.when(j == 0)
    def _():
        for c in range(n_chunks):
            request(0, c)

    for slot in range(2):
        @pl.when(jnp.logical_and(jnp.logical_and(j > 0, j < n_tiles), j % 2 == slot))
        def _(slot=slot):
            arrived(1 - slot)
            for c in range(n_chunks):
                request(slot, c)
                finish(1 - slot, c)

        @pl.when(jnp.logical_and(j == n_tiles, (j - 1) % 2 == slot))
        def _(slot=slot):
            arrived(slot)
            for c in range(n_chunks):
                finish(slot, c)


def _combine(pos, x1, gates_tm, lng, lnb, ys, *, alpha):
    n = x1.shape[0]
    n_tiles = n // TD
    req = lambda j: jnp.minimum(j, n_tiles - 1)
    fin = lambda j: jnp.maximum(j - 1, 0)
    return pl.pallas_call(
        functools.partial(_combine_kernel, n_tiles, alpha),
        grid=(n_tiles + 1,),
        in_specs=[
            pl.BlockSpec((TOP_K, TD), lambda j: (0, req(j)), memory_space=pltpu.SMEM),
            pl.BlockSpec((TD, D_MODEL), lambda j: (fin(j), 0)),
            pl.BlockSpec((TD, TOP_K), lambda j: (fin(j), 0)),
            pl.BlockSpec((1, D_MODEL), lambda j: (0, 0)),
            pl.BlockSpec((1, D_MODEL), lambda j: (0, 0)),
            pl.BlockSpec(memory_space=pl.ANY),
        ],
        out_specs=pl.BlockSpec((TD, D_MODEL), lambda j: (fin(j), 0)),
        out_shape=jax.ShapeDtypeStruct((n, D_MODEL), F32),
        scratch_shapes=[pltpu.VMEM((2, TOP_K, TD, D_MODEL), F32), pltpu.SemaphoreType.DMA((2,))],
        compiler_params=pltpu.CompilerParams(
            dimension_semantics=("arbitrary",), vmem_limit_bytes=VMEM_LIMIT),
        name="combine",
    )(pos, x1, gates_tm, lng, lnb, ys)


def kernel(x, w_in, b_in, w_pool, pool_scale, rpb, conv_dw, conv_dw_b, conv_ln_g, conv_ln_b,
           w_conv_pw, b_conv_pw, w_out, b_out, ln1_g, ln1_b, w_router, b_router,
           w_gate, b_gate, w_up, b_up, w_down, b_down, ln2_g, ln2_b):
    batch, seq, d = x.shape
    depth = w_in.shape[0]
    n = batch * seq
    rows = seq // GRID_W
    alpha = (2.0 * depth) ** 0.25
    off_q, off_k, off_v = POOL_W, POOL_W + NA_W, POOL_W + 2 * NA_W
    off_ca = off_v + NA_W
    m_pad = n * TOP_K + N_EXPERTS * TM_E
    n_tiles = m_pad // TM_E

    tri = (np.arange(TQ)[:, None] < np.arange(TQ)[None, :]).astype(np.float32)
    tri = jnp.asarray(tri, BF16)
    row2 = lambda v: v.reshape(1, -1)
    bias_tabs = _attn_bias_tables(rpb, rows)
    experts_iota = jnp.arange(N_EXPERTS, dtype=I32)
    tile_starts = jnp.arange(n_tiles, dtype=I32) * TM_E
    b_gate4 = b_gate.reshape(depth, N_EXPERTS, 1, d)
    b_up4 = b_up.reshape(depth, N_EXPERTS, 1, d)
    b_down4 = b_down.reshape(depth, N_EXPERTS, 1, d)

    h = x.reshape(n, d)
    for l in range(depth):
        wqkv = w_in[l][:, off_q:off_ca].astype(BF16)
        bqkv = row2(b_in[l][off_q:off_ca])
        wrest = jnp.concatenate([w_in[l][:, :off_q], w_in[l][:, off_ca:]], axis=1).astype(BF16)
        brest = row2(jnp.concatenate([b_in[l][:off_q], b_in[l][off_ca:]]))
        qkv, rest = _inproj(h, wqkv, bqkv, wrest, brest)

        wpool_bd = jax.scipy.linalg.block_diag(*[w_pool[l][g] for g in range(len(POOL_WINDOWS))]).astype(BF16)
        x1, top_i, top_p, rank, counts = _mixer(
            h, qkv, rest, bias_tabs, wpool_bd, row2(pool_scale[l]),
            conv_dw[l].reshape(CONV_K, CONV_W), row2(conv_dw_b[l]), row2(conv_ln_g[l]), row2(conv_ln_b[l]),
            w_conv_pw[l].astype(BF16), row2(b_conv_pw[l]),
            w_out[l].astype(BF16), row2(b_out[l]), row2(ln1_g[l]), row2(ln1_b[l]),
            w_router[l].T.astype(BF16), b_router[l].reshape(N_EXPERTS, 1), tri,
            layer=l, seq=seq, alpha=alpha)

        cnt = counts[:, 0]
        cpad = ((cnt + TM_E - 1) // TM_E) * TM_E
        ends = jnp.cumsum(cpad)
        off = ends - cpad
        hot = top_i[None] == experts_iota[:, None, None]
        pos = rank + jnp.sum(jnp.where(hot, off[:, None, None], 0), axis=0)
        n_active = (ends[-1] // TM_E).astype(I32)
        last_start = jnp.minimum(tile_starts, ends[-1] - TM_E)
        tile_expert = jnp.sum((ends[None, :] <= last_start[:, None]).astype(I32), axis=1)
        owns = cnt > 0
        e_slot = (jnp.cumsum(owns.astype(I32)) - 1) % 2
        later = jnp.logical_and(experts_iota[None, :] > experts_iota[:, None], owns[None, :])
        e_next = jnp.min(jnp.where(later, experts_iota[None, :], N_EXPERTS), axis=1)
        e_next = jnp.where(e_next == N_EXPERTS, experts_iota, e_next)
        tile_hot = tile_expert[:, None] == experts_iota[None, :]
        pick = lambda v: jnp.sum(jnp.where(tile_hot, v[None, :], 0), axis=1).astype(I32)
        tile_first = (pick(off) == tile_starts).astype(I32)
        tile_slot = pick(e_slot)
        tile_next = pick(e_next)

        xs = _dispatch(x1, pos, (off + cnt).astype(I32), ends.astype(I32), m_pad)
        real_rows = jnp.clip(pick(off + cnt) - tile_starts, 0, TM_E)
        tile_chunks = (real_rows + TM_CHUNK - 1) // TM_CHUNK
        ys = _experts(l, tile_expert, n_active.reshape(1), tile_first, tile_slot, tile_next, tile_chunks, xs,
                      w_gate, b_gate4, w_up, b_up4, w_down, b_down4)
        h = _combine(pos, x1, top_p.T, row2(ln2_g[l]), row2(ln2_b[l]), ys, alpha=alpha)
    return h.reshape(batch, seq, d)
```
